```python
import jax, jax.numpy as jnp
from jax import lax
import numpy as np

D_MODEL = 1024
BATCH = 8
SEQ = 4096
DEPTH = 2

CTX_LEN = 256
GRID_W = 64

ATT_HEADS = 8
ATT_KV_HEADS = 2
ATT_GROUP = ATT_HEADS // ATT_KV_HEADS
ATT_HEAD_DIM = 64
WINDOW = 128
ATT_BLOCK = 128
BAND_SIDE = WINDOW // ATT_BLOCK
ROPE_BASE = 10000.0

ML_HEADS = 4
ML_DK = 64
ML_DV = 128
ML_CONV_W = 3

HG_HEADS = 4
HG_DK = 64
HG_DV = 128

CHUNK = 64
D_FF = ((8 * D_MODEL // 3 + 127) // 128) * 128
N_BRANCH = 3
N_SUB = 3
N_MOD = 3 * N_SUB
EPS = 1e-6

ATT_Q = ATT_HEADS * ATT_HEAD_DIM
ATT_KV = ATT_KV_HEADS * ATT_HEAD_DIM
ML_QK = ML_HEADS * ML_DK
ML_V = ML_HEADS * ML_DV
HG_K = HG_HEADS * HG_DK
HG_V = HG_HEADS * HG_DV
COLUMN_SIZES = (ATT_Q, ATT_KV, ATT_KV,
                ML_QK, ML_QK, ML_V, ML_V, 2 * ML_HEADS, 2 * ML_HEADS,
                HG_K, HG_K, HG_K, HG_V, HG_V,
                N_BRANCH * D_MODEL)
D_IN = sum(COLUMN_SIZES)

kernel_name = "hybrid_swa_mlstm_hgrn2_diffusion_block"


def rms_norm(x, g):
    x32 = x.astype(jnp.float32)
    y = x32 * lax.rsqrt(jnp.mean(x32 * x32, axis=-1, keepdims=True) + EPS)
    return (y * g.astype(jnp.float32)).astype(x.dtype)


def ada_pre(x, mod, j, g):
    return rms_norm(x, g) * (1.0 + mod[..., 3 * j + 1, :]) + mod[..., 3 * j, :]


def ada_post(x, y, mod, j, g, w):
    return x + w * mod[..., 3 * j + 2, :] * rms_norm(y, g)


def macaron_ffn(x, mod, j, g_pre, g_post, w1, w3, w2):
    h = ada_pre(x, mod, j, g_pre)
    y = (jax.nn.silu(h @ w1) * (h @ w3)) @ w2
    return ada_post(x, y, mod, j, g_post, 0.5)


def split_columns(p):
    idx = np.cumsum(COLUMN_SIZES)[:-1].tolist()
    return jnp.split(p, idx, axis=-1)


def short_conv(x, w):
    K = w.shape[0]
    L = x.shape[1]
    r = K // 2
    xp = jnp.pad(x, ((0, 0), (r, r), (0, 0)))
    out = xp[:, 0:L] * w[0]
    for j in range(1, K):
        out = out + xp[:, j:j + L] * w[j]
    return out


def axial_angles(L):
    rows = L // GRID_W
    row = jnp.repeat(jnp.arange(rows, dtype=jnp.float32), GRID_W)
    col = jnp.tile(jnp.arange(GRID_W, dtype=jnp.float32), rows)
    n_freq = ATT_HEAD_DIM // 4
    inv = ROPE_BASE ** (-jnp.arange(n_freq, dtype=jnp.float32) / n_freq)
    return row[:, None] * inv, col[:, None] * inv


def rope_axis(x, ang):
    F = ang.shape[-1]
    shape = (ang.shape[0],) + (1,) * (x.ndim - 3) + (F,)
    cos = jnp.cos(ang).reshape(shape).astype(x.dtype)
    sin = jnp.sin(ang).reshape(shape).astype(x.dtype)
    x1, x2 = x[..., :F], x[..., F:]
    return jnp.concatenate([x1 * cos - x2 * sin, x2 * cos + x1 * sin], axis=-1)


def rope_2d(x, ang_row, ang_col):
    half = ATT_HEAD_DIM // 2
    return jnp.concatenate([rope_axis(x[..., :half], ang_row), rope_axis(x[..., half:], ang_col)], axis=-1)


def sink_softmax(s, sink):
    sk = sink.astype(jnp.float32)[:, :, None, None]
    m = jnp.maximum(jnp.max(s, axis=-1, keepdims=True), sk)
    p = jnp.exp(s - m)
    return p / (jnp.sum(p, axis=-1, keepdims=True) + jnp.exp(sk - m))


def context_attention(q, k, v, sink):
    B, Lc = q.shape[:2]
    s = jnp.einsum('bqgrd,bkgd->bgrqk', q, k).astype(jnp.float32)
    p = sink_softmax(s, sink.reshape(ATT_KV_HEADS, ATT_GROUP)).astype(v.dtype)
    return jnp.einsum('bgrqk,bkgd->bqgrd', p, v).reshape(B, Lc, ATT_Q)


def banded_attention(q, k, v, k_ctx, v_ctx, sink):
    B, L, G, R, d = q.shape
    nb = L // ATT_BLOCK
    pad = BAND_SIDE * ATT_BLOCK
    n_band = (2 * BAND_SIDE + 1) * ATT_BLOCK

    def band(a):
        ap = jnp.pad(a, ((0, 0), (pad, pad), (0, 0), (0, 0)))
        pieces = [ap[:, s * ATT_BLOCK: s * ATT_BLOCK + L].reshape(B, nb, ATT_BLOCK, G, d)
                  for s in range(2 * BAND_SIDE + 1)]
        return jnp.moveaxis(jnp.concatenate(pieces, axis=2), 1, 0)

    qb = jnp.moveaxis(q.reshape(B, nb, ATT_BLOCK, G, R, d), 1, 0)
    kb, vb = band(k), band(v)
    rel = (jnp.arange(n_band) - pad)[None, :] - jnp.arange(ATT_BLOCK)[:, None]
    in_window = jnp.abs(rel) <= WINDOW
    sink_gr = sink.reshape(G, R)

    def block(inp):
        i, qi, ki, vi = inp
        kpos = i * ATT_BLOCK - pad + jnp.arange(n_band)
        valid = in_window & ((kpos >= 0) & (kpos < L))[None, :]
        s_lat = jnp.einsum('bqgrd,bkgd->bgrqk', qi, ki).astype(jnp.float32)
        s_lat = jnp.where(valid, s_lat, -jnp.inf)
        s_ctx = jnp.einsum('bqgrd,bkgd->bgrqk', qi, k_ctx).astype(jnp.float32)
        p = sink_softmax(jnp.concatenate([s_lat, s_ctx], axis=-1), sink_gr).astype(vi.dtype)
        return jnp.einsum('bgrqk,bkgd->bqgrd', p, jnp.concatenate([vi, v_ctx], axis=1))

    out = lax.map(block, (jnp.arange(nb), qb, kb, vb))
    return jnp.moveaxis(out, 0, 1).reshape(B, L, G * R * d)


def mlstm_scan(q, k, v, log_i, log_f, state):
    B, H, L, _ = q.shape
    nc = L // CHUNK
    tri = jnp.tril(jnp.ones((CHUNK, CHUNK), bool))

    def chunks(a):
        return jnp.moveaxis(a.reshape((B, H, nc, CHUNK) + a.shape[3:]), 2, 0)

    def step(carry, inp):
        C, n, m = carry
        qc, kc, vc, ic, fc = inp
        b = jnp.cumsum(fc, axis=-1)
        logw = jnp.where(tri, b[..., :, None] - b[..., None, :] + ic[..., None, :], -jnp.inf)
        inter = b + m[..., None]
        m_t = jnp.maximum(inter, jnp.max(logw, axis=-1))
        w_prev = jnp.exp(inter - m_t)
        scores = jnp.einsum('bhtd,bhsd->bhts', qc, kc) * jnp.exp(logw - m_t[..., None])
        num = (w_prev[..., None] * jnp.einsum('bhtd,bhde->bhte', qc, C)
               + jnp.einsum('bhts,bhse->bhte', scores, vc))
        den = w_prev * jnp.einsum('bhtd,bhd->bht', qc, n) + jnp.sum(scores, axis=-1)
        h = num / jnp.maximum(jnp.abs(den), jnp.exp(-m_t))[..., None]
        bl = b[..., -1]
        logu = bl[..., None] - b + ic
        m_new = jnp.maximum(bl + m, jnp.max(logu, axis=-1))
        a_prev = jnp.exp(bl + m - m_new)
        u = jnp.exp(logu - m_new[..., None])
        C = a_prev[..., None, None] * C + jnp.einsum('bhs,bhsd,bhse->bhde', u, kc, vc)
        n = a_prev[..., None] * n + jnp.einsum('bhs,bhsd->bhd', u, kc)
        return (C, n, m_new), h

    state, hs = lax.scan(step, state, (chunks(q), chunks(k), chunks(v), chunks(log_i), chunks(log_f)))
    return jnp.moveaxis(hs, 0, 2).reshape(B, H, L, -1), state


def hgrn2_scan(q, k, v, log_f, S0):
    B, H, L, _ = q.shape
    nc = L // CHUNK
    tri = jnp.tril(jnp.ones((CHUNK, CHUNK), bool))

    def chunks(a):
        return jnp.moveaxis(a.reshape(B, H, nc, CHUNK, a.shape[-1]), 2, 0)

    def step(S, inp):
        qc, kc, vc, fc = inp
        b = jnp.cumsum(fc, axis=2)
        o = jnp.einsum('bhtd,bhde->bhte', qc * jnp.exp(b), S)
        diff = b[:, :, :, None, :] - b[:, :, None, :, :]
        decay = jnp.exp(jnp.where(tri[:, :, None], diff, -jnp.inf))
        A = jnp.einsum('bhtd,bhsd,bhtsd->bhts', qc, kc, decay)
        o = o + jnp.einsum('bhts,bhse->bhte', A, vc)
        bl = b[:, :, -1:, :]
        S = (jnp.exp(bl[:, :, 0, :, None]) * S
             + jnp.einsum('bhsd,bhse->bhde', kc * jnp.exp(bl - b), vc))
        return S, o

    S, o = lax.scan(step, S0, (chunks(q), chunks(k), chunks(v), chunks(log_f)))
    return jnp.moveaxis(o, 0, 2).reshape(B, H, L, -1), S


def _identity(a):
    return a


def _reverse(a):
    return jnp.flip(a, axis=2)


def project_stream(p, ml_conv, ml_f_bias, lb, ang):
    B, L, _ = p.shape
    f32 = jnp.float32
    (a_q, a_k, a_v, m_q, m_k, m_v, m_o, m_i, m_f,
     h_q, h_f_fwd, h_f_bwd, h_i, h_g, br_g) = split_columns(p)

    def heads(a, h):
        return jnp.swapaxes(a.reshape(B, L, h, -1), 1, 2).astype(f32)

    q = a_q.reshape(B, L, ATT_KV_HEADS, ATT_GROUP, ATT_HEAD_DIM)
    k = a_k.reshape(B, L, ATT_KV_HEADS, ATT_HEAD_DIM)
    v = a_v.reshape(B, L, ATT_KV_HEADS, ATT_HEAD_DIM)
    if ang is not None:
        q = rope_2d(q, ang[0], ang[1])
        k = rope_2d(k, ang[0], ang[1])
    q = q * ATT_HEAD_DIM ** -0.5

    mq = jax.nn.silu(short_conv(m_q, ml_conv[:, :ML_QK]))
    mk = jax.nn.silu(short_conv(m_k, ml_conv[:, ML_QK:])) * ML_DK ** -0.5
    ml_i = jnp.transpose(m_i.reshape(B, L, 2, ML_HEADS).astype(f32), (2, 0, 3, 1))
    ml_f = jnp.transpose(jax.nn.log_sigmoid(m_f.reshape(B, L, 2, ML_HEADS).astype(f32) + ml_f_bias),
                         (2, 0, 3, 1))

    lb_h = lb.reshape(HG_HEADS, HG_DK)
    hq = jax.nn.silu(h_q) * HG_DK ** -0.5
    hg_k, hg_f = [], []
    for z in (h_f_fwd, h_f_bwd):
        z = z.reshape(B, L, HG_HEADS, HG_DK).astype(f32)
        log_f = jnp.logaddexp(jnp.log(lb_h), jnp.log1p(-lb_h) + jax.nn.log_sigmoid(z))
        hg_f.append(jnp.swapaxes(log_f, 1, 2))
        hg_k.append(jnp.swapaxes((1.0 - lb_h) * jax.nn.sigmoid(-z), 1, 2))

    return dict(att_q=q, att_k=k, att_v=v,
                ml_q=heads(mq, ML_HEADS), ml_k=heads(mk, ML_HEADS), ml_v=heads(m_v, ML_HEADS),
                ml_o=m_o, ml_i=ml_i, ml_f=ml_f,
                hg_q=heads(hq, HG_HEADS), hg_k=hg_k, hg_f=hg_f, hg_v=heads(h_i, HG_HEADS), hg_g=h_g,
                br_g=br_g)


def mlstm_bidir(sx, sc):
    B = sc['ml_q'].shape[0]
    f32 = jnp.float32
    init = (jnp.zeros((B, ML_HEADS, ML_DK, ML_DV), f32), jnp.zeros((B, ML_HEADS, ML_DK), f32),
            jnp.zeros((B, ML_HEADS), f32))
    out_x, out_c = [], []
    for d in range(2):
        fl = _reverse if d == 1 else _identity
        hc, st = mlstm_scan(fl(sc['ml_q']), fl(sc['ml_k']), fl(sc['ml_v']),
                            fl(sc['ml_i'][d]), fl(sc['ml_f'][d]), init)
        hx, _ = mlstm_scan(fl(sx['ml_q']), fl(sx['ml_k']), fl(sx['ml_v']),
                           fl(sx['ml_i'][d]), fl(sx['ml_f'][d]), st)
        out_x.append(fl(hx))
        out_c.append(fl(hc))
    return out_x[0] + out_x[1], out_c[0] + out_c[1]


def hgrn2_bidir(sx, sc):
    B = sc['hg_q'].shape[0]
    init = jnp.zeros((B, HG_HEADS, HG_DK, HG_DV), jnp.float32)
    out_x, out_c = [], []
    for d in range(2):
        fl = _reverse if d == 1 else _identity
        oc, st = hgrn2_scan(fl(sc['hg_q']), fl(sc['hg_k'][d]), fl(sc['hg_v']), fl(sc['hg_f'][d]), init)
        ox, _ = hgrn2_scan(fl(sx['hg_q']), fl(sx['hg_k'][d]), fl(sx['hg_v']), fl(sx['hg_f'][d]), st)
        out_x.append(fl(ox))
        out_c.append(fl(oc))
    return out_x[0] + out_x[1], out_c[0] + out_c[1]


def mlstm_readout(h, o_gate, g):
    B, _, L, _ = h.shape
    y = rms_norm(jnp.swapaxes(h, 1, 2), g.reshape(ML_HEADS, ML_DV)).reshape(B, L, ML_V)
    return (y * jax.nn.sigmoid(o_gate.astype(jnp.float32))).astype(o_gate.dtype)


def hgrn2_readout(o, gate, g):
    B, _, L, _ = o.shape
    y = rms_norm(jnp.swapaxes(o, 1, 2), g.reshape(HG_HEADS, HG_DV)).reshape(B, L, HG_V)
    return (y * jax.nn.silu(gate.astype(jnp.float32))).astype(gate.dtype)


def merge_branches(br_g, att, ml, hg, wb_att, wb_ml, wb_hg, w_out):
    B, L, _ = att.shape
    g = jax.nn.sigmoid(br_g).reshape(B, L, N_BRANCH, D_MODEL)
    y = (g[..., 0, :] * (att @ wb_att) + g[..., 1, :] * (ml @ wb_ml) + g[..., 2, :] * (hg @ wb_hg))
    return y @ w_out


def token_mix(hx, hc, ang, w_in, sink, ml_conv, ml_f_bias, ml_norm, lb, hg_norm,
              wb_att, wb_ml, wb_hg, w_out, need_ctx_out):
    sx = project_stream(hx @ w_in, ml_conv, ml_f_bias, lb, ang)
    sc = project_stream(hc @ w_in, ml_conv, ml_f_bias, lb, None)
    att_x = banded_attention(sx['att_q'], sx['att_k'], sx['att_v'], sc['att_k'], sc['att_v'], sink)
    ml_x, ml_c = mlstm_bidir(sx, sc)
    hg_x, hg_c = hgrn2_bidir(sx, sc)
    yx = merge_branches(sx['br_g'], att_x.astype(hx.dtype),
                        mlstm_readout(ml_x, sx['ml_o'], ml_norm),
                        hgrn2_readout(hg_x, sx['hg_g'], hg_norm), wb_att, wb_ml, wb_hg, w_out)
    yc = None
    if need_ctx_out:
        att_c = context_attention(sc['att_q'], sc['att_k'], sc['att_v'], sink)
        yc = merge_branches(sc['br_g'], att_c.astype(hc.dtype),
                            mlstm_readout(ml_c, sc['ml_o'], ml_norm),
                            hgrn2_readout(hg_c, sc['hg_g'], hg_norm), wb_att, wb_ml, wb_hg, w_out)
    return yx, yc


def setup_inputs(seed: int = 0) -> dict:
    key = jax.random.key(seed)
    ks = jax.random.split(key, 24)
    f32 = jnp.float32

    def nrm(k, shape, scale):
        return jax.random.normal(k, shape, f32) * scale

    return {
        "x": nrm(ks[0], (BATCH, SEQ, D_MODEL), 1.0),
        "c": nrm(ks[1], (BATCH, D_MODEL), 1.0),
        "ctx": nrm(ks[2], (BATCH, CTX_LEN, D_MODEL), 1.0),
        "c_ctx": nrm(ks[3], (D_MODEL,), 1.0),
        "w_ada": nrm(ks[4], (DEPTH, D_MODEL, N_MOD * D_MODEL), 0.5 * D_MODEL ** -0.5),
        "b_ada": nrm(ks[5], (DEPTH, N_MOD * D_MODEL), 0.02),
        "norm_pre": 1.0 + nrm(ks[6], (DEPTH, N_SUB, D_MODEL), 0.02),
        "norm_post": 1.0 + nrm(ks[7], (DEPTH, N_SUB, D_MODEL), 0.02),
        "ffn_w1": nrm(ks[8], (DEPTH, 2, D_MODEL, D_FF), D_MODEL ** -0.5),
        "ffn_w3": nrm(ks[9], (DEPTH, 2, D_MODEL, D_FF), D_MODEL ** -0.5),
        "ffn_w2": nrm(ks[10], (DEPTH, 2, D_FF, D_MODEL), D_FF ** -0.5),
        "w_in": nrm(ks[11], (DEPTH, D_MODEL, D_IN), D_MODEL ** -0.5),
        "att_sink": nrm(ks[12], (DEPTH, ATT_HEADS), 0.5),
        "ml_conv": nrm(ks[13], (DEPTH, ML_CONV_W, 2 * ML_QK), ML_CONV_W ** -0.5),
        "ml_f_bias": jnp.linspace(3.0, 6.0, ML_HEADS, dtype=f32) + nrm(ks[14], (DEPTH, 2, ML_HEADS), 0.1),
        "ml_norm": 1.0 + nrm(ks[15], (DEPTH, ML_V), 0.02),
        "hg_lb_logits": nrm(ks[16], (DEPTH, HG_K), 0.1),
        "hg_norm": 1.0 + nrm(ks[17], (DEPTH, HG_V), 0.02),
        "w_branch_att": nrm(ks[18], (DEPTH, ATT_Q, D_MODEL), ATT_Q ** -0.5),
        "w_branch_ml": nrm(ks[19], (DEPTH, ML_V, D_MODEL), ML_V ** -0.5),
        "w_branch_hg": nrm(ks[20], (DEPTH, HG_V, D_MODEL), HG_V ** -0.5),
        "w_out": nrm(ks[21], (DEPTH, D_MODEL, D_MODEL), D_MODEL ** -0.5),
    }


def reference(x, c, ctx, c_ctx, w_ada, b_ada, norm_pre, norm_post, ffn_w1, ffn_w3, ffn_w2,
              w_in, att_sink, ml_conv, ml_f_bias, ml_norm, hg_lb_logits, hg_norm,
              w_branch_att, w_branch_ml, w_branch_hg, w_out):
    B, L, D = x.shape
    ang = axial_angles(L)
    lb_all = jnp.cumsum(jax.nn.softmax(hg_lb_logits.astype(jnp.float32), axis=0), axis=0)
    lb_all = lb_all - lb_all[0:1]
    cx = ctx
    for l in range(DEPTH):
        last = l == DEPTH - 1
        mod_x = (jax.nn.silu(c) @ w_ada[l] + b_ada[l]).reshape(B, 1, N_MOD, D)
        mod_c = (jax.nn.silu(c_ctx) @ w_ada[l] + b_ada[l]).reshape(1, 1, N_MOD, D)
        x = macaron_ffn(x, mod_x, 0, norm_pre[l, 0], norm_post[l, 0], ffn_w1[l, 0], ffn_w3[l, 0], ffn_w2[l, 0])
        cx = macaron_ffn(cx, mod_c, 0, norm_pre[l, 0], norm_post[l, 0], ffn_w1[l, 0], ffn_w3[l, 0], ffn_w2[l, 0])
        hx = ada_pre(x, mod_x, 1, norm_pre[l, 1])
        hc = ada_pre(cx, mod_c, 1, norm_pre[l, 1])
        yx, yc = token_mix(hx, hc, ang, w_in[l], att_sink[l], ml_conv[l], ml_f_bias[l], ml_norm[l],
                           lb_all[l], hg_norm[l], w_branch_att[l], w_branch_ml[l], w_branch_hg[l],
                           w_out[l], not last)
        x = ada_post(x, yx, mod_x, 1, norm_post[l, 1], 1.0)
        x = macaron_ffn(x, mod_x, 2, norm_pre[l, 2], norm_post[l, 2], ffn_w1[l, 1], ffn_w3[l, 1], ffn_w2[l, 1])
        if not last:
            cx = ada_post(cx, yc, mod_c, 1, norm_post[l, 1], 1.0)
            cx = macaron_ffn(cx, mod_c, 2, norm_pre[l, 2], norm_post[l, 2], ffn_w1[l, 1], ffn_w3[l, 1], ffn_w2[l, 1])
    return x
```

```python
import functools

import numpy as np
import jax
import jax.numpy as jnp
from jax import lax
from jax.experimental import pallas as pl
from jax.experimental.pallas import tpu as pltpu

F32 = jnp.float32
BF16 = jnp.bfloat16

ATT_HEADS = 8
ATT_KV_HEADS = 2
ATT_GROUP = ATT_HEADS // ATT_KV_HEADS
ATT_HEAD_DIM = 64
ATT_BLOCK = 128
ML_HEADS = 4
ML_DK = 64
ML_DV = 128
HG_HEADS = 4
HG_DK = 64
HG_DV = 128
GRID_W = 64
ROPE_BASE = 10000.0
EPS = 1e-6
N_MOD = 9
MOD_ROWS = 16

ATT_Q = ATT_HEADS * ATT_HEAD_DIM
ATT_KV = ATT_KV_HEADS * ATT_HEAD_DIM
ML_QK = ML_HEADS * ML_DK
ML_V = ML_HEADS * ML_DV
HG_K = HG_HEADS * HG_DK
HG_V = HG_HEADS * HG_DV
N_GATE = 4 * ML_HEADS
GATE_PAD = 128

OFF_ATT = 0
OFF_MQK = OFF_ATT + ATT_Q + 2 * ATT_KV
OFF_MVO = OFF_MQK + 2 * ML_QK
OFF_HG = OFF_MVO + 2 * ML_V
OFF_GATE = OFF_HG + 3 * HG_K + 2 * HG_V
W_PROJ_COLS = OFF_GATE + GATE_PAD

VMEM_LIMIT = 56 * 1024 * 1024


def _dot(a, b):
    return jnp.dot(a, b, preferred_element_type=F32)


def _dot_nt(a, b):
    return lax.dot_general(a, b, (((1,), (1,)), ((), ())), preferred_element_type=F32)


def _dot_tn(a, b):
    return lax.dot_general(a, b, (((0,), (0,)), ((), ())), preferred_element_type=F32)


def _dot_hi(a, b):
    return jnp.dot(a, b, precision=lax.Precision.HIGHEST, preferred_element_type=F32)


def _dot_nt_hi(a, b):
    return lax.dot_general(a, b, (((1,), (1,)), ((), ())), precision=lax.Precision.HIGHEST,
                           preferred_element_type=F32)


def _rms(x, g):
    ms = jnp.mean(x * x, axis=-1, keepdims=True)
    return x * lax.rsqrt(ms + EPS) * g


def _silu(x):
    return x * jax.nn.sigmoid(x)


def _log_sigmoid(x):
    return jnp.minimum(x, 0.0) - jnp.log1p(jnp.exp(-jnp.abs(x)))


def _params(sem):
    return pltpu.CompilerParams(dimension_semantics=sem, vmem_limit_bytes=VMEM_LIMIT)


def _resident(shape):
    nd = len(shape)
    return pl.BlockSpec(shape, lambda *_: (0,) * nd, pipeline_mode=pl.Buffered(1))


def _mod_kernel(c_ref, w_ref, b_ref, o_ref):
    s = _silu(c_ref[...]).astype(BF16)
    o_ref[0] = _dot(s, w_ref[0].astype(BF16)) + b_ref[0]


def _modulation(cs, w_ada, b_ada):
    depth, d, nm = w_ada.shape
    tn = 1024
    return pl.pallas_call(
        _mod_kernel,
        grid=(depth, nm // tn),
        in_specs=[
            pl.BlockSpec((MOD_ROWS, d), lambda l, n: (0, 0)),
            pl.BlockSpec((1, d, tn), lambda l, n: (l, 0, n)),
            pl.BlockSpec((1, 1, tn), lambda l, n: (l, 0, n)),
        ],
        out_specs=pl.BlockSpec((1, MOD_ROWS, tn), lambda l, n: (l, 0, n)),
        out_shape=jax.ShapeDtypeStruct((depth, MOD_ROWS, nm), F32),
        compiler_params=_params(("parallel", "parallel")),
        name="modulation",
    )(cs, w_ada, b_ada.reshape(depth, 1, nm))


class _Layout:
    def __init__(self, batch, lc, l):
        self.batch, self.lc, self.l = batch, lc, l
        self.nc = batch * lc
        self.n = self.nc + batch * l
        tm = 512
        while self.nc % tm or l % tm:
            tm //= 2
        self.tm = tm
        self.nct = self.nc // tm
        self.tpb = l // tm
        self.ntiles = self.n // tm

    def mod_index(self, t):
        return jnp.where(t < self.nct, self.batch, (t - self.nct) // self.tpb)


def _ffn_kernel(x_ref, mod_ref, gpre_ref, gpost_ref, w1_ref, w3_ref, w2_ref, o_ref, *, j, halves):
    x = x_ref[...]
    mod = mod_ref[0]
    shift, scale, gate = mod[3 * j:3 * j + 1], mod[3 * j + 1:3 * j + 2], mod[3 * j + 2:3 * j + 3]
    h = (_rms(x, gpre_ref[...]) * (1.0 + scale) + shift).astype(BF16)
    dff = w1_ref.shape[1]
    step = dff // halves
    y = None
    for c in range(halves):
        sl = slice(c * step, (c + 1) * step)
        a = _dot(h, w1_ref[:, sl])
        b = _dot(h, w3_ref[:, sl])
        part = _dot((_silu(a) * b).astype(BF16), w2_ref[sl, :])
        y = part if y is None else y + part
    o_ref[...] = x + 0.5 * gate * _rms(y, gpost_ref[...])


def _ffn(lay, x, mod, j, gpre, gpost, w1, w3, w2):
    n, d = x.shape
    dff = w1.shape[1]
    tm = lay.tm
    halves = 2 if (dff // 2) % 128 == 0 else 1
    row = pl.BlockSpec((tm, d), lambda t: (t, 0))
    return pl.pallas_call(
        functools.partial(_ffn_kernel, j=j, halves=halves),
        grid=(lay.ntiles,),
        in_specs=[
            row,
            pl.BlockSpec((1, N_MOD, d), lambda t: (lay.mod_index(t), 0, 0)),
            _resident((1, d)), _resident((1, d)),
            _resident((d, dff)), _resident((d, dff)), _resident((dff, d)),
        ],
        out_specs=row,
        out_shape=jax.ShapeDtypeStruct((n, d), F32),
        compiler_params=_params(("parallel",)),
        name="ffn",
    )(x, mod, gpre, gpost, w1, w3, w2)


def _proj_kernel(x_ref, xp_ref, xn_ref, mod_ref, gpre_ref, w_ref, wgt_ref, rope_ref, conv_ref,
                 fbrow_ref, fbcol_ref, lbv_ref,
                 aq_ref, akv_ref, mqk_ref, mv_ref, mo_ref, gc_ref, gr_ref,
                 hq_ref, hk_ref, hlf_ref, hv_ref, hg_ref, *, nct, lc, l, tm):
    t = pl.program_id(0)
    mod = mod_ref[0]
    shift, scale = mod[3:4], mod[4:5]
    gpre = gpre_ref[...]

    def pre(xx):
        return (_rms(xx, gpre) * (1.0 + scale) + shift).astype(BF16)

    h = pre(x_ref[...])

    pa = _dot(h, w_ref[:, OFF_ATT:OFF_MQK])
    rope = rope_ref[...]
    cos, sin = rope[:, :128], rope[:, 128:]

    def swap(v):
        w = v.shape[1]
        lane = lax.broadcasted_iota(jnp.int32, v.shape, 1)
        return jnp.where((lane & 16) == 0, pltpu.roll(v, w - 16, 1), pltpu.roll(v, 16, 1))

    q = pa[:, :ATT_Q]
    k = pa[:, ATT_Q:ATT_Q + ATT_KV]
    cos4 = jnp.concatenate([cos] * (ATT_Q // 128), axis=1)
    sin4 = jnp.concatenate([sin] * (ATT_Q // 128), axis=1)
    q = (q * cos4 + swap(q) * sin4) * (ATT_HEAD_DIM ** -0.5)
    k = k * cos + swap(k) * sin
    aq_ref[...] = q.astype(BF16)
    akv_ref[...] = jnp.concatenate([k, pa[:, ATT_Q + ATT_KV:]], axis=1).astype(BF16)

    wm = w_ref[:, OFF_MQK:OFF_MVO]
    pm = _dot(h, wm)
    pprev = _dot(pre(xp_ref[...]), wm)[7:8]
    pnext = _dot(pre(xn_ref[...]), wm)[0:1]
    r = lax.broadcasted_iota(jnp.int32, (tm, 1), 0)
    grow = t * tm + r
    is_ctx = t < nct
    pos = jnp.where(is_ctx, grow & (lc - 1), (grow - nct * tm) & (l - 1))
    last = jnp.where(is_ctx, lc - 1, l - 1)
    dn = jnp.where(r == 0, pprev, pltpu.roll(pm, 1, 0))
    dn = jnp.where(pos == 0, 0.0, dn)
    up = jnp.where(r == tm - 1, pnext, pltpu.roll(pm, tm - 1, 0))
    up = jnp.where(pos == last, 0.0, up)
    cw = conv_ref[...]
    cv = dn * cw[0:1] + pm * cw[1:2] + up * cw[2:3]
    lane = lax.broadcasted_iota(jnp.int32, (1, 2 * ML_QK), 1)
    mqk_ref[...] = (_silu(cv) * jnp.where(lane < ML_QK, 1.0, ML_DK ** -0.5)).astype(BF16)

    pv = _dot(h, w_ref[:, OFF_MVO:OFF_HG])
    mv_ref[...] = pv[:, :ML_V].astype(BF16)
    mo_ref[...] = pv[:, ML_V:]

    pg = _dot(h, w_ref[:, OFF_GATE:OFF_GATE + GATE_PAD])[:, :N_GATE] + fbrow_ref[...]
    lane16 = lax.broadcasted_iota(jnp.int32, (1, N_GATE), 1)
    gc_ref[...] = jnp.where(lane16 < 2 * ML_HEADS, pg, _log_sigmoid(pg))
    pgr = _dot_nt(wgt_ref[...], h) + fbcol_ref[...]
    row16 = lax.broadcasted_iota(jnp.int32, (N_GATE, 1), 0)
    gr_ref[...] = jnp.where(row16 < 2 * ML_HEADS, pgr, _log_sigmoid(pgr))

    ph = _dot(h, w_ref[:, OFF_HG:OFF_GATE])
    hq_ref[...] = _silu(ph[:, :HG_K]) * (HG_DK ** -0.5)
    z = ph[:, HG_K:3 * HG_K]
    lbv = lbv_ref[...]
    log_lb, log_1m_lb, one_m_lb = lbv[0:1], lbv[1:2], lbv[2:3]
    bv = log_1m_lb + _log_sigmoid(z)
    hlf_ref[...] = jnp.maximum(log_lb, bv) + jnp.log1p(jnp.exp(-jnp.abs(log_lb - bv)))
    hk_ref[...] = one_m_lb * jax.nn.sigmoid(-z)
    hv_ref[...] = ph[:, 3 * HG_K:3 * HG_K + HG_V].astype(BF16)
    hg_ref[...] = ph[:, 3 * HG_K + HG_V:]


def _project(lay, x, mod, gpre, w, wgt, rope, conv, fbrow, fbcol, lbv):
    n, d = x.shape
    tm = lay.tm
    nb8 = n // 8

    def row(c):
        return pl.BlockSpec((tm, c), lambda t: (t, 0))

    out_cols = [(ATT_Q, BF16), (2 * ATT_KV, BF16), (2 * ML_QK, BF16), (ML_V, BF16), (ML_V, F32),
                (N_GATE, F32), None, (HG_K, F32), (2 * HG_K, F32), (2 * HG_K, F32), (HG_V, BF16), (HG_V, F32)]
    out_specs, out_shape = [], []
    for oc in out_cols:
        if oc is None:
            out_specs.append(pl.BlockSpec((N_GATE, tm), lambda t: (0, t)))
            out_shape.append(jax.ShapeDtypeStruct((N_GATE, n), F32))
        else:
            out_specs.append(row(oc[0]))
            out_shape.append(jax.ShapeDtypeStruct((n, oc[0]), oc[1]))
    return pl.pallas_call(
        functools.partial(_proj_kernel, nct=lay.nct, lc=lay.lc, l=lay.l, tm=tm),
        grid=(lay.ntiles,),
        in_specs=[
            row(d),
            pl.BlockSpec((8, d), lambda t: (jnp.maximum(t * (tm // 8) - 1, 0), 0)),
            pl.BlockSpec((8, d), lambda t: (jnp.minimum((t + 1) * (tm // 8), nb8 - 1), 0)),
            pl.BlockSpec((1, N_MOD, d), lambda t: (lay.mod_index(t), 0, 0)),
            _resident((1, d)),
            _resident(w.shape), _resident(wgt.shape),
            row(256),
            _resident(conv.shape), _resident(fbrow.shape), _resident(fbcol.shape), _resident(lbv.shape),
        ],
        out_specs=out_specs,
        out_shape=out_shape,
        compiler_params=_params(("parallel",)),
        name="project",
    )(x, x, x, mod, gpre, w, wgt, rope, conv, fbrow, fbcol, lbv)


def _merge_kernel(x_ref, att_ref, ml_ref, hg_ref, mod_ref, gpre_ref, gpost_ref,
                  wg_ref, wa_ref, wm_ref, wh_ref, wo_ref, o_ref):
    x = x_ref[...]
    d = x.shape[1]
    mod = mod_ref[0]
    shift, scale, gate = mod[3:4], mod[4:5], mod[5:6]
    h = (_rms(x, gpre_ref[...]) * (1.0 + scale) + shift).astype(BF16)
    y = None
    for i, (b_ref, w_ref) in enumerate(((att_ref, wa_ref), (ml_ref, wm_ref), (hg_ref, wh_ref))):
        g = jax.nn.sigmoid(_dot(h, wg_ref[:, i * d:(i + 1) * d]))
        part = g * _dot(b_ref[...], w_ref[...])
        y = part if y is None else y + part
    yy = _dot(y.astype(BF16), wo_ref[...])
    o_ref[...] = x + gate * _rms(yy, gpost_ref[...])


def _merge(lay, x, att, ml, hg, mod, gpre, gpost, wg, wa, wm, wh, wo):
    n, d = x.shape
    tm = lay.tm

    def row(c):
        return pl.BlockSpec((tm, c), lambda t: (t, 0))

    return pl.pallas_call(
        _merge_kernel,
        grid=(lay.ntiles,),
        in_specs=[
            row(d), row(ATT_Q), row(ML_V), row(HG_V),
            pl.BlockSpec((1, N_MOD, d), lambda t: (lay.mod_index(t), 0, 0)),
            _resident((1, d)), _resident((1, d)),
            _resident(wg.shape), _resident(wa.shape), _resident(wm.shape), _resident(wh.shape),
            _resident(wo.shape),
        ],
        out_specs=row(d),
        out_shape=jax.ShapeDtypeStruct((n, d), F32),
        compiler_params=_params(("parallel",)),
        name="merge",
    )(x, att, ml, hg, mod, gpre, gpost, wg, wa, wm, wh, wo)


def _attend(q, kv, bias, sink_ref):
    outs = []
    for hd in range(ATT_HEADS):
        g = hd // ATT_GROUP
        qh = q[:, ATT_HEAD_DIM * hd:ATT_HEAD_DIM * (hd + 1)]
        kg = kv[:, ATT_HEAD_DIM * g:ATT_HEAD_DIM * (g + 1)]
        vg = kv[:, ATT_KV + ATT_HEAD_DIM * g:ATT_KV + ATT_HEAD_DIM * (g + 1)]
        s = _dot_nt(qh, kg)
        if bias is not None:
            s = s + bias
        sk = sink_ref[hd]
        m = jnp.maximum(jnp.max(s, axis=1, keepdims=True), sk)
        p = jnp.exp(s - m)
        den = jnp.sum(p, axis=1, keepdims=True) + jnp.exp(sk - m)
        outs.append(_dot(p.astype(BF16), vg) / den)
    return jnp.concatenate(outs, axis=1).astype(BF16)


def _attn_lat_kernel(sink_ref, q_ref, kl_ref, km_ref, kr_ref, kc_ref, o_ref, *, nb, lc):
    i = pl.program_id(1)
    kv = jnp.concatenate([kl_ref[...], km_ref[...], kr_ref[...], kc_ref[...]], axis=0)
    r = lax.broadcasted_iota(jnp.int32, (ATT_BLOCK, ATT_BLOCK), 0)
    j = lax.broadcasted_iota(jnp.int32, (ATT_BLOCK, ATT_BLOCK), 1)
    ninf = jnp.float32(-jnp.inf)
    left = jnp.where((j >= r) & (i > 0), 0.0, ninf)
    right = jnp.where((j <= r) & (i < nb - 1), 0.0, ninf)
    bias = jnp.concatenate([left, jnp.zeros((ATT_BLOCK, ATT_BLOCK), F32), right,
                            jnp.zeros((ATT_BLOCK, lc), F32)], axis=1)
    o_ref[...] = _attend(q_ref[...], kv, bias, sink_ref)


def _attn_ctx_kernel(sink_ref, q_ref, kc_ref, o_ref):
    o_ref[...] = _attend(q_ref[...], kc_ref[...], None, sink_ref)


def _attention(lay, aq, akv, sink, with_ctx):
    n = aq.shape[0]
    nb = lay.l // ATT_BLOCK
    base = lay.nc // ATT_BLOCK
    lc = lay.lc
    smem = pl.BlockSpec(memory_space=pltpu.SMEM)
    kvw = 2 * ATT_KV

    def band(off):
        return pl.BlockSpec((ATT_BLOCK, kvw),
                            lambda b, i: (base + b * nb + jnp.clip(i + off, 0, nb - 1), 0))

    out = pl.pallas_call(
        functools.partial(_attn_lat_kernel, nb=nb, lc=lc),
        grid=(lay.batch, nb),
        in_specs=[smem,
                  pl.BlockSpec((ATT_BLOCK, ATT_Q), lambda b, i: (base + b * nb + i, 0)),
                  band(-1), band(0), band(1),
                  pl.BlockSpec((lc, kvw), lambda b, i: (b, 0))],
        out_specs=pl.BlockSpec((ATT_BLOCK, ATT_Q), lambda b, i: (b * nb + i, 0)),
        out_shape=jax.ShapeDtypeStruct((lay.batch * lay.l, ATT_Q), BF16),
        compiler_params=_params(("parallel", "parallel")),
        name="attn_latent",
    )(sink, aq, akv, akv, akv, akv)
    if not with_ctx:
        return jnp.concatenate([jnp.zeros((lay.nc, ATT_Q), BF16), out], axis=0)
    ncb = lc // ATT_BLOCK
    out_c = pl.pallas_call(
        _attn_ctx_kernel,
        grid=(lay.batch, ncb),
        in_specs=[smem,
                  pl.BlockSpec((ATT_BLOCK, ATT_Q), lambda b, i: (b * ncb + i, 0)),
                  pl.BlockSpec((lc, kvw), lambda b, i: (b, 0))],
        out_specs=pl.BlockSpec((ATT_BLOCK, ATT_Q), lambda b, i: (b * ncb + i, 0)),
        out_shape=jax.ShapeDtypeStruct((lay.nc, ATT_Q), BF16),
        compiler_params=_params(("parallel", "parallel")),
        name="attn_context",
    )(sink, aq, akv)
    return jnp.concatenate([out_c, out], axis=0)


class _Chunks:
    def __init__(self, lay, ch):
        self.ch = ch
        self.nctx = lay.lc // ch
        self.nlat = lay.l // ch
        self.base = lay.nc // ch
        self.steps = self.nctx + self.nlat
        self.total = lay.n // ch

    def fwd(self, b, j):
        return jnp.where(j < self.nctx, b * self.nctx + j, self.base + b * self.nlat + (j - self.nctx))

    def bwd(self, b, j):
        return jnp.where(j < self.nctx, b * self.nctx + (self.nctx - 1 - j),
                         self.base + b * self.nlat + (self.nlat - 1 - (j - self.nctx)))


def _tri(ch):
    t = np.arange(ch)[:, None]
    u = np.arange(ch)[None, :]
    return jnp.asarray(np.stack([(u <= t), (u >= t)]).astype(np.float32))


def _ml_scan_kernel(kf_ref, vf_ref, gf_ref, kb_ref, vb_ref, gb_ref, tri_ref,
                    cf_ref, nf_ref, mf_ref, cb_ref, nb_ref, mb_ref, c_s, n_s, m_s, *, ch):
    @pl.when(pl.program_id(1) == 0)
    def _():
        c_s[...] = jnp.zeros_like(c_s)
        n_s[...] = jnp.zeros_like(n_s)
        m_s[...] = jnp.zeros_like(m_s)

    streams = ((kf_ref, vf_ref, gf_ref, cf_ref, nf_ref, mf_ref),
               (kb_ref, vb_ref, gb_ref, cb_ref, nb_ref, mb_ref))
    for d, (k_ref, v_ref, g_ref, c_out, n_out, m_out) in enumerate(streams):
        gc = g_ref[...]
        ig = gc[:, ML_HEADS * d:ML_HEADS * (d + 1)]
        lf = gc[:, 2 * ML_HEADS + ML_HEADS * d:2 * ML_HEADS + ML_HEADS * (d + 1)]
        b = _dot_hi(tri_ref[d], lf)
        a = ig - b
        bl = b[ch - 1:ch] if d == 0 else b[0:1]
        m_prev = m_s[d]
        mx = jnp.maximum(m_prev, jnp.max(a, axis=0, keepdims=True))
        alpha = jnp.exp(m_prev - mx)
        u = jnp.exp(a - mx)
        m_out[0] = m_prev
        m_s[d] = bl + mx
        k = k_ref[...].astype(F32)
        v = v_ref[...]
        for hh in range(ML_HEADS):
            c_old = c_s[d, hh]
            n_old = n_s[d, hh]
            c_out[0, hh] = c_old.astype(BF16)
            n_out[0, hh] = n_old
            ku = k[:, ML_DK * hh:ML_DK * (hh + 1)] * u[:, hh:hh + 1]
            al = alpha[:, hh:hh + 1]
            c_s[d, hh] = al * c_old + _dot_tn(ku.astype(BF16), v[:, ML_DV * hh:ML_DV * (hh + 1)])
            n_s[d, hh] = al * n_old + jnp.sum(ku, axis=0, keepdims=True)


def _ml_out_kernel(mf_ref, mb_ref, qk_ref, v_ref, o_ref, gc_ref, gr_ref, cf_ref, cb_ref, nf_ref, nb_ref,
                   tri_ref, gn_ref, out_ref, *, c0, ch):
    c = pl.program_id(0) + c0
    qk = qk_ref[...]
    v = v_ref[...]
    gc = gc_ref[...]
    gr = gr_ref[...]
    rr = lax.broadcasted_iota(jnp.int32, (ch, ch), 0)
    cc = lax.broadcasted_iota(jnp.int32, (ch, ch), 1)
    ninf = jnp.float32(-jnp.inf)
    hs = [None] * ML_HEADS
    for d in range(2):
        tri = tri_ref[d]
        lo = 2 * ML_HEADS + ML_HEADS * d
        bcol = _dot_hi(tri, gc[:, lo:lo + ML_HEADS])
        brow = _dot_nt_hi(gr[lo:lo + ML_HEADS, :], tri)
        arow = gr[ML_HEADS * d:ML_HEADS * (d + 1), :] - brow
        mask = (cc <= rr) if d == 0 else (cc >= rr)
        m_ref = mf_ref if d == 0 else mb_ref
        c_ref = cf_ref if d == 0 else cb_ref
        n_ref = nf_ref if d == 0 else nb_ref
        for hh in range(ML_HEADS):
            m_prev = m_ref[c, hh]
            am = jnp.where(mask, arow[hh:hh + 1, :], ninf)
            g = jnp.max(am, axis=1, keepdims=True)
            e = jnp.exp(am - g)
            q = qk[:, ML_DK * hh:ML_DK * (hh + 1)]
            k = qk[:, ML_QK + ML_DK * hh:ML_QK + ML_DK * (hh + 1)]
            s = _dot_nt(q, k) * e
            den_i = jnp.sum(s, axis=1, keepdims=True)
            num_i = _dot(s.astype(BF16), v[:, ML_DV * hh:ML_DV * (hh + 1)])
            mt = jnp.maximum(g, m_prev)
            rf = jnp.exp(g - mt)
            wp = jnp.exp(m_prev - mt)
            inter = _dot(q, c_ref[0, hh])
            dn = jnp.sum(q.astype(F32) * n_ref[0, hh], axis=1, keepdims=True)
            num = wp * inter + rf * num_i
            den = wp * dn + rf * den_i
            hv = num / jnp.maximum(jnp.abs(den), jnp.exp(-bcol[:, hh:hh + 1] - mt))
            hs[hh] = hv if hs[hh] is None else hs[hh] + hv
    gn = gn_ref[...]
    og = o_ref[...]
    outs = []
    for hh in range(ML_HEADS):
        sl = slice(ML_DV * hh, ML_DV * (hh + 1))
        outs.append(_rms(hs[hh], gn[:, sl]) * jax.nn.sigmoid(og[:, sl]))
    out_ref[...] = jnp.concatenate(outs, axis=1).astype(BF16)


def _mlstm(lay, mqk, mv, mo, gc, gr, gnorm, with_ctx):
    ch = min(256, lay.lc)
    ck = _Chunks(lay, ch)
    tri = _tri(ch)
    nt = ck.total

    def spec(cols, fn, colblk=0):
        return pl.BlockSpec((ch, cols), lambda b, j: (fn(b, j), colblk))

    def state_specs(fn):
        return [pl.BlockSpec((1, ML_HEADS, ML_DK, ML_DV), lambda b, j: (fn(b, j), 0, 0, 0)),
                pl.BlockSpec((1, ML_HEADS, 1, ML_DK), lambda b, j: (fn(b, j), 0, 0, 0)),
                pl.BlockSpec((1, 1, ML_HEADS), lambda b, j: (fn(b, j), 0, 0))]

    state_shapes = [jax.ShapeDtypeStruct((nt, ML_HEADS, ML_DK, ML_DV), BF16),
                    jax.ShapeDtypeStruct((nt, ML_HEADS, 1, ML_DK), F32),
                    jax.ShapeDtypeStruct((nt, 1, ML_HEADS), F32)]
    cf, nf, mf, cb, nb, mb = pl.pallas_call(
        functools.partial(_ml_scan_kernel, ch=ch),
        grid=(lay.batch, ck.steps),
        in_specs=[spec(ML_QK, ck.fwd, 1), spec(ML_V, ck.fwd), spec(N_GATE, ck.fwd),
                  spec(ML_QK, ck.bwd, 1), spec(ML_V, ck.bwd), spec(N_GATE, ck.bwd),
                  pl.BlockSpec((2, ch, ch), lambda b, j: (0, 0, 0))],
        out_specs=state_specs(ck.fwd) + state_specs(ck.bwd),
        out_shape=state_shapes + state_shapes,
        scratch_shapes=[pltpu.VMEM((2, ML_HEADS, ML_DK, ML_DV), F32),
                        pltpu.VMEM((2, ML_HEADS, 1, ML_DK), F32),
                        pltpu.VMEM((2, 1, ML_HEADS), F32)],
        compiler_params=_params(("parallel", "arbitrary")),
        name="mlstm_scan",
    )(mqk, mv, gc, mqk, mv, gc, tri)

    c0 = 0 if with_ctx else ck.base
    nout = nt - c0
    smem = pl.BlockSpec(memory_space=pltpu.SMEM)

    def rows(cols):
        return pl.BlockSpec((ch, cols), lambda c: (c + c0, 0))

    cst = pl.BlockSpec((1, ML_HEADS, ML_DK, ML_DV), lambda c: (c + c0, 0, 0, 0))
    nst = pl.BlockSpec((1, ML_HEADS, 1, ML_DK), lambda c: (c + c0, 0, 0, 0))
    out = pl.pallas_call(
        functools.partial(_ml_out_kernel, c0=c0, ch=ch),
        grid=(nout,),
        in_specs=[smem, smem, rows(2 * ML_QK), rows(ML_V), rows(ML_V), rows(N_GATE),
                  pl.BlockSpec((N_GATE, ch), lambda c: (0, c + c0)),
                  cst, cst, nst, nst,
                  pl.BlockSpec((2, ch, ch), lambda c: (0, 0, 0)),
                  pl.BlockSpec((1, ML_V), lambda c: (0, 0))],
        out_specs=pl.BlockSpec((ch, ML_V), lambda c: (c, 0)),
        out_shape=jax.ShapeDtypeStruct((nout * ch, ML_V), BF16),
        compiler_params=_params(("parallel",)),
        name="mlstm_out",
    )(mf.reshape(nt, ML_HEADS), mb.reshape(nt, ML_HEADS), mqk, mv, mo, gc, gr, cf, cb, nf, nb, tri, gnorm)
    if not with_ctx:
        out = jnp.concatenate([jnp.zeros((lay.nc, ML_V), BF16), out], axis=0)
    return out


def _hg_levels(cg):
    t = np.arange(cg)[:, None]
    u = np.arange(cg)[None, :]
    nlev = int(np.log2(cg))
    tst = np.zeros((2, nlev + 1, cg, cg), np.float32)
    msk = np.zeros((2, nlev + 1, cg, cg), np.float32)
    tst[0, 0] = u <= t
    tst[1, 0] = u >= t
    msk[0, 0] = msk[1, 0] = (u == t)
    for l in range(1, nlev + 1):
        bs = 2 ** l
        mid = (t // bs) * bs + bs // 2
        same = (t // bs) == (u // bs)
        right_t, right_u = (t % bs) >= bs // 2, (u % bs) >= bs // 2
        tst[0, l] = u <= mid - 1
        tst[1, l] = u >= mid
        msk[0, l] = same & right_t & ~right_u
        msk[1, l] = same & ~right_t & right_u
    return jnp.asarray(tst.reshape(2, (nlev + 1) * cg, cg)), jnp.asarray(msk), nlev


def _hg_scan_kernel(kf_ref, lff_ref, vf_ref, kb_ref, lfb_ref, vb_ref, tri_ref, sf_ref, sb_ref, s_s, *, cg):
    @pl.when(pl.program_id(1) == 0)
    def _():
        s_s[...] = jnp.zeros_like(s_s)

    streams = ((kf_ref, lff_ref, vf_ref, sf_ref), (kb_ref, lfb_ref, vb_ref, sb_ref))
    for d, (k_ref, lf_ref, v_ref, s_out) in enumerate(streams):
        s_old = s_s[d]
        s_out[0] = s_old.astype(BF16)
        b = _dot_hi(tri_ref[d], lf_ref[...])
        bl = b[cg - 1:cg] if d == 0 else b[0:1]
        kd = (k_ref[...] * jnp.exp(bl - b)).astype(BF16)
        v = v_ref[...]
        upd = [_dot_tn(v[:, HG_DV * hh:HG_DV * (hh + 1)], kd[:, HG_DK * hh:HG_DK * (hh + 1)])
               for hh in range(HG_HEADS)]
        s_s[d] = jnp.exp(bl) * s_old + jnp.concatenate(upd, axis=1)


def _hg_out_kernel(q_ref, k_ref, lf_ref, v_ref, g_ref, sf_ref, sb_ref, tst_ref, msk_ref, gn_ref, out_ref,
                   *, cg, nlev):
    q = q_ref[...]
    kk = k_ref[...]
    lff = lf_ref[...]
    v = v_ref[...]
    qb = q.astype(BF16)
    acc = [None] * HG_HEADS
    for d in range(2):
        k = kk[:, HG_K * d:HG_K * (d + 1)]
        bb = _dot_hi(tst_ref[d], lff[:, HG_K * d:HG_K * (d + 1)])
        b = bb[0:cg]
        qe = (q * jnp.exp(b)).astype(BF16)
        st = (sf_ref if d == 0 else sb_ref)[0]
        qs, ks = [qb], [k.astype(BF16)]
        for l in range(1, nlev + 1):
            ref = bb[l * cg:(l + 1) * cg]
            qs.append((q * jnp.exp(jnp.minimum(b - ref, 0.0))).astype(BF16))
            ks.append((k * jnp.exp(jnp.minimum(ref - b, 0.0))).astype(BF16))
        for hh in range(HG_HEADS):
            sl = slice(HG_DK * hh, HG_DK * (hh + 1))
            a = None
            for l in range(nlev + 1):
                term = msk_ref[d, l] * _dot_nt(qs[l][:, sl], ks[l][:, sl])
                a = term if a is None else a + term
            o = _dot_nt(qe[:, sl], st[:, sl]) + _dot(a.astype(BF16), v[:, HG_DV * hh:HG_DV * (hh + 1)])
            acc[hh] = o if acc[hh] is None else acc[hh] + o
    gn = gn_ref[...]
    gate = g_ref[...]
    outs = []
    for hh in range(HG_HEADS):
        sl = slice(HG_DV * hh, HG_DV * (hh + 1))
        outs.append(_rms(acc[hh], gn[:, sl]) * _silu(gate[:, sl]))
    out_ref[...] = jnp.concatenate(outs, axis=1).astype(BF16)


def _hgrn2(lay, hq, hk, hlf, hv, hgate, gnorm, with_ctx):
    cg = min(64, lay.lc)
    ck = _Chunks(lay, cg)
    tri = _tri(cg)
    tst, msk, nlev = _hg_levels(cg)
    nt = ck.total

    def spec(cols, fn, colblk=0):
        return pl.BlockSpec((cg, cols), lambda b, j: (fn(b, j), colblk))

    def sspec(fn):
        return pl.BlockSpec((1, HG_DV, HG_K), lambda b, j: (fn(b, j), 0, 0))

    sshape = jax.ShapeDtypeStruct((nt, HG_DV, HG_K), BF16)
    sf, sb = pl.pallas_call(
        functools.partial(_hg_scan_kernel, cg=cg),
        grid=(lay.batch, ck.steps),
        in_specs=[spec(HG_K, ck.fwd, 0), spec(HG_K, ck.fwd, 0), spec(HG_V, ck.fwd),
                  spec(HG_K, ck.bwd, 1), spec(HG_K, ck.bwd, 1), spec(HG_V, ck.bwd),
                  pl.BlockSpec((2, cg, cg), lambda b, j: (0, 0, 0))],
        out_specs=[sspec(ck.fwd), sspec(ck.bwd)],
        out_shape=[sshape, sshape],
        scratch_shapes=[pltpu.VMEM((2, HG_DV, HG_K), F32)],
        compiler_params=_params(("parallel", "arbitrary")),
        name="hgrn2_scan",
    )(hk, hlf, hv, hk, hlf, hv, tri)

    c0 = 0 if with_ctx else ck.base
    nout = nt - c0

    def rows(cols):
        return pl.BlockSpec((cg, cols), lambda c: (c + c0, 0))

    sst = pl.BlockSpec((1, HG_DV, HG_K), lambda c: (c + c0, 0, 0))
    out = pl.pallas_call(
        functools.partial(_hg_out_kernel, cg=cg, nlev=nlev),
        grid=(nout,),
        in_specs=[rows(HG_K), rows(2 * HG_K), rows(2 * HG_K), rows(HG_V), rows(HG_V), sst, sst,
                  pl.BlockSpec(tst.shape, lambda c: (0, 0, 0)),
                  pl.BlockSpec(msk.shape, lambda c: (0, 0, 0, 0)),
                  pl.BlockSpec((1, HG_V), lambda c: (0, 0))],
        out_specs=pl.BlockSpec((cg, HG_V), lambda c: (c, 0)),
        out_shape=jax.ShapeDtypeStruct((nout * cg, HG_V), BF16),
        compiler_params=_params(("parallel",)),
        name="hgrn2_out",
    )(hq, hk, hlf, hv, hgate, sf, sb, tst, msk, gnorm)
    if not with_ctx:
        out = jnp.concatenate([jnp.zeros((lay.nc, HG_V), BF16), out], axis=0)
    return out


def _rope_table(lay):
    l = lay.l
    rows = l // GRID_W
    row = jnp.repeat(jnp.arange(rows, dtype=F32), GRID_W)
    col = jnp.tile(jnp.arange(GRID_W, dtype=F32), rows)
    n_freq = ATT_HEAD_DIM // 4
    inv = ROPE_BASE ** (-jnp.arange(n_freq, dtype=F32) / n_freq)
    ar, ac = row[:, None] * inv, col[:, None] * inv
    cos = jnp.concatenate([jnp.cos(ar), jnp.cos(ar), jnp.cos(ac), jnp.cos(ac)], axis=1)
    sin = jnp.concatenate([-jnp.sin(ar), jnp.sin(ar), -jnp.sin(ac), jnp.sin(ac)], axis=1)
    lat = jnp.concatenate([cos, cos, sin, sin], axis=1)
    lat = jnp.tile(lat, (lay.batch, 1))
    ctx = jnp.concatenate([jnp.ones((lay.nc, 128), F32), jnp.zeros((lay.nc, 128), F32)], axis=1)
    return jnp.concatenate([ctx, lat], axis=0)


def kernel(x, c, ctx, c_ctx, w_ada, b_ada, norm_pre, norm_post, ffn_w1, ffn_w3, ffn_w2, w_in, att_sink,
           ml_conv, ml_f_bias, ml_norm, hg_lb_logits, hg_norm, w_branch_att, w_branch_ml, w_branch_hg,
           w_out):
    batch, l, d = x.shape
    lc = ctx.shape[1]
    depth = w_ada.shape[0]
    assert l % GRID_W == 0 and l & (l - 1) == 0 and lc & (lc - 1) == 0
    assert l % 256 == 0 and lc % 128 == 0 and batch + 1 <= MOD_ROWS
    lay = _Layout(batch, lc, l)

    xs = jnp.concatenate([ctx.reshape(batch * lc, d), x.reshape(batch * l, d)], axis=0)
    cs = jnp.concatenate([c, c_ctx[None, :], jnp.zeros((MOD_ROWS - batch - 1, d), F32)], axis=0)
    mod_all = _modulation(cs, w_ada, b_ada).reshape(depth, MOD_ROWS, N_MOD, d)
    rope = _rope_table(lay)

    lb_all = jnp.cumsum(jax.nn.softmax(hg_lb_logits.astype(F32), axis=0), axis=0)
    lb_all = lb_all - lb_all[0:1]

    sizes = (ATT_Q, ATT_KV, ATT_KV, ML_QK, ML_QK, ML_V, ML_V, 2 * ML_HEADS, 2 * ML_HEADS,
             HG_K, HG_K, HG_K, HG_V, HG_V, 3 * d)
    offs = np.concatenate([[0], np.cumsum(sizes)])
    g0, g1 = int(offs[7]), int(offs[9])

    for layer in range(depth):
        last = layer == depth - 1
        mod = mod_all[layer]
        wl = w_in[layer]
        w_gate = wl[:, g0:g1]
        w_proj = jnp.concatenate(
            [wl[:, :g0], wl[:, g1:int(offs[14])], w_gate, jnp.zeros((d, GATE_PAD - N_GATE), F32)],
            axis=1).astype(BF16)
        w_gate_t = w_gate.T.astype(BF16)
        w_brg = wl[:, int(offs[14]):].astype(BF16)
        fb = ml_f_bias[layer].reshape(1, 2 * ML_HEADS)
        fbrow = jnp.concatenate([jnp.zeros((1, 2 * ML_HEADS), F32), fb], axis=1)
        fbcol = fbrow.reshape(N_GATE, 1)
        lb = lb_all[layer]
        lbv = jnp.stack([jnp.tile(jnp.log(lb), 2), jnp.tile(jnp.log1p(-lb), 2), jnp.tile(1.0 - lb, 2)])

        def npre(i):
            return norm_pre[layer, i].reshape(1, d)

        def npost(i):
            return norm_post[layer, i].reshape(1, d)

        def ffn(xin, j, i):
            return _ffn(lay, xin, mod, j, npre(j), npost(j), ffn_w1[layer, i].astype(BF16),
                        ffn_w3[layer, i].astype(BF16), ffn_w2[layer, i].astype(BF16))

        xs = ffn(xs, 0, 0)
        (aq, akv, mqk, mv, mo, gc, gr, hq, hk, hlf, hv, hgate) = _project(
            lay, xs, mod, npre(1), w_proj, w_gate_t, rope, ml_conv[layer], fbrow, fbcol, lbv)
        att = _attention(lay, aq, akv, att_sink[layer], not last)
        ml = _mlstm(lay, mqk, mv, mo, gc, gr, ml_norm[layer].reshape(1, ML_V), not last)
        hg = _hgrn2(lay, hq, hk, hlf, hv, hgate, hg_norm[layer].reshape(1, HG_V), not last)
        xs = _merge(lay, xs, att, ml, hg, mod, npre(1), npost(1), w_brg,
                    w_branch_att[layer].astype(BF16), w_branch_ml[layer].astype(BF16),
                    w_branch_hg[layer].astype(BF16), w_out[layer].astype(BF16))
        xs = ffn(xs, 2, 1)
    return xs[lay.nc:].reshape(batch, l, d)
```

```python
import functools

import numpy as np
import jax
import jax.numpy as jnp
from jax import lax
from jax.experimental import pallas as pl
from jax.experimental.pallas import tpu as pltpu

F32 = jnp.float32
BF16 = jnp.bfloat16

ATT_HEADS = 8
ATT_KV_HEADS = 2
ATT_GROUP = ATT_HEADS // ATT_KV_HEADS
ATT_HEAD_DIM = 64
ATT_BLOCK = 128
ML_HEADS = 4
ML_DK = 64
ML_DV = 128
HG_HEADS = 4
HG_DK = 64
HG_DV = 128
GRID_W = 64
ROPE_BASE = 10000.0
EPS = 1e-6
N_MOD = 9
MOD_ROWS = 16

ATT_Q = ATT_HEADS * ATT_HEAD_DIM
ATT_KV = ATT_KV_HEADS * ATT_HEAD_DIM
ML_QK = ML_HEADS * ML_DK
ML_V = ML_HEADS * ML_DV
HG_K = HG_HEADS * HG_DK
HG_V = HG_HEADS * HG_DV
N_GATE = 4 * ML_HEADS
GATE_PAD = 128

OFF_ATT = 0
OFF_MQK = OFF_ATT + ATT_Q + 2 * ATT_KV
OFF_MVO = OFF_MQK + 2 * ML_QK
OFF_HG = OFF_MVO + 2 * ML_V
OFF_GATE = OFF_HG + 3 * HG_K + 2 * HG_V
W_PROJ_COLS = OFF_GATE + GATE_PAD

VMEM_LIMIT = 56 * 1024 * 1024


def _dot(a, b):
    return jnp.dot(a, b, preferred_element_type=F32)


def _dot_nt(a, b):
    return lax.dot_general(a, b, (((1,), (1,)), ((), ())), preferred_element_type=F32)


def _dot_tn(a, b):
    return lax.dot_general(a, b, (((0,), (0,)), ((), ())), preferred_element_type=F32)


def _rms(x, g):
    ms = jnp.mean(x * x, axis=-1, keepdims=True)
    return x * lax.rsqrt(ms + EPS) * g


def _silu(x):
    return x * jax.nn.sigmoid(x)


def _log_sigmoid(x):
    return jnp.minimum(x, 0.0) - jnp.log1p(jnp.exp(-jnp.abs(x)))


def _params(sem):
    return pltpu.CompilerParams(dimension_semantics=sem, vmem_limit_bytes=VMEM_LIMIT)


def _resident(shape):
    nd = len(shape)
    return pl.BlockSpec(shape, lambda *_: (0,) * nd, pipeline_mode=pl.Buffered(1))


def _mod_kernel(c_ref, w_ref, b_ref, o_ref):
    s = _silu(c_ref[...]).astype(BF16)
    o_ref[0] = _dot(s, w_ref[0].astype(BF16)) + b_ref[0]


def _modulation(cs, w_ada, b_ada):
    depth, d, nm = w_ada.shape
    tn = 1024
    return pl.pallas_call(
        _mod_kernel,
        grid=(depth, nm // tn),
        in_specs=[
            pl.BlockSpec((MOD_ROWS, d), lambda l, n: (0, 0)),
            pl.BlockSpec((1, d, tn), lambda l, n: (l, 0, n)),
            pl.BlockSpec((1, 1, tn), lambda l, n: (l, 0, n)),
        ],
        out_specs=pl.BlockSpec((1, MOD_ROWS, tn), lambda l, n: (l, 0, n)),
        out_shape=jax.ShapeDtypeStruct((depth, MOD_ROWS, nm), F32),
        compiler_params=_params(("parallel", "parallel")),
        name="modulation",
    )(cs, w_ada, b_ada.reshape(depth, 1, nm))


class _Layout:
    def __init__(self, batch, lc, l):
        self.batch, self.lc, self.l = batch, lc, l
        self.nc = batch * lc
        self.n = self.nc + batch * l
        tm = 512
        while self.nc % tm or l % tm:
            tm //= 2
        self.tm = tm
        self.nct = self.nc // tm
        self.tpb = l // tm
        self.ntiles = self.n // tm

    def mod_index(self, t):
        return jnp.where(t < self.nct, self.batch, (t - self.nct) // self.tpb)


def _ffn_kernel(x_ref, mod_ref, gpre_ref, gpost_ref, w1_ref, w3_ref, w2_ref, o_ref, *, j, halves):
    x = x_ref[...]
    mod = mod_ref[0]
    shift, scale, gate = mod[3 * j:3 * j + 1], mod[3 * j + 1:3 * j + 2], mod[3 * j + 2:3 * j + 3]
    h = (_rms(x, gpre_ref[...]) * (1.0 + scale) + shift).astype(BF16)
    dff = w1_ref.shape[1]
    step = dff // halves
    y = None
    for c in range(halves):
        sl = slice(c * step, (c + 1) * step)
        a = _dot(h, w1_ref[:, sl])
        b = _dot(h, w3_ref[:, sl])
        part = _dot((_silu(a) * b).astype(BF16), w2_ref[sl, :])
        y = part if y is None else y + part
    o_ref[...] = x + 0.5 * gate * _rms(y, gpost_ref[...])


def _ffn(lay, x, mod, j, gpre, gpost, w1, w3, w2):
    n, d = x.shape
    dff = w1.shape[1]
    tm = lay.tm
    halves = 2 if (dff // 2) % 128 == 0 else 1
    row = pl.BlockSpec((tm, d), lambda t: (t, 0))
    return pl.pallas_call(
        functools.partial(_ffn_kernel, j=j, halves=halves),
        grid=(lay.ntiles,),
        in_specs=[
            row,
            pl.BlockSpec((1, N_MOD, d), lambda t: (lay.mod_index(t), 0, 0)),
            _resident((1, d)), _resident((1, d)),
            _resident((d, dff)), _resident((d, dff)), _resident((dff, d)),
        ],
        out_specs=row,
        out_shape=jax.ShapeDtypeStruct((n, d), F32),
        compiler_params=_params(("parallel",)),
        name="ffn",
    )(x, mod, gpre, gpost, w1, w3, w2)


def _proj_kernel(x_ref, xp_ref, xn_ref, mod_ref, gpre_ref, w_ref, wgt_ref, rope_ref, conv_ref,
                 fbrow_ref, fbcol_ref, lbv_ref,
                 aq_ref, akv_ref, mqk_ref, mv_ref, mo_ref, gc_ref, gr_ref,
                 hq_ref, hk_ref, hlf_ref, hv_ref, hg_ref, *, nct, lc, l, tm):
    t = pl.program_id(0)
    mod = mod_ref[0]
    shift, scale = mod[3:4], mod[4:5]
    gpre = gpre_ref[...]

    def pre(xx):
        return (_rms(xx, gpre) * (1.0 + scale) + shift).astype(BF16)

    h = pre(x_ref[...])

    pa = _dot(h, w_ref[:, OFF_ATT:OFF_MQK])
    rope = rope_ref[...]
    cos, sin = rope[:, :128], rope[:, 128:]

    def swap(v):
        w = v.shape[1]
        lane = lax.broadcasted_iota(jnp.int32, v.shape, 1)
        return jnp.where((lane & 16) == 0, pltpu.roll(v, w - 16, 1), pltpu.roll(v, 16, 1))

    q = pa[:, :ATT_Q]
    k = pa[:, ATT_Q:ATT_Q + ATT_KV]
    cos4 = jnp.concatenate([cos] * (ATT_Q // 128), axis=1)
    sin4 = jnp.concatenate([sin] * (ATT_Q // 128), axis=1)
    q = (q * cos4 + swap(q) * sin4) * (ATT_HEAD_DIM ** -0.5)
    k = k * cos + swap(k) * sin
    aq_ref[...] = q.astype(BF16)
    akv_ref[...] = jnp.concatenate([k, pa[:, ATT_Q + ATT_KV:]], axis=1).astype(BF16)

    wm = w_ref[:, OFF_MQK:OFF_MVO]
    pm = _dot(h, wm)
    pprev = _dot(pre(xp_ref[...]), wm)[7:8]
    pnext = _dot(pre(xn_ref[...]), wm)[0:1]
    r = lax.broadcasted_iota(jnp.int32, (tm, 1), 0)
    grow = t * tm + r
    is_ctx = t < nct
    pos = jnp.where(is_ctx, grow & (lc - 1), (grow - nct * tm) & (l - 1))
    last = jnp.where(is_ctx, lc - 1, l - 1)
    dn = jnp.where(r == 0, pprev, pltpu.roll(pm, 1, 0))
    dn = jnp.where(pos == 0, 0.0, dn)
    up = jnp.where(r == tm - 1, pnext, pltpu.roll(pm, tm - 1, 0))
    up = jnp.where(pos == last, 0.0, up)
    cw = conv_ref[...]
    cv = dn * cw[0:1] + pm * cw[1:2] + up * cw[2:3]
    lane = lax.broadcasted_iota(jnp.int32, (1, 2 * ML_QK), 1)
    mqk_ref[...] = (_silu(cv) * jnp.where(lane < ML_QK, 1.0, ML_DK ** -0.5)).astype(BF16)

    pv = _dot(h, w_ref[:, OFF_MVO:OFF_HG])
    mv_ref[...] = pv[:, :ML_V].astype(BF16)
    mo_ref[...] = pv[:, ML_V:]

    pg = _dot(h, w_ref[:, OFF_GATE:OFF_GATE + GATE_PAD])[:, :N_GATE] + fbrow_ref[...]
    lane16 = lax.broadcasted_iota(jnp.int32, (1, N_GATE), 1)
    gc_ref[...] = jnp.where(lane16 < 2 * ML_HEADS, pg, _log_sigmoid(pg))
    pgr = _dot_nt(wgt_ref[...], h) + fbcol_ref[...]
    row16 = lax.broadcasted_iota(jnp.int32, (N_GATE, 1), 0)
    gr_ref[...] = jnp.where(row16 < 2 * ML_HEADS, pgr, _log_sigmoid(pgr))

    ph = _dot(h, w_ref[:, OFF_HG:OFF_GATE])
    hq_ref[...] = _silu(ph[:, :HG_K]) * (HG_DK ** -0.5)
    z = ph[:, HG_K:3 * HG_K]
    lbv = lbv_ref[...]
    log_lb, log_1m_lb, one_m_lb = lbv[0:1], lbv[1:2], lbv[2:3]
    bv = log_1m_lb + _log_sigmoid(z)
    hlf_ref[...] = jnp.maximum(log_lb, bv) + jnp.log1p(jnp.exp(-jnp.abs(log_lb - bv)))
    hk_ref[...] = one_m_lb * jax.nn.sigmoid(-z)
    hv_ref[...] = ph[:, 3 * HG_K:3 * HG_K + HG_V].astype(BF16)
    hg_ref[...] = ph[:, 3 * HG_K + HG_V:]


def _project(lay, x, mod, gpre, w, wgt, rope, conv, fbrow, fbcol, lbv):
    n, d = x.shape
    tm = lay.tm
    nb8 = n // 8

    def row(c):
        return pl.BlockSpec((tm, c), lambda t: (t, 0))

    out_cols = [(ATT_Q, BF16), (2 * ATT_KV, BF16), (2 * ML_QK, BF16), (ML_V, BF16), (ML_V, F32),
                (N_GATE, F32), None, (HG_K, F32), (2 * HG_K, F32), (2 * HG_K, F32), (HG_V, BF16), (HG_V, F32)]
    out_specs, out_shape = [], []
    for oc in out_cols:
        if oc is None:
            out_specs.append(pl.BlockSpec((N_GATE, tm), lambda t: (0, t)))
            out_shape.append(jax.ShapeDtypeStruct((N_GATE, n), F32))
        else:
            out_specs.append(row(oc[0]))
            out_shape.append(jax.ShapeDtypeStruct((n, oc[0]), oc[1]))
    return pl.pallas_call(
        functools.partial(_proj_kernel, nct=lay.nct, lc=lay.lc, l=lay.l, tm=tm),
        grid=(lay.ntiles,),
        in_specs=[
            row(d),
            pl.BlockSpec((8, d), lambda t: (jnp.maximum(t * (tm // 8) - 1, 0), 0)),
            pl.BlockSpec((8, d), lambda t: (jnp.minimum((t + 1) * (tm // 8), nb8 - 1), 0)),
            pl.BlockSpec((1, N_MOD, d), lambda t: (lay.mod_index(t), 0, 0)),
            _resident((1, d)),
            _resident(w.shape), _resident(wgt.shape),
            row(256),
            _resident(conv.shape), _resident(fbrow.shape), _resident(fbcol.shape), _resident(lbv.shape),
        ],
        out_specs=out_specs,
        out_shape=out_shape,
        compiler_params=_params(("parallel",)),
        name="project",
    )(x, x, x, mod, gpre, w, wgt, rope, conv, fbrow, fbcol, lbv)


def _merge_kernel(x_ref, att_ref, ml_ref, hg_ref, mod_ref, gpre_ref, gpost_ref,
                  wg_ref, wa_ref, wm_ref, wh_ref, wo_ref, o_ref):
    x = x_ref[...]
    d = x.shape[1]
    mod = mod_ref[0]
    shift, scale, gate = mod[3:4], mod[4:5], mod[5:6]
    h = (_rms(x, gpre_ref[...]) * (1.0 + scale) + shift).astype(BF16)
    y = None
    for i, (b_ref, w_ref) in enumerate(((att_ref, wa_ref), (ml_ref, wm_ref), (hg_ref, wh_ref))):
        g = jax.nn.sigmoid(_dot(h, wg_ref[:, i * d:(i + 1) * d]))
        part = g * _dot(b_ref[...], w_ref[...])
        y = part if y is None else y + part
    yy = _dot(y.astype(BF16), wo_ref[...])
    o_ref[...] = x + gate * _rms(yy, gpost_ref[...])


def _merge(lay, x, att, ml, hg, mod, gpre, gpost, wg, wa, wm, wh, wo):
    n, d = x.shape
    tm = lay.tm

    def row(c):
        return pl.BlockSpec((tm, c), lambda t: (t, 0))

    return pl.pallas_call(
        _merge_kernel,
        grid=(lay.ntiles,),
        in_specs=[
            row(d), row(ATT_Q), row(ML_V), row(HG_V),
            pl.BlockSpec((1, N_MOD, d), lambda t: (lay.mod_index(t), 0, 0)),
            _resident((1, d)), _resident((1, d)),
            _resident(wg.shape), _resident(wa.shape), _resident(wm.shape), _resident(wh.shape),
            _resident(wo.shape),
        ],
        out_specs=row(d),
        out_shape=jax.ShapeDtypeStruct((n, d), F32),
        compiler_params=_params(("parallel",)),
        name="merge",
    )(x, att, ml, hg, mod, gpre, gpost, wg, wa, wm, wh, wo)


def _swap_lane_halves(x):
    return pltpu.bitcast(pltpu.roll(pltpu.bitcast(x, jnp.uint32), 64, 1), BF16)


def _attend(q, kv, bias, sink_ref):
    tq = q.shape[0]
    assert ATT_GROUP == 4 and 2 * ATT_HEAD_DIM == 128 and ATT_KV == 128
    k, v = kv[:, :ATT_KV], kv[:, ATT_KV:]
    ks, vs = _swap_lane_halves(k), _swap_lane_halves(v)
    lo_k = lax.broadcasted_iota(jnp.int32, k.shape, 1) < ATT_HEAD_DIM
    lo_q = lax.broadcasted_iota(jnp.int32, (tq, 128), 1) < ATT_HEAD_DIM
    row = lax.broadcasted_iota(jnp.int32, (4 * tq, 1), 0)
    zero = jnp.zeros_like(k)
    zq = jnp.zeros((tq, 128), BF16)
    bias4 = None if bias is None else jnp.concatenate([bias] * 4, axis=0)
    outs = []
    for g in range(ATT_KV_HEADS):
        own, other = (k, ks) if g == 0 else (ks, k)
        k2 = jnp.where(lo_k, own, other)
        vown, voth = (v, vs) if g == 0 else (vs, v)
        va = jnp.where(lo_k, vown, zero)
        vb = jnp.where(lo_k, zero, voth)
        p0 = q[:, 256 * g:256 * g + 128]
        p1 = q[:, 256 * g + 128:256 * g + 256]
        qg = jnp.concatenate([jnp.where(lo_q, p0, zq), jnp.where(lo_q, p1, zq),
                              jnp.where(lo_q, zq, p0), jnp.where(lo_q, zq, p1)], axis=0)
        s = _dot_nt(qg, k2)
        if bias4 is not None:
            nbk = bias4.shape[1]
            s = jnp.concatenate([s[:, :nbk] + bias4, s[:, nbk:]], axis=1)
        h0 = 4 * g
        sk = jnp.where(row < tq, sink_ref[h0],
                       jnp.where(row < 2 * tq, sink_ref[h0 + 2],
                                 jnp.where(row < 3 * tq, sink_ref[h0 + 1], sink_ref[h0 + 3])))
        m = jnp.maximum(jnp.max(s, axis=1, keepdims=True), sk)
        p = jnp.exp(s - m)
        inv = 1.0 / (jnp.sum(p, axis=1, keepdims=True) + jnp.exp(sk - m))
        pb = p.astype(BF16)
        o = _dot(pb[:2 * tq], va) * inv[:2 * tq] + _dot(pb[2 * tq:], vb) * inv[2 * tq:]
        outs += [o[:tq], o[tq:]]
    return jnp.concatenate(outs, axis=1).astype(BF16)


def _attn_lat_kernel(sink_ref, q_ref, kl_ref, km_ref, kr_ref, kc_ref, o_ref, *, nb):
    i = pl.program_id(1)
    kv = jnp.concatenate([kl_ref[...], kr_ref[...], km_ref[...], kc_ref[...]], axis=0)
    r = lax.broadcasted_iota(jnp.int32, (ATT_BLOCK, ATT_BLOCK), 0)
    j = lax.broadcasted_iota(jnp.int32, (ATT_BLOCK, ATT_BLOCK), 1)
    ninf = jnp.float32(-jnp.inf)
    left = jnp.where((j >= r) & (i > 0), 0.0, ninf)
    right = jnp.where((j <= r) & (i < nb - 1), 0.0, ninf)
    bias = jnp.concatenate([left, right], axis=1)
    o_ref[...] = _attend(q_ref[...], kv, bias, sink_ref)


def _attn_ctx_kernel(sink_ref, q_ref, kc_ref, o_ref):
    o_ref[...] = _attend(q_ref[...], kc_ref[...], None, sink_ref)


def _attention(lay, aq, akv, sink, with_ctx):
    n = aq.shape[0]
    nb = lay.l // ATT_BLOCK
    base = lay.nc // ATT_BLOCK
    lc = lay.lc
    smem = pl.BlockSpec(memory_space=pltpu.SMEM)
    kvw = 2 * ATT_KV

    def band(off):
        return pl.BlockSpec((ATT_BLOCK, kvw),
                            lambda b, i: (base + b * nb + jnp.clip(i + off, 0, nb - 1), 0))

    out = pl.pallas_call(
        functools.partial(_attn_lat_kernel, nb=nb),
        grid=(lay.batch, nb),
        in_specs=[smem,
                  pl.BlockSpec((ATT_BLOCK, ATT_Q), lambda b, i: (base + b * nb + i, 0)),
                  band(-1), band(0), band(1),
                  pl.BlockSpec((lc, kvw), lambda b, i: (b, 0))],
        out_specs=pl.BlockSpec((ATT_BLOCK, ATT_Q), lambda b, i: (b * nb + i, 0)),
        out_shape=jax.ShapeDtypeStruct((lay.batch * lay.l, ATT_Q), BF16),
        compiler_params=_params(("parallel", "parallel")),
        name="attn_latent",
    )(sink, aq, akv, akv, akv, akv)
    if not with_ctx:
        return jnp.concatenate([jnp.zeros((lay.nc, ATT_Q), BF16), out], axis=0)
    ncb = lc // ATT_BLOCK
    out_c = pl.pallas_call(
        _attn_ctx_kernel,
        grid=(lay.batch, ncb),
        in_specs=[smem,
                  pl.BlockSpec((ATT_BLOCK, ATT_Q), lambda b, i: (b * ncb + i, 0)),
                  pl.BlockSpec((lc, kvw), lambda b, i: (b, 0))],
        out_specs=pl.BlockSpec((ATT_BLOCK, ATT_Q), lambda b, i: (b * ncb + i, 0)),
        out_shape=jax.ShapeDtypeStruct((lay.nc, ATT_Q), BF16),
        compiler_params=_params(("parallel", "parallel")),
        name="attn_context",
    )(sink, aq, akv)
    return jnp.concatenate([out_c, out], axis=0)


class _Chunks:
    def __init__(self, lay, ch):
        self.ch = ch
        self.nctx = lay.lc // ch
        self.nlat = lay.l // ch
        self.base = lay.nc // ch
        self.steps = self.nctx + self.nlat
        self.total = lay.n // ch

    def fwd(self, b, j):
        return jnp.where(j < self.nctx, b * self.nctx + j, self.base + b * self.nlat + (j - self.nctx))

    def bwd(self, b, j):
        return jnp.where(j < self.nctx, b * self.nctx + (self.nctx - 1 - j),
                         self.base + b * self.nlat + (self.nlat - 1 - (j - self.nctx)))


def _ml_scan_kernel(kf_ref, vf_ref, gf_ref, kb_ref, vb_ref, gb_ref, tri_ref,
                    cf_ref, nf_ref, mf_ref, cb_ref, nb_ref, mb_ref, c_s, n_s, m_s, *, rows, cg):
    @pl.when(pl.program_id(1) == 0)
    def _():
        c_s[...] = jnp.zeros_like(c_s)
        n_s[...] = jnp.zeros_like(n_s)
        m_s[...] = jnp.zeros_like(m_s)

    nsub = rows // cg
    streams = ((kf_ref, vf_ref, gf_ref, cf_ref, nf_ref, mf_ref),
               (kb_ref, vb_ref, gb_ref, cb_ref, nb_ref, mb_ref))
    for d, (k_ref, v_ref, g_ref, c_out, n_out, m_out) in enumerate(streams):
        gc = g_ref[...]
        ig = gc[:, ML_HEADS * d:ML_HEADS * (d + 1)]
        lf = gc[:, 2 * ML_HEADS + ML_HEADS * d:2 * ML_HEADS + ML_HEADS * (d + 1)]
        b = _cumsum3(tri_ref[d], lf)
        a = ig - b
        last = cg - 1 if d == 0 else 0
        gmaxs = [jnp.max(a[c * cg:(c + 1) * cg], axis=0, keepdims=True) for c in range(nsub)]
        u = jnp.exp(a - jnp.concatenate([jnp.broadcast_to(g, (cg, ML_HEADS)) for g in gmaxs], axis=0))
        k = k_ref[...].astype(F32)
        v = v_ref[...]
        m = m_s[d]
        for c in (range(nsub) if d == 0 else reversed(range(nsub))):
            rs = slice(c * cg, (c + 1) * cg)
            mx = jnp.maximum(m, gmaxs[c])
            alpha = jnp.exp(m - mx)
            beta = jnp.exp(gmaxs[c] - mx)
            m_out[c] = m
            for hh in range(ML_HEADS):
                c_old = c_s[d, hh]
                n_old = n_s[d, hh]
                c_out[c, hh] = c_old.astype(BF16)
                n_out[c, hh] = n_old
                ku = k[rs, ML_DK * hh:ML_DK * (hh + 1)] * u[rs, hh:hh + 1]
                al, be = alpha[:, hh:hh + 1], beta[:, hh:hh + 1]
                c_s[d, hh] = al * c_old + be * _dot_tn(ku.astype(BF16), v[rs, ML_DV * hh:ML_DV * (hh + 1)])
                n_s[d, hh] = al * n_old + be * jnp.sum(ku, axis=0, keepdims=True)
            m = b[c * cg + last:c * cg + last + 1] + mx
        m_s[d] = m


def _ml_out_kernel(mf_ref, mb_ref, qk_ref, v_ref, o_ref, gc_ref, gr_ref, cf_ref, cb_ref, nf_ref, nb_ref,
                   tri_ref, gn_ref, out_ref, *, c0, rows, cg):
    step = pl.program_id(0) + c0
    nsub = rows // cg
    qk = qk_ref[...]
    v = v_ref[...]
    gc = gc_ref[...]
    gr = gr_ref[...]
    rr = lax.broadcasted_iota(jnp.int32, (cg, cg), 0)
    cc = lax.broadcasted_iota(jnp.int32, (cg, cg), 1)
    ninf = jnp.float32(-jnp.inf)
    hs = [[None] * ML_HEADS for _ in range(nsub)]
    for d in range(2):
        tri = tri_ref[d]
        lo = 2 * ML_HEADS + ML_HEADS * d
        bcol = _cumsum3(tri, gc[:, lo:lo + ML_HEADS])
        lfr = gr[lo:lo + ML_HEADS, :]
        hi = lfr.astype(BF16)
        r1 = lfr - hi.astype(F32)
        mid = r1.astype(BF16)
        low = (r1 - mid.astype(F32)).astype(BF16)
        bb = _dot_nt(jnp.concatenate([hi, mid, low], axis=0), tri)
        brow = bb[:ML_HEADS] + bb[ML_HEADS:2 * ML_HEADS] + bb[2 * ML_HEADS:]
        arow = gr[ML_HEADS * d:ML_HEADS * (d + 1), :] - brow
        mask = (cc <= rr) if d == 0 else (cc >= rr)
        m_ref = mf_ref if d == 0 else mb_ref
        c_ref = cf_ref if d == 0 else cb_ref
        n_ref = nf_ref if d == 0 else nb_ref
        for c in range(nsub):
            rs = slice(c * cg, (c + 1) * cg)
            for hh in range(ML_HEADS):
                m_prev = m_ref[step * nsub + c, hh]
                am = jnp.where(mask, arow[hh:hh + 1, rs], ninf)
                g = jnp.max(am, axis=1, keepdims=True)
                e = jnp.exp(am - g)
                q = qk[rs, ML_DK * hh:ML_DK * (hh + 1)]
                k = qk[rs, ML_QK + ML_DK * hh:ML_QK + ML_DK * (hh + 1)]
                s = _dot_nt(q, k) * e
                den_i = jnp.sum(s, axis=1, keepdims=True)
                num_i = _dot(s.astype(BF16), v[rs, ML_DV * hh:ML_DV * (hh + 1)])
                mt = jnp.maximum(g, m_prev)
                rf = jnp.exp(g - mt)
                wp = jnp.exp(m_prev - mt)
                inter = _dot(q, c_ref[c, hh])
                dn = jnp.sum(q.astype(F32) * n_ref[c, hh], axis=1, keepdims=True)
                num = wp * inter + rf * num_i
                den = wp * dn + rf * den_i
                hv = num / jnp.maximum(jnp.abs(den), jnp.exp(-bcol[rs, hh:hh + 1] - mt))
                hs[c][hh] = hv if d == 0 else hs[c][hh] + hv
    gn = gn_ref[...]
    og = o_ref[...]
    for c in range(nsub):
        rs = slice(c * cg, (c + 1) * cg)
        outs = []
        for hh in range(ML_HEADS):
            sl = slice(ML_DV * hh, ML_DV * (hh + 1))
            outs.append(_rms(hs[c][hh], gn[:, sl]) * jax.nn.sigmoid(og[rs, sl]))
        out_ref[rs, :] = jnp.concatenate(outs, axis=1).astype(BF16)


def _mlstm(lay, mqk, mv, mo, gc, gr, gnorm, with_ctx):
    cg = min(256, lay.lc)
    orows = min(256, lay.lc)
    nsub = orows // cg
    ck = _Chunks(lay, orows)
    tri = _chunk_tri(orows, cg)
    nt = lay.n // cg

    def spec(cols, fn, colblk=0):
        return pl.BlockSpec((orows, cols), lambda b, j: (fn(b, j), colblk))

    def state_specs(fn):
        return [pl.BlockSpec((nsub, ML_HEADS, ML_DK, ML_DV), lambda b, j: (fn(b, j), 0, 0, 0)),
                pl.BlockSpec((nsub, ML_HEADS, 1, ML_DK), lambda b, j: (fn(b, j), 0, 0, 0)),
                pl.BlockSpec((nsub, 1, ML_HEADS), lambda b, j: (fn(b, j), 0, 0))]

    state_shapes = [jax.ShapeDtypeStruct((nt, ML_HEADS, ML_DK, ML_DV), BF16),
                    jax.ShapeDtypeStruct((nt, ML_HEADS, 1, ML_DK), F32),
                    jax.ShapeDtypeStruct((nt, 1, ML_HEADS), F32)]
    cf, nf, mf, cb, nb, mb = pl.pallas_call(
        functools.partial(_ml_scan_kernel, rows=orows, cg=cg),
        grid=(lay.batch, ck.steps),
        in_specs=[spec(ML_QK, ck.fwd, 1), spec(ML_V, ck.fwd), spec(N_GATE, ck.fwd),
                  spec(ML_QK, ck.bwd, 1), spec(ML_V, ck.bwd), spec(N_GATE, ck.bwd),
                  pl.BlockSpec((2, orows, orows), lambda b, j: (0, 0, 0))],
        out_specs=state_specs(ck.fwd) + state_specs(ck.bwd),
        out_shape=state_shapes + state_shapes,
        scratch_shapes=[pltpu.VMEM((2, ML_HEADS, ML_DK, ML_DV), F32),
                        pltpu.VMEM((2, ML_HEADS, 1, ML_DK), F32),
                        pltpu.VMEM((2, 1, ML_HEADS), F32)],
        compiler_params=_params(("parallel", "arbitrary")),
        name="mlstm_scan",
    )(mqk, mv, gc, mqk, mv, gc, tri)

    c0 = 0 if with_ctx else lay.nc // orows
    nout = lay.n // orows - c0
    smem = pl.BlockSpec(memory_space=pltpu.SMEM)

    def rows(cols):
        return pl.BlockSpec((orows, cols), lambda c: (c + c0, 0))

    cst = pl.BlockSpec((nsub, ML_HEADS, ML_DK, ML_DV), lambda c: (c + c0, 0, 0, 0))
    nst = pl.BlockSpec((nsub, ML_HEADS, 1, ML_DK), lambda c: (c + c0, 0, 0, 0))
    out = pl.pallas_call(
        functools.partial(_ml_out_kernel, c0=c0, rows=orows, cg=cg),
        grid=(nout,),
        in_specs=[smem, smem, rows(2 * ML_QK), rows(ML_V), rows(ML_V), rows(N_GATE),
                  pl.BlockSpec((N_GATE, orows), lambda c: (0, c + c0)),
                  cst, cst, nst, nst,
                  pl.BlockSpec((2, orows, orows), lambda c: (0, 0, 0)),
                  pl.BlockSpec((1, ML_V), lambda c: (0, 0))],
        out_specs=pl.BlockSpec((orows, ML_V), lambda c: (c, 0)),
        out_shape=jax.ShapeDtypeStruct((nout * orows, ML_V), BF16),
        compiler_params=_params(("parallel",)),
        name="mlstm_out",
    )(mf.reshape(nt, ML_HEADS), mb.reshape(nt, ML_HEADS), mqk, mv, mo, gc, gr, cf, cb, nf, nb, tri, gnorm)
    if not with_ctx:
        out = jnp.concatenate([jnp.zeros((lay.nc, ML_V), BF16), out], axis=0)
    return out


def _chunk_tri(rows, cg):
    t = np.arange(rows)[:, None]
    u = np.arange(rows)[None, :]
    same = (t // cg) == (u // cg)
    return jnp.asarray(np.stack([same & (u <= t), same & (u >= t)]).astype(np.float32)).astype(BF16)


def _cumsum3(tri, x):
    w = x.shape[1]
    hi = x.astype(BF16)
    r1 = x - hi.astype(F32)
    mid = r1.astype(BF16)
    lo = (r1 - mid.astype(F32)).astype(BF16)
    bb = _dot(tri, jnp.concatenate([hi, mid, lo], axis=1))
    return bb[:, :w] + bb[:, w:2 * w] + bb[:, 2 * w:]


def _level_ref(b, bs, d):
    rows, w = b.shape
    off = bs // 2 - 1 + d
    if bs >= 8:
        pieces = [jnp.broadcast_to(b[s + off:s + off + 1], (bs, w)) for s in range(0, rows, bs)]
        return jnp.concatenate(pieces, axis=0)
    b8 = b.reshape(rows // 8, 8, w)
    u = lax.broadcasted_iota(jnp.int32, (1, 8, 1), 1)
    out = None
    for s in range(0, 8, bs):
        piece = jnp.broadcast_to(b8[:, s + off:s + off + 1, :], b8.shape)
        out = piece if out is None else jnp.where(u >= s, piece, out)
    return out.reshape(rows, w)


def _hg_scan_kernel(kf_ref, lff_ref, vf_ref, kb_ref, lfb_ref, vb_ref, tri_ref, sf_ref, sb_ref, s_s, *, rows, cg):
    @pl.when(pl.program_id(1) == 0)
    def _():
        s_s[...] = jnp.zeros_like(s_s)

    nsub = rows // cg
    streams = ((kf_ref, lff_ref, vf_ref, sf_ref), (kb_ref, lfb_ref, vb_ref, sb_ref))
    for d, (k_ref, lf_ref, v_ref, s_out) in enumerate(streams):
        b = _cumsum3(tri_ref[d], lf_ref[...])
        last = cg - 1 if d == 0 else 0
        bls = [b[c * cg + last:c * cg + last + 1] for c in range(nsub)]
        bl_rows = jnp.concatenate([jnp.broadcast_to(bl, (cg, HG_K)) for bl in bls], axis=0)
        kd = (k_ref[...] * jnp.exp(bl_rows - b)).astype(BF16)
        v = v_ref[...]
        s = s_s[d]
        for c in (range(nsub) if d == 0 else reversed(range(nsub))):
            rs = slice(c * cg, (c + 1) * cg)
            s_out[c] = s.astype(BF16)
            upd = [_dot_tn(v[rs, HG_DV * hh:HG_DV * (hh + 1)], kd[rs, HG_DK * hh:HG_DK * (hh + 1)])
                   for hh in range(HG_HEADS)]
            s = jnp.exp(bls[c]) * s + jnp.concatenate(upd, axis=1)
        s_s[d] = s


def _hg_out_kernel(q_ref, k_ref, lf_ref, v_ref, g_ref, sf_ref, sb_ref, tri_ref, gn_ref, out_ref,
                   *, rows, cg, nlev):
    q = q_ref[...]
    kk = k_ref[...]
    lff = lf_ref[...]
    v = v_ref[...]
    nsub = rows // cg
    qb = q.astype(BF16)
    t_i = lax.broadcasted_iota(jnp.int32, (cg, cg), 0)
    s_i = lax.broadcasted_iota(jnp.int32, (cg, cg), 1)
    xr = t_i ^ s_i
    level = jnp.zeros((cg, cg), jnp.int32)
    for l in range(nlev):
        level = level + (xr >= 2 ** l).astype(jnp.int32)
    amat = [[None] * HG_HEADS for _ in range(nsub)]
    inter = [[None] * HG_HEADS for _ in range(nsub)]
    for d in range(2):
        k = kk[:, HG_K * d:HG_K * (d + 1)]
        b = _cumsum3(tri_ref[d], lff[:, HG_K * d:HG_K * (d + 1)])
        qe = (q * jnp.exp(b)).astype(BF16)
        ops = [(qb, k.astype(BF16))]
        for l in range(1, nlev + 1):
            e = jnp.exp(-jnp.abs(b - _level_ref(b, 2 ** l, d)))
            ops.append(((q * e).astype(BF16), (k * e).astype(BF16)))
        lev_d = jnp.where((t_i > s_i) if d == 0 else (t_i < s_i), level, -1)
        lev_d = jnp.where(t_i == s_i, 0, lev_d)
        s_ref = sf_ref if d == 0 else sb_ref
        for c in range(nsub):
            rs = slice(c * cg, (c + 1) * cg)
            st = s_ref[c]
            for hh in range(HG_HEADS):
                sl = slice(HG_DK * hh, HG_DK * (hh + 1))
                a = jnp.zeros((cg, cg), F32)
                for l in range(nlev + 1):
                    a = jnp.where(lev_d == l, _dot_nt(ops[l][0][rs, sl], ops[l][1][rs, sl]), a)
                it = _dot_nt(qe[rs, sl], st[:, sl])
                amat[c][hh] = a if d == 0 else amat[c][hh] + a
                inter[c][hh] = it if d == 0 else inter[c][hh] + it
    gn = gn_ref[...]
    gate = g_ref[...]
    for c in range(nsub):
        rs = slice(c * cg, (c + 1) * cg)
        outs = []
        for hh in range(HG_HEADS):
            sl = slice(HG_DV * hh, HG_DV * (hh + 1))
            o = inter[c][hh] + _dot(amat[c][hh].astype(BF16), v[rs, sl])
            outs.append(_rms(o, gn[:, sl]) * _silu(gate[rs, sl]))
        out_ref[rs, :] = jnp.concatenate(outs, axis=1).astype(BF16)


def _hgrn2(lay, hq, hk, hlf, hv, hgate, gnorm, with_ctx):
    cg = min(64, lay.lc)
    orows = min(256, lay.lc)
    nsub = orows // cg
    nlev = int(np.log2(cg))
    ck = _Chunks(lay, orows)
    tri = _chunk_tri(orows, cg)
    nt = lay.n // cg

    def spec(cols, fn, colblk=0):
        return pl.BlockSpec((orows, cols), lambda b, j: (fn(b, j), colblk))

    def sspec(fn):
        return pl.BlockSpec((nsub, HG_DV, HG_K), lambda b, j: (fn(b, j), 0, 0))

    sshape = jax.ShapeDtypeStruct((nt, HG_DV, HG_K), BF16)
    sf, sb = pl.pallas_call(
        functools.partial(_hg_scan_kernel, rows=orows, cg=cg),
        grid=(lay.batch, ck.steps),
        in_specs=[spec(HG_K, ck.fwd, 0), spec(HG_K, ck.fwd, 0), spec(HG_V, ck.fwd),
                  spec(HG_K, ck.bwd, 1), spec(HG_K, ck.bwd, 1), spec(HG_V, ck.bwd),
                  pl.BlockSpec((2, orows, orows), lambda b, j: (0, 0, 0))],
        out_specs=[sspec(ck.fwd), sspec(ck.bwd)],
        out_shape=[sshape, sshape],
        scratch_shapes=[pltpu.VMEM((2, HG_DV, HG_K), F32)],
        compiler_params=_params(("parallel", "arbitrary")),
        name="hgrn2_scan",
    )(hk, hlf, hv, hk, hlf, hv, tri)

    c0 = 0 if with_ctx else lay.nc // orows
    nout = lay.n // orows - c0

    def rows(cols):
        return pl.BlockSpec((orows, cols), lambda c: (c + c0, 0))

    sst = pl.BlockSpec((nsub, HG_DV, HG_K), lambda c: (c + c0, 0, 0))
    out = pl.pallas_call(
        functools.partial(_hg_out_kernel, rows=orows, cg=cg, nlev=nlev),
        grid=(nout,),
        in_specs=[rows(HG_K), rows(2 * HG_K), rows(2 * HG_K), rows(HG_V), rows(HG_V), sst, sst,
                  pl.BlockSpec((2, orows, orows), lambda c: (0, 0, 0)),
                  pl.BlockSpec((1, HG_V), lambda c: (0, 0))],
        out_specs=pl.BlockSpec((orows, HG_V), lambda c: (c, 0)),
        out_shape=jax.ShapeDtypeStruct((nout * orows, HG_V), BF16),
        compiler_params=_params(("parallel",)),
        name="hgrn2_out",
    )(hq, hk, hlf, hv, hgate, sf, sb, tri, gnorm)
    if not with_ctx:
        out = jnp.concatenate([jnp.zeros((lay.nc, HG_V), BF16), out], axis=0)
    return out


def _rope_table(lay):
    l = lay.l
    rows = l // GRID_W
    row = jnp.repeat(jnp.arange(rows, dtype=F32), GRID_W)
    col = jnp.tile(jnp.arange(GRID_W, dtype=F32), rows)
    n_freq = ATT_HEAD_DIM // 4
    inv = ROPE_BASE ** (-jnp.arange(n_freq, dtype=F32) / n_freq)
    ar, ac = row[:, None] * inv, col[:, None] * inv
    cos = jnp.concatenate([jnp.cos(ar), jnp.cos(ar), jnp.cos(ac), jnp.cos(ac)], axis=1)
    sin = jnp.concatenate([-jnp.sin(ar), jnp.sin(ar), -jnp.sin(ac), jnp.sin(ac)], axis=1)
    lat = jnp.concatenate([cos, cos, sin, sin], axis=1)
    lat = jnp.tile(lat, (lay.batch, 1))
    ctx = jnp.concatenate([jnp.ones((lay.nc, 128), F32), jnp.zeros((lay.nc, 128), F32)], axis=1)
    return jnp.concatenate([ctx, lat], axis=0)


def kernel(x, c, ctx, c_ctx, w_ada, b_ada, norm_pre, norm_post, ffn_w1, ffn_w3, ffn_w2, w_in, att_sink,
           ml_conv, ml_f_bias, ml_norm, hg_lb_logits, hg_norm, w_branch_att, w_branch_ml, w_branch_hg,
           w_out):
    batch, l, d = x.shape
    lc = ctx.shape[1]
    depth = w_ada.shape[0]
    assert l % GRID_W == 0 and l & (l - 1) == 0 and lc & (lc - 1) == 0
    assert l % 256 == 0 and lc % 128 == 0 and batch + 1 <= MOD_ROWS
    lay = _Layout(batch, lc, l)

    xs = jnp.concatenate([ctx.reshape(batch * lc, d), x.reshape(batch * l, d)], axis=0)
    cs = jnp.concatenate([c, c_ctx[None, :], jnp.zeros((MOD_ROWS - batch - 1, d), F32)], axis=0)
    mod_all = _modulation(cs, w_ada, b_ada).reshape(depth, MOD_ROWS, N_MOD, d)
    rope = _rope_table(lay)

    lb_all = jnp.cumsum(jax.nn.softmax(hg_lb_logits.astype(F32), axis=0), axis=0)
    lb_all = lb_all - lb_all[0:1]

    sizes = (ATT_Q, ATT_KV, ATT_KV, ML_QK, ML_QK, ML_V, ML_V, 2 * ML_HEADS, 2 * ML_HEADS,
             HG_K, HG_K, HG_K, HG_V, HG_V, 3 * d)
    offs = np.concatenate([[0], np.cumsum(sizes)])
    g0, g1 = int(offs[7]), int(offs[9])

    for layer in range(depth):
        last = layer == depth - 1
        mod = mod_all[layer]
        wl = w_in[layer]
        w_gate = wl[:, g0:g1]
        w_proj = jnp.concatenate(
            [wl[:, :g0], wl[:, g1:int(offs[14])], w_gate, jnp.zeros((d, GATE_PAD - N_GATE), F32)],
            axis=1).astype(BF16)
        w_gate_t = w_gate.T.astype(BF16)
        w_brg = wl[:, int(offs[14]):].astype(BF16)
        fb = ml_f_bias[layer].reshape(1, 2 * ML_HEADS)
        fbrow = jnp.concatenate([jnp.zeros((1, 2 * ML_HEADS), F32), fb], axis=1)
        fbcol = fbrow.reshape(N_GATE, 1)
        lb = lb_all[layer]
        lbv = jnp.stack([jnp.tile(jnp.log(lb), 2), jnp.tile(jnp.log1p(-lb), 2), jnp.tile(1.0 - lb, 2)])

        def npre(i):
            return norm_pre[layer, i].reshape(1, d)

        def npost(i):
            return norm_post[layer, i].reshape(1, d)

        def ffn(xin, j, i):
            return _ffn(lay, xin, mod, j, npre(j), npost(j), ffn_w1[layer, i].astype(BF16),
                        ffn_w3[layer, i].astype(BF16), ffn_w2[layer, i].astype(BF16))

        xs = ffn(xs, 0, 0)
        (aq, akv, mqk, mv, mo, gc, gr, hq, hk, hlf, hv, hgate) = _project(
            lay, xs, mod, npre(1), w_proj, w_gate_t, rope, ml_conv[layer], fbrow, fbcol, lbv)
        att = _attention(lay, aq, akv, att_sink[layer], not last)
        ml = _mlstm(lay, mqk, mv, mo, gc, gr, ml_norm[layer].reshape(1, ML_V), not last)
        hg = _hgrn2(lay, hq, hk, hlf, hv, hgate, hg_norm[layer].reshape(1, HG_V), not last)
        xs = _merge(lay, xs, att, ml, hg, mod, npre(1), npost(1), w_brg,
                    w_branch_att[layer].astype(BF16), w_branch_ml[layer].astype(BF16),
                    w_branch_hg[layer].astype(BF16), w_out[layer].astype(BF16))
        xs = ffn(xs, 2, 1)
    return xs[lay.nc:].reshape(batch, l, d)
```

```python
import functools

import numpy as np
import jax
import jax.numpy as jnp
from jax import lax
from jax.experimental import pallas as pl
from jax.experimental.pallas import tpu as pltpu

F32 = jnp.float32
BF16 = jnp.bfloat16

ATT_HEADS = 8
ATT_KV_HEADS = 2
ATT_GROUP = ATT_HEADS // ATT_KV_HEADS
ATT_HEAD_DIM = 64
ATT_BLOCK = 128
ML_HEADS = 4
ML_DK = 64
ML_DV = 128
HG_HEADS = 4
HG_DK = 64
HG_DV = 128
GRID_W = 64
ROPE_BASE = 10000.0
EPS = 1e-6
N_MOD = 9
MOD_ROWS = 16

ATT_Q = ATT_HEADS * ATT_HEAD_DIM
ATT_KV = ATT_KV_HEADS * ATT_HEAD_DIM
ML_QK = ML_HEADS * ML_DK
ML_V = ML_HEADS * ML_DV
HG_K = HG_HEADS * HG_DK
HG_V = HG_HEADS * HG_DV
N_GATE = 4 * ML_HEADS
GATE_PAD = 128

OFF_ATT = 0
OFF_MQK = OFF_ATT + ATT_Q + 2 * ATT_KV
OFF_MVO = OFF_MQK + 2 * ML_QK
OFF_HG = OFF_MVO + 2 * ML_V
OFF_GATE = OFF_HG + 3 * HG_K + 2 * HG_V
W_PROJ_COLS = OFF_GATE + GATE_PAD

VMEM_LIMIT = 56 * 1024 * 1024
FFN_TM = 1024
FFN_SPLIT = 11


def _dot(a, b):
    return jnp.dot(a, b, preferred_element_type=F32)


def _dot_nt(a, b):
    return lax.dot_general(a, b, (((1,), (1,)), ((), ())), preferred_element_type=F32)


def _dot_tn(a, b):
    return lax.dot_general(a, b, (((0,), (0,)), ((), ())), preferred_element_type=F32)


def _rms(x, g):
    ms = jnp.mean(x * x, axis=-1, keepdims=True)
    return x * lax.rsqrt(ms + EPS) * g


def _silu(x):
    return x * jax.nn.sigmoid(x)


def _log_sigmoid(x):
    return jnp.minimum(x, 0.0) - jnp.log1p(jnp.exp(-jnp.abs(x)))


def _head_lanes(x, w):
    r, h = x.shape
    return jnp.concatenate([jnp.broadcast_to(x[:, i:i + 1], (r, w)) for i in range(h)], axis=1)


def _split3(x):
    hi = x.astype(BF16)
    r1 = x - hi.astype(F32)
    mid = r1.astype(BF16)
    return hi, mid, (r1 - mid.astype(F32)).astype(BF16)


def _params(sem):
    return pltpu.CompilerParams(dimension_semantics=sem, vmem_limit_bytes=VMEM_LIMIT)


def _resident(shape):
    nd = len(shape)
    return pl.BlockSpec(shape, lambda *_: (0,) * nd, pipeline_mode=pl.Buffered(1))


def _mod_kernel(c_ref, w_ref, b_ref, o_ref):
    s = _silu(c_ref[...]).astype(BF16)
    o_ref[0] = _dot(s, w_ref[0].astype(BF16)) + b_ref[0]


def _modulation(cs, w_ada, b_ada):
    depth, d, nm = w_ada.shape
    tn = 1024
    return pl.pallas_call(
        _mod_kernel,
        grid=(depth, nm // tn),
        in_specs=[
            pl.BlockSpec((MOD_ROWS, d), lambda l, n: (0, 0)),
            pl.BlockSpec((1, d, tn), lambda l, n: (l, 0, n)),
            pl.BlockSpec((1, 1, tn), lambda l, n: (l, 0, n)),
        ],
        out_specs=pl.BlockSpec((1, MOD_ROWS, tn), lambda l, n: (l, 0, n)),
        out_shape=jax.ShapeDtypeStruct((depth, MOD_ROWS, nm), F32),
        compiler_params=_params(("parallel", "parallel")),
        name="modulation",
    )(cs, w_ada, b_ada.reshape(depth, 1, nm))


class _Layout:
    def __init__(self, batch, lc, l, tm=512):
        self.batch, self.lc, self.l = batch, lc, l
        self.nc = batch * lc
        self.n = self.nc + batch * l
        while self.nc % tm or l % tm:
            tm //= 2
        self.tm = tm
        self.nct = self.nc // tm
        self.tpb = l // tm
        self.ntiles = self.n // tm

    def mod_index(self, t):
        return jnp.where(t < self.nct, self.batch, (t - self.nct) // self.tpb)


def _ffn_split_kernel(c_ref, x_ref, *rest, nct, **kw):
    _ffn_body(jnp.where(pl.program_id(0) < nct, c_ref[...], x_ref[...]), *rest, **kw)


def _ffn_kernel(x_ref, *rest, **kw):
    _ffn_body(x_ref[...], *rest, **kw)


def _ffn_body(x, mod_ref, gpre_ref, gpost_ref, w1_ref, w3_ref, w2_ref, o_ref, *, j, halves):
    mod = mod_ref[0]
    shift, scale, gate = mod[3 * j:3 * j + 1], mod[3 * j + 1:3 * j + 2], mod[3 * j + 2:3 * j + 3]
    h = (_rms(x, gpre_ref[...]) * (1.0 + scale) + shift).astype(BF16)
    dff = w1_ref.shape[1]
    step = dff // halves
    y = None
    for c in range(halves):
        sl = slice(c * step, (c + 1) * step)
        a = _dot(h, w1_ref[:, sl])
        b = _dot(h, w3_ref[:, sl])
        part = _dot((_silu(a) * b).astype(BF16), w2_ref[sl, :])
        y = part if y is None else y + part
    o_ref[...] = x + 0.5 * gate * _rms(y, gpost_ref[...])


def _ffn(lay, xin, mod, j, gpre, gpost, w1, w3, w2, latent_only=False):
    d, dff = w1.shape
    tm = lay.tm
    halves = FFN_SPLIT if (dff // FFN_SPLIT) % 128 == 0 else 1
    tile0 = lay.nct if latent_only else 0
    ntiles = lay.ntiles - tile0
    row = pl.BlockSpec((tm, d), lambda t: (t, 0))
    if isinstance(xin, tuple):
        body = functools.partial(_ffn_split_kernel, nct=lay.nct, j=j, halves=halves)
        xs_specs = [pl.BlockSpec((tm, d), lambda t: (jnp.minimum(t, lay.nct - 1), 0)),
                    pl.BlockSpec((tm, d), lambda t: (jnp.maximum(t - lay.nct, 0), 0))]
    else:
        body = functools.partial(_ffn_kernel, j=j, halves=halves)
        xs_specs, xin = [row], (xin,)
    return pl.pallas_call(
        body,
        grid=(ntiles,),
        in_specs=xs_specs + [
            pl.BlockSpec((1, N_MOD, d), lambda t: (lay.mod_index(t + tile0), 0, 0)),
            _resident((1, d)), _resident((1, d)),
            _resident((d, dff)), _resident((d, dff)), _resident((dff, d)),
        ],
        out_specs=row,
        out_shape=jax.ShapeDtypeStruct((ntiles * tm, d), F32),
        compiler_params=_params(("parallel",)),
        name="ffn",
    )(*xin, mod, gpre, gpost, w1, w3, w2)


def _proj_kernel(x_ref, xp_ref, xn_ref, mod_ref, gpre_ref, w_ref, wgt_ref, rope_ref, conv_ref,
                 fbrow_ref, fbcol_ref, lbv_ref,
                 aq_ref, akv_ref, mqk_ref, mv_ref, mo_ref, gc_ref, gr_ref,
                 hq_ref, hk_ref, hlf_ref, hv_ref, hg_ref, *, nct, lc, l, tm):
    t = pl.program_id(0)
    mod = mod_ref[0]
    shift, scale = mod[3:4], mod[4:5]
    gpre = gpre_ref[...]

    def pre(xx):
        return (_rms(xx, gpre) * (1.0 + scale) + shift).astype(BF16)

    h = pre(x_ref[...])

    pa = _dot(h, w_ref[:, OFF_ATT:OFF_MQK])
    rope = rope_ref[...]
    cos, sin = rope[:, :128], rope[:, 128:]

    def swap(v):
        w = v.shape[1]
        lane = lax.broadcasted_iota(jnp.int32, v.shape, 1)
        return jnp.where((lane & 16) == 0, pltpu.roll(v, w - 16, 1), pltpu.roll(v, 16, 1))

    q = pa[:, :ATT_Q]
    k = pa[:, ATT_Q:ATT_Q + ATT_KV]
    cos4 = jnp.concatenate([cos] * (ATT_Q // 128), axis=1)
    sin4 = jnp.concatenate([sin] * (ATT_Q // 128), axis=1)
    q = (q * cos4 + swap(q) * sin4) * (ATT_HEAD_DIM ** -0.5)
    k = k * cos + swap(k) * sin
    aq_ref[...] = q.astype(BF16)
    akv_ref[...] = jnp.concatenate([k, pa[:, ATT_Q + ATT_KV:]], axis=1).astype(BF16)

    wm = w_ref[:, OFF_MQK:OFF_MVO]
    pm = _dot(h, wm)
    pprev = _dot(pre(xp_ref[...]), wm)[7:8]
    pnext = _dot(pre(xn_ref[...]), wm)[0:1]
    r = lax.broadcasted_iota(jnp.int32, (tm, 1), 0)
    grow = t * tm + r
    is_ctx = t < nct
    pos = jnp.where(is_ctx, grow & (lc - 1), (grow - nct * tm) & (l - 1))
    last = jnp.where(is_ctx, lc - 1, l - 1)
    dn = jnp.where(r == 0, pprev, pltpu.roll(pm, 1, 0))
    dn = jnp.where(pos == 0, 0.0, dn)
    up = jnp.where(r == tm - 1, pnext, pltpu.roll(pm, tm - 1, 0))
    up = jnp.where(pos == last, 0.0, up)
    cw = conv_ref[...]
    cv = dn * cw[0:1] + pm * cw[1:2] + up * cw[2:3]
    lane = lax.broadcasted_iota(jnp.int32, (1, 2 * ML_QK), 1)
    mqk_ref[...] = (_silu(cv) * jnp.where(lane < ML_QK, 1.0, ML_DK ** -0.5)).astype(BF16)

    pv = _dot(h, w_ref[:, OFF_MVO:OFF_HG])
    mv_ref[...] = pv[:, :ML_V].astype(BF16)
    mo_ref[...] = pv[:, ML_V:]

    pg = _dot(h, w_ref[:, OFF_GATE:OFF_GATE + GATE_PAD])[:, :N_GATE] + fbrow_ref[...]
    lane16 = lax.broadcasted_iota(jnp.int32, (1, N_GATE), 1)
    gc_ref[...] = jnp.where(lane16 < 2 * ML_HEADS, pg, _log_sigmoid(pg))
    pgr = _dot_nt(wgt_ref[...], h) + fbcol_ref[...]
    row16 = lax.broadcasted_iota(jnp.int32, (N_GATE, 1), 0)
    gr_ref[...] = jnp.where(row16 < 2 * ML_HEADS, pgr, _log_sigmoid(pgr))

    ph = _dot(h, w_ref[:, OFF_HG:OFF_GATE])
    hq_ref[...] = _silu(ph[:, :HG_K]) * (HG_DK ** -0.5)
    z = ph[:, HG_K:3 * HG_K]
    lbv = lbv_ref[...]
    log_lb, log_1m_lb, one_m_lb = lbv[0:1], lbv[1:2], lbv[2:3]
    bv = log_1m_lb + _log_sigmoid(z)
    hlf_ref[...] = jnp.maximum(log_lb, bv) + jnp.log1p(jnp.exp(-jnp.abs(log_lb - bv)))
    hk_ref[...] = one_m_lb * jax.nn.sigmoid(-z)
    hv_ref[...] = ph[:, 3 * HG_K:3 * HG_K + HG_V].astype(BF16)
    hg_ref[...] = ph[:, 3 * HG_K + HG_V:]


def _project(lay, x, mod, gpre, w, wgt, rope, conv, fbrow, fbcol, lbv):
    n, d = x.shape
    tm = lay.tm
    nb8 = n // 8

    def row(c):
        return pl.BlockSpec((tm, c), lambda t: (t, 0))

    out_cols = [(ATT_Q, BF16), (2 * ATT_KV, BF16), (2 * ML_QK, BF16), (ML_V, BF16), (ML_V, F32),
                (N_GATE, F32), None, (HG_K, F32), (2 * HG_K, F32), (2 * HG_K, F32), (HG_V, BF16), (HG_V, F32)]
    out_specs, out_shape = [], []
    for oc in out_cols:
        if oc is None:
            out_specs.append(pl.BlockSpec((N_GATE, tm), lambda t: (0, t)))
            out_shape.append(jax.ShapeDtypeStruct((N_GATE, n), F32))
        else:
            out_specs.append(row(oc[0]))
            out_shape.append(jax.ShapeDtypeStruct((n, oc[0]), oc[1]))
    return pl.pallas_call(
        functools.partial(_proj_kernel, nct=lay.nct, lc=lay.lc, l=lay.l, tm=tm),
        grid=(lay.ntiles,),
        in_specs=[
            row(d),
            pl.BlockSpec((8, d), lambda t: (jnp.maximum(t * (tm // 8) - 1, 0), 0)),
            pl.BlockSpec((8, d), lambda t: (jnp.minimum((t + 1) * (tm // 8), nb8 - 1), 0)),
            pl.BlockSpec((1, N_MOD, d), lambda t: (lay.mod_index(t), 0, 0)),
            _resident((1, d)),
            _resident(w.shape), _resident(wgt.shape),
            row(256),
            _resident(conv.shape), _resident(fbrow.shape), _resident(fbcol.shape), _resident(lbv.shape),
        ],
        out_specs=out_specs,
        out_shape=out_shape,
        compiler_params=_params(("parallel",)),
        name="project",
    )(x, x, x, mod, gpre, w, wgt, rope, conv, fbrow, fbcol, lbv)


def _merge_kernel(x_ref, att_ref, ml_ref, hg_ref, mod_ref, gpre_ref, gpost_ref,
                  wg_ref, wa_ref, wm_ref, wh_ref, wo_ref, o_ref):
    x = x_ref[...]
    d = x.shape[1]
    mod = mod_ref[0]
    shift, scale, gate = mod[3:4], mod[4:5], mod[5:6]
    h = (_rms(x, gpre_ref[...]) * (1.0 + scale) + shift).astype(BF16)
    y = None
    for i, (b_ref, w_ref) in enumerate(((att_ref, wa_ref), (ml_ref, wm_ref), (hg_ref, wh_ref))):
        g = jax.nn.sigmoid(_dot(h, wg_ref[:, i * d:(i + 1) * d]))
        part = g * _dot(b_ref[...], w_ref[...])
        y = part if y is None else y + part
    yy = _dot(y.astype(BF16), wo_ref[...])
    o_ref[...] = x + gate * _rms(yy, gpost_ref[...])


def _merge(lay, x, att, ml, hg, mod, gpre, gpost, wg, wa, wm, wh, wo, latent_only=False):
    d = x.shape[1]
    tm = lay.tm
    tile0 = lay.nct if latent_only else 0
    ntiles = lay.ntiles - tile0

    def row(c, off=0):
        return pl.BlockSpec((tm, c), lambda t: (t + off, 0))

    return pl.pallas_call(
        _merge_kernel,
        grid=(ntiles,),
        in_specs=[
            row(d, tile0), row(ATT_Q), row(ML_V), row(HG_V),
            pl.BlockSpec((1, N_MOD, d), lambda t: (lay.mod_index(t + tile0), 0, 0)),
            _resident((1, d)), _resident((1, d)),
            _resident(wg.shape), _resident(wa.shape), _resident(wm.shape), _resident(wh.shape),
            _resident(wo.shape),
        ],
        out_specs=row(d),
        out_shape=jax.ShapeDtypeStruct((ntiles * tm, d), F32),
        compiler_params=_params(("parallel",)),
        name="merge",
    )(x, att, ml, hg, mod, gpre, gpost, wg, wa, wm, wh, wo)


def _swap_lane_halves(x):
    return pltpu.bitcast(pltpu.roll(pltpu.bitcast(x, jnp.uint32), 64, 1), BF16)


def _attend(q, kv, bias, sink_ref):
    tq = q.shape[0]
    assert ATT_GROUP == 4 and 2 * ATT_HEAD_DIM == 128 and ATT_KV == 128
    k, v = kv[:, :ATT_KV], kv[:, ATT_KV:]
    ks, vs = _swap_lane_halves(k), _swap_lane_halves(v)
    lo_k = lax.broadcasted_iota(jnp.int32, k.shape, 1) < ATT_HEAD_DIM
    lo_q = lax.broadcasted_iota(jnp.int32, (tq, 128), 1) < ATT_HEAD_DIM
    row = lax.broadcasted_iota(jnp.int32, (4 * tq, 1), 0)
    zero = jnp.zeros_like(k)
    zq = jnp.zeros((tq, 128), BF16)
    bias4 = None if bias is None else jnp.concatenate([bias] * 4, axis=0)
    outs = []
    for g in range(ATT_KV_HEADS):
        own, other = (k, ks) if g == 0 else (ks, k)
        k2 = jnp.where(lo_k, own, other)
        vown, voth = (v, vs) if g == 0 else (vs, v)
        va = jnp.where(lo_k, vown, zero)
        vb = jnp.where(lo_k, zero, voth)
        p0 = q[:, 256 * g:256 * g + 128]
        p1 = q[:, 256 * g + 128:256 * g + 256]
        qg = jnp.concatenate([jnp.where(lo_q, p0, zq), jnp.where(lo_q, p1, zq),
                              jnp.where(lo_q, zq, p0), jnp.where(lo_q, zq, p1)], axis=0)
        s = _dot_nt(qg, k2)
        if bias4 is not None:
            nbk = bias4.shape[1]
            s = jnp.concatenate([s[:, :nbk] + bias4, s[:, nbk:]], axis=1)
        h0 = 4 * g
        sk = jnp.where(row < tq, sink_ref[h0],
                       jnp.where(row < 2 * tq, sink_ref[h0 + 2],
                                 jnp.where(row < 3 * tq, sink_ref[h0 + 1], sink_ref[h0 + 3])))
        m = jnp.maximum(jnp.max(s, axis=1, keepdims=True), sk)
        p = jnp.exp(s - m)
        inv = 1.0 / (jnp.sum(p, axis=1, keepdims=True) + jnp.exp(sk - m))
        pb = p.astype(BF16)
        o = _dot(pb[:2 * tq], va) * inv[:2 * tq] + _dot(pb[2 * tq:], vb) * inv[2 * tq:]
        outs += [o[:tq], o[tq:]]
    return jnp.concatenate(outs, axis=1).astype(BF16)


def _attn_lat_kernel(sink_ref, q_ref, kl_ref, km_ref, kr_ref, kc_ref, o_ref, *, nb):
    i = pl.program_id(1)
    kv = jnp.concatenate([kl_ref[...], kr_ref[...], km_ref[...], kc_ref[...]], axis=0)
    r = lax.broadcasted_iota(jnp.int32, (ATT_BLOCK, ATT_BLOCK), 0)
    j = lax.broadcasted_iota(jnp.int32, (ATT_BLOCK, ATT_BLOCK), 1)
    ninf = jnp.float32(-jnp.inf)
    left = jnp.where((j >= r) & (i > 0), 0.0, ninf)
    right = jnp.where((j <= r) & (i < nb - 1), 0.0, ninf)
    bias = jnp.concatenate([left, right], axis=1)
    o_ref[...] = _attend(q_ref[...], kv, bias, sink_ref)


def _attn_ctx_kernel(sink_ref, q_ref, kc_ref, latent_out_ref, o_ref):
    del latent_out_ref
    o_ref[...] = _attend(q_ref[...], kc_ref[...], None, sink_ref)


def _attention(lay, aq, akv, sink, with_ctx):
    n = aq.shape[0]
    nb = lay.l // ATT_BLOCK
    base = lay.nc // ATT_BLOCK
    lc = lay.lc
    smem = pl.BlockSpec(memory_space=pltpu.SMEM)
    kvw = 2 * ATT_KV

    def band(off):
        return pl.BlockSpec((ATT_BLOCK, kvw),
                            lambda b, i: (base + b * nb + jnp.clip(i + off, 0, nb - 1), 0))

    row0 = base if with_ctx else 0
    out = pl.pallas_call(
        functools.partial(_attn_lat_kernel, nb=nb),
        grid=(lay.batch, nb),
        in_specs=[smem,
                  pl.BlockSpec((ATT_BLOCK, ATT_Q), lambda b, i: (base + b * nb + i, 0)),
                  band(-1), band(0), band(1),
                  pl.BlockSpec((lc, kvw), lambda b, i: (b, 0))],
        out_specs=pl.BlockSpec((ATT_BLOCK, ATT_Q), lambda b, i: (row0 + b * nb + i, 0)),
        out_shape=jax.ShapeDtypeStruct(((row0 + lay.batch * nb) * ATT_BLOCK, ATT_Q), BF16),
        compiler_params=_params(("parallel", "parallel")),
        name="attn_latent",
    )(sink, aq, akv, akv, akv, akv)
    if not with_ctx:
        return out
    ncb = lc // ATT_BLOCK
    return pl.pallas_call(
        _attn_ctx_kernel,
        grid=(lay.batch, ncb),
        in_specs=[smem,
                  pl.BlockSpec((ATT_BLOCK, ATT_Q), lambda b, i: (b * ncb + i, 0)),
                  pl.BlockSpec((lc, kvw), lambda b, i: (b, 0)),
                  pl.BlockSpec(memory_space=pl.ANY)],
        out_specs=pl.BlockSpec((ATT_BLOCK, ATT_Q), lambda b, i: (b * ncb + i, 0)),
        out_shape=jax.ShapeDtypeStruct(out.shape, BF16),
        input_output_aliases={3: 0},
        compiler_params=_params(("parallel", "parallel")),
        name="attn_context",
    )(sink, aq, akv, out)


class _Chunks:
    def __init__(self, lay, ch):
        self.ch = ch
        self.nctx = lay.lc // ch
        self.nlat = lay.l // ch
        self.base = lay.nc // ch
        self.steps = self.nctx + self.nlat
        self.total = lay.n // ch

    def fwd(self, b, j):
        return jnp.where(j < self.nctx, b * self.nctx + j, self.base + b * self.nlat + (j - self.nctx))

    def bwd(self, b, j):
        return jnp.where(j < self.nctx, b * self.nctx + (self.nctx - 1 - j),
                         self.base + b * self.nlat + (self.nlat - 1 - (j - self.nctx)))


def _ml_scan_kernel(kf_ref, vf_ref, gf_ref, kb_ref, vb_ref, gb_ref, tri_ref,
                    cf_ref, nf_ref, mf_ref, cb_ref, nb_ref, mb_ref, c_s, n_s, m_s, *, rows, cg):
    @pl.when(pl.program_id(1) == 0)
    def _():
        c_s[...] = jnp.zeros_like(c_s)
        n_s[...] = jnp.zeros_like(n_s)
        m_s[...] = jnp.zeros_like(m_s)

    nsub = rows // cg
    streams = ((kf_ref, vf_ref, gf_ref, cf_ref, nf_ref, mf_ref),
               (kb_ref, vb_ref, gb_ref, cb_ref, nb_ref, mb_ref))
    for d, (k_ref, v_ref, g_ref, c_out, n_out, m_out) in enumerate(streams):
        gc = g_ref[...]
        ig = gc[:, ML_HEADS * d:ML_HEADS * (d + 1)]
        lf = gc[:, 2 * ML_HEADS + ML_HEADS * d:2 * ML_HEADS + ML_HEADS * (d + 1)]
        b = _cumsum3(tri_ref[d], lf)
        a = ig - b
        last = cg - 1 if d == 0 else 0
        gmaxs = [jnp.max(a[c * cg:(c + 1) * cg], axis=0, keepdims=True) for c in range(nsub)]
        u = jnp.exp(a - jnp.concatenate([jnp.broadcast_to(g, (cg, ML_HEADS)) for g in gmaxs], axis=0))
        ku = k_ref[...].astype(F32) * _head_lanes(u, ML_DK)
        kub = ku.astype(BF16)
        v = v_ref[...]
        m = m_s[d]
        ct = c_s[d]
        n = n_s[d]
        for c in (range(nsub) if d == 0 else reversed(range(nsub))):
            rs = slice(c * cg, (c + 1) * cg)
            mx = jnp.maximum(m, gmaxs[c])
            alpha = _head_lanes(jnp.exp(m - mx), ML_DK)
            beta = _head_lanes(jnp.exp(gmaxs[c] - mx), ML_DK)
            m_out[c] = m
            c_out[c] = ct.astype(BF16)
            n_out[c] = n
            upd = [_dot_tn(v[rs, ML_DV * hh:ML_DV * (hh + 1)], kub[rs, ML_DK * hh:ML_DK * (hh + 1)])
                   for hh in range(ML_HEADS)]
            ct = alpha * ct + beta * jnp.concatenate(upd, axis=1)
            n = alpha * n + beta * jnp.sum(ku[rs], axis=0, keepdims=True)
            m = b[c * cg + last:c * cg + last + 1] + mx
        m_s[d] = m
        c_s[d] = ct
        n_s[d] = n


def _ml_out_kernel(mf_ref, mb_ref, qk_ref, v_ref, o_ref, gc_ref, gr_ref, cf_ref, cb_ref, nf_ref, nb_ref,
                   tri_ref, gnt_ref, out_ref, *, c0, rows, cg):
    step = pl.program_id(0) + c0
    nsub = rows // cg
    nch = 2 * ML_HEADS
    qk = qk_ref[...]
    v = v_ref[...]
    gc = gc_ref[...]
    gr = gr_ref[...]
    ninf = jnp.float32(-jnp.inf)
    chain = lax.broadcasted_iota(jnp.int32, (nch, rows), 0)
    lane = lax.broadcasted_iota(jnp.int32, (nch, rows), 1)
    pos = lane & (cg - 1)
    is_fwd = chain < ML_HEADS

    l3 = jnp.concatenate(_split3(gr[nch:]), axis=0)

    def sum3(bb):
        return bb[:nch] + bb[nch:2 * nch] + bb[2 * nch:]

    b_rows = jnp.where(is_fwd, sum3(_dot_nt(l3, tri_ref[0])), sum3(_dot_nt(l3, tri_ref[1])))
    a_rows = gr[:nch] - b_rows
    g_rows = a_rows
    sh = 1
    while sh < cg:
        xf = jnp.where(pos >= sh, pltpu.roll(g_rows, sh, 1), ninf)
        xb = jnp.where(pos < cg - sh, pltpu.roll(g_rows, rows - sh, 1), ninf)
        g_rows = jnp.maximum(g_rows, jnp.where(is_fwd, xf, xb))
        sh *= 2
    m_rows = jnp.zeros((nch, rows), F32)
    for c in range(nsub):
        in_chunk = (lane >= c * cg) & (lane < (c + 1) * cg)
        for ch in range(nch):
            m_ref = mf_ref if ch < ML_HEADS else mb_ref
            m_rows = jnp.where(in_chunk & (chain == ch), m_ref[step * nsub + c, ch % ML_HEADS], m_rows)
    mt = jnp.maximum(g_rows, m_rows)
    rf = jnp.exp(g_rows - mt)
    wp = jnp.exp(m_rows - mt)
    emt = jnp.exp(-b_rows - mt)

    a_cols = []
    for d in range(2):
        lo = nch + ML_HEADS * d
        a_cols.append(gc[:, ML_HEADS * d:ML_HEADS * (d + 1)] - _cumsum3(tri_ref[d], gc[:, lo:lo + ML_HEADS]))

    ss = lax.broadcasted_iota(jnp.int32, (cg, cg), 0)
    tt = lax.broadcasted_iota(jnp.int32, (cg, cg), 1)
    gnt = gnt_ref[...]
    og = o_ref[...]
    for c in range(nsub):
        rs = slice(c * cg, (c + 1) * cg)
        ctf, ctb = cf_ref[c], cb_ref[c]
        nf, nb = nf_ref[c], nb_ref[c]
        outs = []
        for hh in range(ML_HEADS):
            dk = slice(ML_DK * hh, ML_DK * (hh + 1))
            dv = slice(ML_DV * hh, ML_DV * (hh + 1))
            q = qk[rs, dk]
            st = _dot_nt(qk[rs, ML_QK + ML_DK * hh:ML_QK + ML_DK * (hh + 1)], q)
            nn = [t for x in (nf[:, dk], nb[:, dk]) for t in _split3(x)[:2]]
            dn2 = _dot_nt(jnp.concatenate(nn + [jnp.zeros((12, ML_DK), BF16)], axis=0), q)
            ht = None
            for d in range(2):
                ch = d * ML_HEADS + hh
                mask = (ss <= tt) if d == 0 else (ss >= tt)
                e = jnp.exp(jnp.where(mask, a_cols[d][rs, hh:hh + 1] - g_rows[ch:ch + 1, rs], ninf))
                sd = st * e
                den_i = jnp.sum(sd, axis=0, keepdims=True)
                num_t = _dot_tn(v[rs, dv], sd.astype(BF16))
                inter_t = _dot_nt((ctf if d == 0 else ctb)[:, dk], q)
                wpr, rfr = wp[ch:ch + 1, rs], rf[ch:ch + 1, rs]
                den = wpr * (dn2[2 * d:2 * d + 1] + dn2[2 * d + 1:2 * d + 2]) + rfr * den_i
                inv = 1.0 / jnp.maximum(jnp.abs(den), emt[ch:ch + 1, rs])
                part = (wpr * inv) * inter_t + (rfr * inv) * num_t
                ht = part if ht is None else ht + part
            ms = jnp.mean(ht * ht, axis=0, keepdims=True)
            y = (ht * lax.rsqrt(ms + EPS) * gnt[dv]).T
            outs.append(y * jax.nn.sigmoid(og[rs, dv]))
        out_ref[rs, :] = jnp.concatenate(outs, axis=1).astype(BF16)


def _mlstm(lay, mqk, mv, mo, gc, gr, gnorm, with_ctx):
    cg = min(128, lay.lc)
    orows = min(256, lay.lc)
    nsub = orows // cg
    ck = _Chunks(lay, orows)
    tri = _chunk_tri(orows, cg)
    nt = lay.n // cg

    def spec(cols, fn, colblk=0):
        return pl.BlockSpec((orows, cols), lambda b, j: (fn(b, j), colblk))

    def state_specs(fn):
        return [pl.BlockSpec((nsub, ML_DV, ML_QK), lambda b, j: (fn(b, j), 0, 0)),
                pl.BlockSpec((nsub, 1, ML_QK), lambda b, j: (fn(b, j), 0, 0)),
                pl.BlockSpec((nsub, 1, ML_HEADS), lambda b, j: (fn(b, j), 0, 0))]

    state_shapes = [jax.ShapeDtypeStruct((nt, ML_DV, ML_QK), BF16),
                    jax.ShapeDtypeStruct((nt, 1, ML_QK), F32),
                    jax.ShapeDtypeStruct((nt, 1, ML_HEADS), F32)]
    cf, nf, mf, cb, nb, mb = pl.pallas_call(
        functools.partial(_ml_scan_kernel, rows=orows, cg=cg),
        grid=(lay.batch, ck.steps),
        in_specs=[spec(ML_QK, ck.fwd, 1), spec(ML_V, ck.fwd), spec(N_GATE, ck.fwd),
                  spec(ML_QK, ck.bwd, 1), spec(ML_V, ck.bwd), spec(N_GATE, ck.bwd),
                  pl.BlockSpec((2, orows, orows), lambda b, j: (0, 0, 0))],
        out_specs=state_specs(ck.fwd) + state_specs(ck.bwd),
        out_shape=state_shapes + state_shapes,
        scratch_shapes=[pltpu.VMEM((2, ML_DV, ML_QK), F32),
                        pltpu.VMEM((2, 1, ML_QK), F32),
                        pltpu.VMEM((2, 1, ML_HEADS), F32)],
        compiler_params=_params(("parallel", "arbitrary")),
        name="mlstm_scan",
    )(mqk, mv, gc, mqk, mv, gc, tri)

    c0 = 0 if with_ctx else lay.nc // orows
    nout = lay.n // orows - c0
    smem = pl.BlockSpec(memory_space=pltpu.SMEM)

    def rows(cols):
        return pl.BlockSpec((orows, cols), lambda c: (c + c0, 0))

    cst = pl.BlockSpec((nsub, ML_DV, ML_QK), lambda c: (c + c0, 0, 0))
    nst = pl.BlockSpec((nsub, 1, ML_QK), lambda c: (c + c0, 0, 0))
    out = pl.pallas_call(
        functools.partial(_ml_out_kernel, c0=c0, rows=orows, cg=cg),
        grid=(nout,),
        in_specs=[smem, smem, rows(2 * ML_QK), rows(ML_V), rows(ML_V), rows(N_GATE),
                  pl.BlockSpec((N_GATE, orows), lambda c: (0, c + c0)),
                  cst, cst, nst, nst,
                  pl.BlockSpec((2, orows, orows), lambda c: (0, 0, 0)),
                  pl.BlockSpec((ML_V, 1), lambda c: (0, 0))],
        out_specs=pl.BlockSpec((orows, ML_V), lambda c: (c, 0)),
        out_shape=jax.ShapeDtypeStruct((nout * orows, ML_V), BF16),
        compiler_params=_params(("parallel",)),
        name="mlstm_out",
    )(mf.reshape(nt, ML_HEADS), mb.reshape(nt, ML_HEADS), mqk, mv, mo, gc, gr, cf, cb, nf, nb, tri,
      gnorm.reshape(ML_V, 1))
    return out


def _chunk_tri(rows, cg):
    t = np.arange(rows)[:, None]
    u = np.arange(rows)[None, :]
    same = (t // cg) == (u // cg)
    return jnp.asarray(np.stack([same & (u <= t), same & (u >= t)]).astype(np.float32)).astype(BF16)


def _cumsum3(tri, x):
    w = x.shape[1]
    hi = x.astype(BF16)
    r1 = x - hi.astype(F32)
    mid = r1.astype(BF16)
    lo = (r1 - mid.astype(F32)).astype(BF16)
    bb = _dot(tri, jnp.concatenate([hi, mid, lo], axis=1))
    return bb[:, :w] + bb[:, w:2 * w] + bb[:, 2 * w:]


def _level_ref(b, bs, d):
    rows, w = b.shape
    off = bs // 2 - 1 + d
    if bs >= 8:
        pieces = [jnp.broadcast_to(b[s + off:s + off + 1], (bs, w)) for s in range(0, rows, bs)]
        return jnp.concatenate(pieces, axis=0)
    b8 = b.reshape(rows // 8, 8, w)
    u = lax.broadcasted_iota(jnp.int32, (1, 8, 1), 1)
    out = None
    for s in range(0, 8, bs):
        piece = jnp.broadcast_to(b8[:, s + off:s + off + 1, :], b8.shape)
        out = piece if out is None else jnp.where(u >= s, piece, out)
    return out.reshape(rows, w)


def _hg_scan_kernel(kf_ref, lff_ref, vf_ref, kb_ref, lfb_ref, vb_ref, tri_ref, sf_ref, sb_ref, s_s, *, rows, cg):
    @pl.when(pl.program_id(1) == 0)
    def _():
        s_s[...] = jnp.zeros_like(s_s)

    nsub = rows // cg
    streams = ((kf_ref, lff_ref, vf_ref, sf_ref), (kb_ref, lfb_ref, vb_ref, sb_ref))
    for d, (k_ref, lf_ref, v_ref, s_out) in enumerate(streams):
        b = _cumsum3(tri_ref[d], lf_ref[...])
        last = cg - 1 if d == 0 else 0
        bls = [b[c * cg + last:c * cg + last + 1] for c in range(nsub)]
        bl_rows = jnp.concatenate([jnp.broadcast_to(bl, (cg, HG_K)) for bl in bls], axis=0)
        kd = (k_ref[...] * jnp.exp(bl_rows - b)).astype(BF16)
        v = v_ref[...]
        s = s_s[d]
        for c in (range(nsub) if d == 0 else reversed(range(nsub))):
            rs = slice(c * cg, (c + 1) * cg)
            s_out[c] = s.astype(BF16)
            upd = [_dot_tn(v[rs, HG_DV * hh:HG_DV * (hh + 1)], kd[rs, HG_DK * hh:HG_DK * (hh + 1)])
                   for hh in range(HG_HEADS)]
            s = jnp.exp(bls[c]) * s + jnp.concatenate(upd, axis=1)
        s_s[d] = s


def _hg_out_kernel(q_ref, k_ref, lf_ref, v_ref, g_ref, sf_ref, sb_ref, tri_ref, gn_ref, out_ref,
                   *, rows, cg, nlev):
    q = q_ref[...]
    kk = k_ref[...]
    lff = lf_ref[...]
    v = v_ref[...]
    nsub = rows // cg
    qb = q.astype(BF16)
    t_i = lax.broadcasted_iota(jnp.int32, (cg, cg), 0)
    s_i = lax.broadcasted_iota(jnp.int32, (cg, cg), 1)
    xr = t_i ^ s_i
    level = jnp.zeros((cg, cg), jnp.int32)
    for l in range(nlev):
        level = level + (xr >= 2 ** l).astype(jnp.int32)
    amat = [[None] * HG_HEADS for _ in range(nsub)]
    inter = [[None] * HG_HEADS for _ in range(nsub)]
    for d in range(2):
        k = kk[:, HG_K * d:HG_K * (d + 1)]
        b = _cumsum3(tri_ref[d], lff[:, HG_K * d:HG_K * (d + 1)])
        qe = (q * jnp.exp(b)).astype(BF16)
        ops = [(qb, k.astype(BF16))]
        for l in range(1, nlev + 1):
            e = jnp.exp(-jnp.abs(b - _level_ref(b, 2 ** l, d)))
            ops.append(((q * e).astype(BF16), (k * e).astype(BF16)))
        lev_d = jnp.where((t_i > s_i) if d == 0 else (t_i < s_i), level, -1)
        lev_d = jnp.where(t_i == s_i, 0, lev_d)
        s_ref = sf_ref if d == 0 else sb_ref
        for c in range(nsub):
            rs = slice(c * cg, (c + 1) * cg)
            st = s_ref[c]
            for hh in range(HG_HEADS):
                sl = slice(HG_DK * hh, HG_DK * (hh + 1))
                a = jnp.zeros((cg, cg), F32)
                for l in range(nlev + 1):
                    a = jnp.where(lev_d == l, _dot_nt(ops[l][0][rs, sl], ops[l][1][rs, sl]), a)
                it = _dot_nt(qe[rs, sl], st[:, sl])
                amat[c][hh] = a if d == 0 else amat[c][hh] + a
                inter[c][hh] = it if d == 0 else inter[c][hh] + it
    gn = gn_ref[...]
    gate = g_ref[...]
    for c in range(nsub):
        rs = slice(c * cg, (c + 1) * cg)
        outs = []
        for hh in range(HG_HEADS):
            sl = slice(HG_DV * hh, HG_DV * (hh + 1))
            o = inter[c][hh] + _dot(amat[c][hh].astype(BF16), v[rs, sl])
            outs.append(_rms(o, gn[:, sl]) * _silu(gate[rs, sl]))
        out_ref[rs, :] = jnp.concatenate(outs, axis=1).astype(BF16)


def _hgrn2(lay, hq, hk, hlf, hv, hgate, gnorm, with_ctx):
    cg = min(64, lay.lc)
    orows = min(256, lay.lc)
    nsub = orows // cg
    nlev = int(np.log2(cg))
    ck = _Chunks(lay, orows)
    tri = _chunk_tri(orows, cg)
    nt = lay.n // cg

    def spec(cols, fn, colblk=0):
        return pl.BlockSpec((orows, cols), lambda b, j: (fn(b, j), colblk))

    def sspec(fn):
        return pl.BlockSpec((nsub, HG_DV, HG_K), lambda b, j: (fn(b, j), 0, 0))

    sshape = jax.ShapeDtypeStruct((nt, HG_DV, HG_K), BF16)
    sf, sb = pl.pallas_call(
        functools.partial(_hg_scan_kernel, rows=orows, cg=cg),
        grid=(lay.batch, ck.steps),
        in_specs=[spec(HG_K, ck.fwd, 0), spec(HG_K, ck.fwd, 0), spec(HG_V, ck.fwd),
                  spec(HG_K, ck.bwd, 1), spec(HG_K, ck.bwd, 1), spec(HG_V, ck.bwd),
                  pl.BlockSpec((2, orows, orows), lambda b, j: (0, 0, 0))],
        out_specs=[sspec(ck.fwd), sspec(ck.bwd)],
        out_shape=[sshape, sshape],
        scratch_shapes=[pltpu.VMEM((2, HG_DV, HG_K), F32)],
        compiler_params=_params(("parallel", "arbitrary")),
        name="hgrn2_scan",
    )(hk, hlf, hv, hk, hlf, hv, tri)

    c0 = 0 if with_ctx else lay.nc // orows
    nout = lay.n // orows - c0

    def rows(cols):
        return pl.BlockSpec((orows, cols), lambda c: (c + c0, 0))

    sst = pl.BlockSpec((nsub, HG_DV, HG_K), lambda c: (c + c0, 0, 0))
    out = pl.pallas_call(
        functools.partial(_hg_out_kernel, rows=orows, cg=cg, nlev=nlev),
        grid=(nout,),
        in_specs=[rows(HG_K), rows(2 * HG_K), rows(2 * HG_K), rows(HG_V), rows(HG_V), sst, sst,
                  pl.BlockSpec((2, orows, orows), lambda c: (0, 0, 0)),
                  pl.BlockSpec((1, HG_V), lambda c: (0, 0))],
        out_specs=pl.BlockSpec((orows, HG_V), lambda c: (c, 0)),
        out_shape=jax.ShapeDtypeStruct((nout * orows, HG_V), BF16),
        compiler_params=_params(("parallel",)),
        name="hgrn2_out",
    )(hq, hk, hlf, hv, hgate, sf, sb, tri, gnorm)
    return out


def _rope_table(lay):
    l = lay.l
    rows = l // GRID_W
    row = jnp.repeat(jnp.arange(rows, dtype=F32), GRID_W)
    col = jnp.tile(jnp.arange(GRID_W, dtype=F32), rows)
    n_freq = ATT_HEAD_DIM // 4
    inv = ROPE_BASE ** (-jnp.arange(n_freq, dtype=F32) / n_freq)
    ar, ac = row[:, None] * inv, col[:, None] * inv
    cos = jnp.concatenate([jnp.cos(ar), jnp.cos(ar), jnp.cos(ac), jnp.cos(ac)], axis=1)
    sin = jnp.concatenate([-jnp.sin(ar), jnp.sin(ar), -jnp.sin(ac), jnp.sin(ac)], axis=1)
    lat = jnp.concatenate([cos, cos, sin, sin], axis=1)
    lat = jnp.tile(lat, (lay.batch, 1))
    ctx = jnp.concatenate([jnp.ones((lay.nc, 128), F32), jnp.zeros((lay.nc, 128), F32)], axis=1)
    return jnp.concatenate([ctx, lat], axis=0)


def kernel(x, c, ctx, c_ctx, w_ada, b_ada, norm_pre, norm_post, ffn_w1, ffn_w3, ffn_w2, w_in, att_sink,
           ml_conv, ml_f_bias, ml_norm, hg_lb_logits, hg_norm, w_branch_att, w_branch_ml, w_branch_hg,
           w_out):
    batch, l, d = x.shape
    lc = ctx.shape[1]
    depth = w_ada.shape[0]
    assert l % GRID_W == 0 and l & (l - 1) == 0 and lc & (lc - 1) == 0
    assert l % 256 == 0 and lc % 128 == 0 and batch + 1 <= MOD_ROWS
    lay = _Layout(batch, lc, l)
    lay_ffn = _Layout(batch, lc, l, FFN_TM)

    xs = (ctx.reshape(batch * lc, d), x.reshape(batch * l, d))
    cs = jnp.concatenate([c, c_ctx[None, :], jnp.zeros((MOD_ROWS - batch - 1, d), F32)], axis=0)
    mod_all = _modulation(cs, w_ada, b_ada).reshape(depth, MOD_ROWS, N_MOD, d)
    rope = _rope_table(lay)

    lb_all = jnp.cumsum(jax.nn.softmax(hg_lb_logits.astype(F32), axis=0), axis=0)
    lb_all = lb_all - lb_all[0:1]

    sizes = (ATT_Q, ATT_KV, ATT_KV, ML_QK, ML_QK, ML_V, ML_V, 2 * ML_HEADS, 2 * ML_HEADS,
             HG_K, HG_K, HG_K, HG_V, HG_V, 3 * d)
    offs = np.concatenate([[0], np.cumsum(sizes)])
    g0, g1 = int(offs[7]), int(offs[9])

    for layer in range(depth):
        last = layer == depth - 1
        mod = mod_all[layer]
        wl = w_in[layer]
        w_gate = wl[:, g0:g1]
        w_proj = jnp.concatenate(
            [wl[:, :g0], wl[:, g1:int(offs[14])], w_gate, jnp.zeros((d, GATE_PAD - N_GATE), F32)],
            axis=1).astype(BF16)
        w_gate_t = w_gate.T.astype(BF16)
        w_brg = wl[:, int(offs[14]):].astype(BF16)
        fb = ml_f_bias[layer].reshape(1, 2 * ML_HEADS)
        fbrow = jnp.concatenate([jnp.zeros((1, 2 * ML_HEADS), F32), fb], axis=1)
        fbcol = fbrow.reshape(N_GATE, 1)
        lb = lb_all[layer]
        lbv = jnp.stack([jnp.tile(jnp.log(lb), 2), jnp.tile(jnp.log1p(-lb), 2), jnp.tile(1.0 - lb, 2)])

        def npre(i):
            return norm_pre[layer, i].reshape(1, d)

        def npost(i):
            return norm_post[layer, i].reshape(1, d)

        def ffn(xin, j, i, latent_only=False):
            return _ffn(lay_ffn, xin, mod, j, npre(j), npost(j), ffn_w1[layer, i].astype(BF16),
                        ffn_w3[layer, i].astype(BF16), ffn_w2[layer, i].astype(BF16), latent_only)

        xs = ffn(xs, 0, 0)
        (aq, akv, mqk, mv, mo, gc, gr, hq, hk, hlf, hv, hgate) = _project(
            lay, xs, mod, npre(1), w_proj, w_gate_t, rope, ml_conv[layer], fbrow, fbcol, lbv)
        att = _attention(lay, aq, akv, att_sink[layer], not last)
        ml = _mlstm(lay, mqk, mv, mo, gc, gr, ml_norm[layer].reshape(1, ML_V), not last)
        hg = _hgrn2(lay, hq, hk, hlf, hv, hgate, hg_norm[layer].reshape(1, HG_V), not last)
        xs = _merge(lay, xs, att, ml, hg, mod, npre(1), npost(1), w_brg,
                    w_branch_att[layer].astype(BF16), w_branch_ml[layer].astype(BF16),
                    w_branch_hg[layer].astype(BF16), w_out[layer].astype(BF16), last)
        xs = ffn(xs, 2, 1, last)
    return xs.reshape(batch, l, d)
```

```python
import functools

import numpy as np
import jax
import jax.numpy as jnp
from jax import lax
from jax.experimental import pallas as pl
from jax.experimental.pallas import tpu as pltpu

F32 = jnp.float32
BF16 = jnp.bfloat16

ATT_HEADS = 8
ATT_KV_HEADS = 2
ATT_GROUP = ATT_HEADS // ATT_KV_HEADS
ATT_HEAD_DIM = 64
ATT_BLOCK = 128
ML_HEADS = 4
ML_DK = 64
ML_DV = 128
HG_HEADS = 4
HG_DK = 64
HG_DV = 128
GRID_W = 64
ROPE_BASE = 10000.0
EPS = 1e-6
N_MOD = 9
MOD_ROWS = 16

ATT_Q = ATT_HEADS * ATT_HEAD_DIM
ATT_KV = ATT_KV_HEADS * ATT_HEAD_DIM
ML_QK = ML_HEADS * ML_DK
ML_V = ML_HEADS * ML_DV
HG_K = HG_HEADS * HG_DK
HG_V = HG_HEADS * HG_DV
N_GATE = 4 * ML_HEADS
GATE_PAD = 128

OFF_ATT = 0
OFF_MQK = OFF_ATT + ATT_Q + 2 * ATT_KV
OFF_MVO = OFF_MQK + 2 * ML_QK
OFF_HG = OFF_MVO + 2 * ML_V
OFF_GATE = OFF_HG + 3 * HG_K + 2 * HG_V
W_PROJ_COLS = OFF_GATE + GATE_PAD

LOG2E = 1.4426950408889634
VMEM_LIMIT = 56 * 1024 * 1024
FFN_TM = 1024
FFN_SPLIT = 11


def _dot(a, b):
    return jnp.dot(a, b, preferred_element_type=F32)


def _dot_nt(a, b):
    return lax.dot_general(a, b, (((1,), (1,)), ((), ())), preferred_element_type=F32)


def _dot_tn(a, b):
    return lax.dot_general(a, b, (((0,), (0,)), ((), ())), preferred_element_type=F32)


def _rms(x, g):
    ms = jnp.mean(x * x, axis=-1, keepdims=True)
    return x * lax.rsqrt(ms + EPS) * g


def _silu(x):
    return x * jax.nn.sigmoid(x)


def _log_sigmoid(x):
    return jnp.minimum(x, 0.0) - jnp.log(1.0 + jnp.exp(-jnp.abs(x)))


def _head_lanes(x, w):
    r, h = x.shape
    return jnp.concatenate([jnp.broadcast_to(x[:, i:i + 1], (r, w)) for i in range(h)], axis=1)


def _split3(x):
    hi = x.astype(BF16)
    r1 = x - hi.astype(F32)
    mid = r1.astype(BF16)
    return hi, mid, (r1 - mid.astype(F32)).astype(BF16)


def _head_outer(vt, k, heads):
    dv, dk = vt.shape[0] // heads, k.shape[1] // heads
    big = _dot(vt, k)
    lane = lax.broadcasted_iota(jnp.int32, (dv, heads * dk), 1)
    out = big[:dv]
    for h in range(1, heads):
        out = jnp.where(lane >= h * dk, big[h * dv:(h + 1) * dv], out)
    return out


def _head_lanes_mxu(x, w):
    h = x.shape[1]
    row = lax.broadcasted_iota(jnp.int32, (3 * h, h * w), 0)
    lane = lax.broadcasted_iota(jnp.int32, (3 * h, h * w), 1)
    sel = jnp.zeros((3 * h, h * w), jnp.bool_)
    for t in range(3):
        for i in range(h):
            sel = sel | ((row == t * h + i) & (lane >= i * w) & (lane < (i + 1) * w))
    return _dot(jnp.concatenate(_split3(x), axis=1), jnp.where(sel, 1.0, 0.0).astype(BF16))


def _params(sem):
    return pltpu.CompilerParams(dimension_semantics=sem, vmem_limit_bytes=VMEM_LIMIT)


def _resident(shape):
    nd = len(shape)
    return pl.BlockSpec(shape, lambda *_: (0,) * nd, pipeline_mode=pl.Buffered(1))


def _mod_kernel(c_ref, w_ref, b_ref, o_ref):
    s = _silu(c_ref[...]).astype(BF16)
    o_ref[0] = _dot(s, w_ref[0].astype(BF16)) + b_ref[0]


def _modulation(cs, w_ada, b_ada):
    depth, d, nm = w_ada.shape
    tn = 1024
    return pl.pallas_call(
        _mod_kernel,
        grid=(depth, nm // tn),
        in_specs=[
            pl.BlockSpec((MOD_ROWS, d), lambda l, n: (0, 0)),
            pl.BlockSpec((1, d, tn), lambda l, n: (l, 0, n)),
            pl.BlockSpec((1, 1, tn), lambda l, n: (l, 0, n)),
        ],
        out_specs=pl.BlockSpec((1, MOD_ROWS, tn), lambda l, n: (l, 0, n)),
        out_shape=jax.ShapeDtypeStruct((depth, MOD_ROWS, nm), F32),
        compiler_params=_params(("parallel", "parallel")),
        name="modulation",
    )(cs, w_ada, b_ada.reshape(depth, 1, nm))


class _Layout:
    def __init__(self, batch, lc, l, tm=512):
        self.batch, self.lc, self.l = batch, lc, l
        self.nc = batch * lc
        self.n = self.nc + batch * l
        while self.nc % tm or l % tm:
            tm //= 2
        self.tm = tm
        self.nct = self.nc // tm
        self.tpb = l // tm
        self.ntiles = self.n // tm

    def mod_index(self, t):
        return jnp.where(t < self.nct, self.batch, (t - self.nct) // self.tpb)


def _ffn_split_kernel(c_ref, x_ref, *rest, nct, **kw):
    _ffn_body(jnp.where(pl.program_id(0) < nct, c_ref[...], x_ref[...]), *rest, **kw)


def _ffn_kernel(x_ref, *rest, **kw):
    _ffn_body(x_ref[...], *rest, **kw)


def _ffn_body(x, mod_ref, gpre_ref, gpost_ref, w1_ref, w3_ref, w2_ref, o_ref, *, j, halves):
    mod = mod_ref[0]
    shift, scale, gate = mod[3 * j:3 * j + 1], mod[3 * j + 1:3 * j + 2], mod[3 * j + 2:3 * j + 3]
    h = (_rms(x, gpre_ref[...]) * (1.0 + scale) + shift).astype(BF16)
    dff = w1_ref.shape[1]
    step = dff // halves
    y = None
    for c in range(halves):
        sl = slice(c * step, (c + 1) * step)
        a = _dot(h, w1_ref[:, sl])
        b = _dot(h, w3_ref[:, sl])
        part = _dot((_silu(a) * b).astype(BF16), w2_ref[sl, :])
        y = part if y is None else y + part
    o_ref[...] = x + 0.5 * gate * _rms(y, gpost_ref[...])


def _ffn(lay, xin, mod, j, gpre, gpost, w1, w3, w2, latent_only=False):
    d, dff = w1.shape
    tm = lay.tm
    halves = FFN_SPLIT if (dff // FFN_SPLIT) % 128 == 0 else 1
    tile0 = lay.nct if latent_only else 0
    ntiles = lay.ntiles - tile0
    row = pl.BlockSpec((tm, d), lambda t: (t, 0))
    if isinstance(xin, tuple):
        body = functools.partial(_ffn_split_kernel, nct=lay.nct, j=j, halves=halves)
        xs_specs = [pl.BlockSpec((tm, d), lambda t: (jnp.minimum(t, lay.nct - 1), 0)),
                    pl.BlockSpec((tm, d), lambda t: (jnp.maximum(t - lay.nct, 0), 0))]
    else:
        body = functools.partial(_ffn_kernel, j=j, halves=halves)
        xs_specs, xin = [row], (xin,)
    return pl.pallas_call(
        body,
        grid=(ntiles,),
        in_specs=xs_specs + [
            pl.BlockSpec((1, N_MOD, d), lambda t: (lay.mod_index(t + tile0), 0, 0)),
            _resident((1, d)), _resident((1, d)),
            _resident((d, dff)), _resident((d, dff)), _resident((dff, d)),
        ],
        out_specs=row,
        out_shape=jax.ShapeDtypeStruct((ntiles * tm, d), F32),
        compiler_params=_params(("parallel",)),
        name="ffn",
    )(*xin, mod, gpre, gpost, w1, w3, w2)


def _proj_kernel(x_ref, xp_ref, xn_ref, mod_ref, gpre_ref, w_ref, wgt_ref, rope_ref, conv_ref,
                 fbrow_ref, fbcol_ref, lbv_ref,
                 aq_ref, akv_ref, mqk_ref, mo_ref, gc_ref, gr_ref,
                 hq_ref, hk_ref, hlf_ref, hv_ref, hg_ref, mvt_ref, hvt_ref, *, nct, lc, l, tm):
    t = pl.program_id(0)
    mod = mod_ref[0]
    shift, scale = mod[3:4], mod[4:5]
    gpre = gpre_ref[...]

    def pre(xx):
        return (_rms(xx, gpre) * (1.0 + scale) + shift).astype(BF16)

    h = pre(x_ref[...])

    pa = _dot(h, w_ref[:, OFF_ATT:OFF_MQK])
    rope = rope_ref[...]
    cos, sin = rope[:, :128], rope[:, 128:]

    def swap(v):
        w = v.shape[1]
        lane = lax.broadcasted_iota(jnp.int32, v.shape, 1)
        return jnp.where((lane & 16) == 0, pltpu.roll(v, w - 16, 1), pltpu.roll(v, 16, 1))

    q = pa[:, :ATT_Q]
    k = pa[:, ATT_Q:ATT_Q + ATT_KV]
    cos4 = jnp.concatenate([cos] * (ATT_Q // 128), axis=1)
    sin4 = jnp.concatenate([sin] * (ATT_Q // 128), axis=1)
    q = (q * cos4 + swap(q) * sin4) * (ATT_HEAD_DIM ** -0.5 * LOG2E)
    k = k * cos + swap(k) * sin
    aq_ref[...] = q.astype(BF16)
    akv_ref[...] = jnp.concatenate([k, pa[:, ATT_Q + ATT_KV:]], axis=1).astype(BF16)

    wm = w_ref[:, OFF_MQK:OFF_MVO]
    pm = _dot(h, wm)
    pprev = _dot(pre(xp_ref[...]), wm)[7:8]
    pnext = _dot(pre(xn_ref[...]), wm)[0:1]
    r = lax.broadcasted_iota(jnp.int32, (tm, 1), 0)
    grow = t * tm + r
    is_ctx = t < nct
    pos = jnp.where(is_ctx, grow & (lc - 1), (grow - nct * tm) & (l - 1))
    last = jnp.where(is_ctx, lc - 1, l - 1)
    dn = jnp.where(r == 0, pprev, pltpu.roll(pm, 1, 0))
    dn = jnp.where(pos == 0, 0.0, dn)
    up = jnp.where(r == tm - 1, pnext, pltpu.roll(pm, tm - 1, 0))
    up = jnp.where(pos == last, 0.0, up)
    cw = conv_ref[...]
    cv = dn * cw[0:1] + pm * cw[1:2] + up * cw[2:3]
    lane = lax.broadcasted_iota(jnp.int32, (1, 2 * ML_QK), 1)
    mqk_ref[...] = (_silu(cv) * jnp.where(lane < ML_QK, 1.0, ML_DK ** -0.5)).astype(BF16)

    mo_ref[...] = _dot(h, w_ref[:, OFF_MVO + ML_V:OFF_HG])

    pg = _dot(h, w_ref[:, OFF_GATE:OFF_GATE + GATE_PAD])[:, :N_GATE] + fbrow_ref[...]
    lane16 = lax.broadcasted_iota(jnp.int32, (1, N_GATE), 1)
    gc_ref[...] = jnp.where(lane16 < 2 * ML_HEADS, pg, _log_sigmoid(pg)) * LOG2E
    pt = _dot_nt(wgt_ref[...], h)
    mvt_ref[...] = pt[:ML_V].astype(BF16)
    hvt_ref[...] = pt[ML_V:ML_V + HG_V].astype(BF16)
    pgr = pt[ML_V + HG_V:] + fbcol_ref[...]
    row16 = lax.broadcasted_iota(jnp.int32, (N_GATE, 1), 0)
    gr_ref[...] = jnp.where(row16 < 2 * ML_HEADS, pgr, _log_sigmoid(pgr)) * LOG2E

    ph = _dot(h, w_ref[:, OFF_HG:OFF_GATE])
    hq_ref[...] = _silu(ph[:, :HG_K]) * (HG_DK ** -0.5)
    z = ph[:, HG_K:3 * HG_K]
    lbv = lbv_ref[...]
    log_lb, log_1m_lb, one_m_lb = lbv[0:1], lbv[1:2], lbv[2:3]
    ls = _log_sigmoid(z)
    bv = log_1m_lb + ls
    hlf_ref[...] = (jnp.maximum(log_lb, bv) + jnp.log(1.0 + jnp.exp(-jnp.abs(log_lb - bv)))) * LOG2E
    hk_ref[...] = one_m_lb * jnp.exp(ls - z)
    hv_ref[...] = ph[:, 3 * HG_K:3 * HG_K + HG_V].astype(BF16)
    hg_ref[...] = ph[:, 3 * HG_K + HG_V:]


def _project(lay, x, mod, gpre, w, wgt, rope, conv, fbrow, fbcol, lbv):
    n, d = x.shape
    tm = lay.tm
    nb8 = n // 8

    def row(c):
        return pl.BlockSpec((tm, c), lambda t: (t, 0))

    out_cols = [(ATT_Q, BF16), (2 * ATT_KV, BF16), (2 * ML_QK, BF16), (ML_V, F32),
                (N_GATE, F32), (N_GATE, F32, None), (HG_K, F32), (2 * HG_K, F32), (2 * HG_K, F32),
                (HG_V, BF16), (HG_V, F32), (ML_V, BF16, None), (HG_V, BF16, None)]
    out_specs, out_shape = [], []
    for oc in out_cols:
        if len(oc) == 3:
            out_specs.append(pl.BlockSpec((oc[0], tm), lambda t: (0, t)))
            out_shape.append(jax.ShapeDtypeStruct((oc[0], n), oc[1]))
        else:
            out_specs.append(row(oc[0]))
            out_shape.append(jax.ShapeDtypeStruct((n, oc[0]), oc[1]))
    return pl.pallas_call(
        functools.partial(_proj_kernel, nct=lay.nct, lc=lay.lc, l=lay.l, tm=tm),
        grid=(lay.ntiles,),
        in_specs=[
            row(d),
            pl.BlockSpec((8, d), lambda t: (jnp.maximum(t * (tm // 8) - 1, 0), 0)),
            pl.BlockSpec((8, d), lambda t: (jnp.minimum((t + 1) * (tm // 8), nb8 - 1), 0)),
            pl.BlockSpec((1, N_MOD, d), lambda t: (lay.mod_index(t), 0, 0)),
            _resident((1, d)),
            _resident(w.shape), _resident(wgt.shape),
            row(256),
            _resident(conv.shape), _resident(fbrow.shape), _resident(fbcol.shape), _resident(lbv.shape),
        ],
        out_specs=out_specs,
        out_shape=out_shape,
        compiler_params=_params(("parallel",)),
        name="project",
    )(x, x, x, mod, gpre, w, wgt, rope, conv, fbrow, fbcol, lbv)


def _merge_kernel(x_ref, att_ref, ml_ref, hg_ref, mod_ref, gpre_ref, gpost_ref,
                  wg_ref, wa_ref, wm_ref, wh_ref, wo_ref, o_ref):
    x = x_ref[...]
    d = x.shape[1]
    mod = mod_ref[0]
    shift, scale, gate = mod[3:4], mod[4:5], mod[5:6]
    h = (_rms(x, gpre_ref[...]) * (1.0 + scale) + shift).astype(BF16)
    y = None
    for i, (b_ref, w_ref) in enumerate(((att_ref, wa_ref), (ml_ref, wm_ref), (hg_ref, wh_ref))):
        g = jax.nn.sigmoid(_dot(h, wg_ref[:, i * d:(i + 1) * d]))
        part = g * _dot(b_ref[...], w_ref[...])
        y = part if y is None else y + part
    yy = _dot(y.astype(BF16), wo_ref[...])
    o_ref[...] = x + gate * _rms(yy, gpost_ref[...])


def _merge(lay, x, att, ml, hg, mod, gpre, gpost, wg, wa, wm, wh, wo, latent_only=False):
    d = x.shape[1]
    tm = lay.tm
    tile0 = lay.nct if latent_only else 0
    ntiles = lay.ntiles - tile0

    def row(c, off=0):
        return pl.BlockSpec((tm, c), lambda t: (t + off, 0))

    return pl.pallas_call(
        _merge_kernel,
        grid=(ntiles,),
        in_specs=[
            row(d, tile0), row(ATT_Q), row(ML_V), row(HG_V),
            pl.BlockSpec((1, N_MOD, d), lambda t: (lay.mod_index(t + tile0), 0, 0)),
            _resident((1, d)), _resident((1, d)),
            _resident(wg.shape), _resident(wa.shape), _resident(wm.shape), _resident(wh.shape),
            _resident(wo.shape),
        ],
        out_specs=row(d),
        out_shape=jax.ShapeDtypeStruct((ntiles * tm, d), F32),
        compiler_params=_params(("parallel",)),
        name="merge",
    )(x, att, ml, hg, mod, gpre, gpost, wg, wa, wm, wh, wo)


def _swap_lane_halves(x):
    return pltpu.bitcast(pltpu.roll(pltpu.bitcast(x, jnp.uint32), 64, 1), BF16)


def _attend(q, kv, bias, sink_ref):
    tq = q.shape[0]
    assert ATT_GROUP == 4 and 2 * ATT_HEAD_DIM == 128 and ATT_KV == 128
    k, v = kv[:, :ATT_KV], kv[:, ATT_KV:]
    ks, vs = _swap_lane_halves(k), _swap_lane_halves(v)
    lo_k = lax.broadcasted_iota(jnp.int32, k.shape, 1) < ATT_HEAD_DIM
    lo_q = lax.broadcasted_iota(jnp.int32, (tq, 128), 1) < ATT_HEAD_DIM
    row = lax.broadcasted_iota(jnp.int32, (4 * tq, 1), 0)
    zero = jnp.zeros_like(k)
    zq = jnp.zeros((tq, 128), BF16)
    bias4 = None if bias is None else jnp.concatenate([bias] * 4, axis=0)
    outs = []
    for g in range(ATT_KV_HEADS):
        own, other = (k, ks) if g == 0 else (ks, k)
        k2 = jnp.where(lo_k, own, other)
        vown, voth = (v, vs) if g == 0 else (vs, v)
        va = jnp.where(lo_k, vown, zero)
        vb = jnp.where(lo_k, zero, voth)
        p0 = q[:, 256 * g:256 * g + 128]
        p1 = q[:, 256 * g + 128:256 * g + 256]
        qg = jnp.concatenate([jnp.where(lo_q, p0, zq), jnp.where(lo_q, p1, zq),
                              jnp.where(lo_q, zq, p0), jnp.where(lo_q, zq, p1)], axis=0)
        s = _dot_nt(qg, k2)
        if bias4 is not None:
            nbk = bias4.shape[1]
            s = jnp.concatenate([s[:, :nbk] + bias4, s[:, nbk:]], axis=1)
        h0 = 4 * g
        sk = jnp.where(row < tq, sink_ref[h0],
                       jnp.where(row < 2 * tq, sink_ref[h0 + 2],
                                 jnp.where(row < 3 * tq, sink_ref[h0 + 1], sink_ref[h0 + 3]))) * LOG2E
        m = jnp.maximum(jnp.max(s, axis=1, keepdims=True), sk)
        p = jnp.exp2(s - m)
        inv = 1.0 / (jnp.sum(p, axis=1, keepdims=True) + jnp.exp2(sk - m))
        pb = p.astype(BF16)
        o = _dot(pb[:2 * tq], va) * inv[:2 * tq] + _dot(pb[2 * tq:], vb) * inv[2 * tq:]
        outs += [o[:tq], o[tq:]]
    return jnp.concatenate(outs, axis=1).astype(BF16)


def _attn_kernel(sink_ref, q_ref, kl_ref, km_ref, kr_ref, kc_ref, o_ref, *, nb):
    i = pl.program_id(1)

    @pl.when(i < nb)
    def _():
        kv = jnp.concatenate([kl_ref[...], kr_ref[...], km_ref[...], kc_ref[...]], axis=0)
        r = lax.broadcasted_iota(jnp.int32, (ATT_BLOCK, ATT_BLOCK), 0)
        j = lax.broadcasted_iota(jnp.int32, (ATT_BLOCK, ATT_BLOCK), 1)
        ninf = jnp.float32(-jnp.inf)
        left = jnp.where((j >= r) & (i > 0), 0.0, ninf)
        right = jnp.where((j <= r) & (i < nb - 1), 0.0, ninf)
        o_ref[...] = _attend(q_ref[...], kv, jnp.concatenate([left, right], axis=1), sink_ref)

    @pl.when(i >= nb)
    def _():
        o_ref[...] = _attend(q_ref[...], kc_ref[...], None, sink_ref)


def _attention(lay, aq, akv, sink, with_ctx):
    nb = lay.l // ATT_BLOCK
    base = lay.nc // ATT_BLOCK
    lc = lay.lc
    ncb = lc // ATT_BLOCK if with_ctx else 0
    row0 = base if with_ctx else 0
    smem = pl.BlockSpec(memory_space=pltpu.SMEM)
    kvw = 2 * ATT_KV

    def band(off):
        return pl.BlockSpec((ATT_BLOCK, kvw),
                            lambda b, i: (base + b * nb + jnp.clip(i + off, 0, nb - 1), 0))

    def qrow(b, i):
        return jnp.where(i < nb, base + b * nb + i, b * ncb + (i - nb))

    def orow(b, i):
        return jnp.where(i < nb, row0 + b * nb + i, b * ncb + (i - nb))

    return pl.pallas_call(
        functools.partial(_attn_kernel, nb=nb),
        grid=(lay.batch, nb + ncb),
        in_specs=[smem,
                  pl.BlockSpec((ATT_BLOCK, ATT_Q), lambda b, i: (qrow(b, i), 0)),
                  band(-1), band(0), band(1),
                  pl.BlockSpec((lc, kvw), lambda b, i: (b, 0))],
        out_specs=pl.BlockSpec((ATT_BLOCK, ATT_Q), lambda b, i: (orow(b, i), 0)),
        out_shape=jax.ShapeDtypeStruct(((row0 + lay.batch * nb) * ATT_BLOCK, ATT_Q), BF16),
        compiler_params=_params(("parallel", "parallel")),
        name="attention",
    )(sink, aq, akv, akv, akv, akv)


class _Chunks:
    def __init__(self, lay, ch):
        self.ch = ch
        self.nctx = lay.lc // ch
        self.nlat = lay.l // ch
        self.base = lay.nc // ch
        self.steps = self.nctx + self.nlat
        self.total = lay.n // ch

    def fwd(self, b, j):
        return jnp.where(j < self.nctx, b * self.nctx + j, self.base + b * self.nlat + (j - self.nctx))

    def bwd(self, b, j):
        return jnp.where(j < self.nctx, b * self.nctx + (self.nctx - 1 - j),
                         self.base + b * self.nlat + (self.nlat - 1 - (j - self.nctx)))


def _ml_scan_body(kf_ref, vf_ref, gf_ref, kb_ref, vb_ref, gb_ref, tri_ref,
                  cf_ref, nf_ref, mf_ref, cb_ref, nb_ref, mb_ref, c_s, n_s, m_s, *, rows, cg):
    nsub = rows // cg
    streams = ((kf_ref, vf_ref, gf_ref, cf_ref, nf_ref, mf_ref),
               (kb_ref, vb_ref, gb_ref, cb_ref, nb_ref, mb_ref))
    for d, (k_ref, v_ref, g_ref, c_out, n_out, m_out) in enumerate(streams):
        gc = g_ref[...]
        ig = gc[:, ML_HEADS * d:ML_HEADS * (d + 1)]
        lf = gc[:, 2 * ML_HEADS + ML_HEADS * d:2 * ML_HEADS + ML_HEADS * (d + 1)]
        b = _cumsum3(tri_ref[d], lf)
        a = ig - b
        last = cg - 1 if d == 0 else 0
        gmaxs = [jnp.max(a[c * cg:(c + 1) * cg], axis=0, keepdims=True) for c in range(nsub)]
        rel = a - jnp.concatenate([jnp.broadcast_to(g, (cg, ML_HEADS)) for g in gmaxs], axis=0)
        ku = k_ref[...].astype(F32) * jnp.exp2(_head_lanes_mxu(rel, ML_DK))
        kub = ku.astype(BF16)
        vt = v_ref[...]
        m = m_s[d]
        ct = c_s[d]
        n = n_s[d]
        for c in (range(nsub) if d == 0 else reversed(range(nsub))):
            rs = slice(c * cg, (c + 1) * cg)
            mx = jnp.maximum(m, gmaxs[c])
            alpha = _head_lanes(jnp.exp2(m - mx), ML_DK)
            beta = _head_lanes(jnp.exp2(gmaxs[c] - mx), ML_DK)
            m_out[c] = m
            c_out[c] = ct.astype(BF16)
            n_out[c] = n
            ct = alpha * ct + beta * _head_outer(vt[:, rs], kub[rs], ML_HEADS)
            n = alpha * n + beta * jnp.sum(ku[rs], axis=0, keepdims=True)
            m = b[c * cg + last:c * cg + last + 1] + mx
        m_s[d] = m
        c_s[d] = ct
        n_s[d] = n


def _ml_out_kernel(mf_ref, mb_ref, qk_ref, vt_ref, o_ref, gc_ref, gr_ref, cf_ref, cb_ref, nf_ref, nb_ref,
                   tri_ref, gnt_ref, out_ref, *, c0, rows, cg):
    step = pl.program_id(0) + c0
    nsub = rows // cg
    nch = 2 * ML_HEADS
    qk = qk_ref[...]
    vt = vt_ref[...]
    gc = gc_ref[...]
    gr = gr_ref[...]
    ninf = jnp.float32(-jnp.inf)
    chain =lax.broadcasted_iota(jnp.int32, (nch, rows), 0)
    lane = lax.broadcasted_iota(jnp.int32, (nch, rows), 1)
    pos = lane & (cg - 1)
    is_fwd = chain < ML_HEADS

    l3 = jnp.concatenate(_split3(gr[nch:]), axis=0)

    def sum3(bb):
        return bb[:nch] + bb[nch:2 * nch] + bb[2 * nch:]

    b_rows = jnp.where(is_fwd, sum3(_dot_nt(l3, tri_ref[0])), sum3(_dot_nt(l3, tri_ref[1])))
    a_rows = gr[:nch] - b_rows
    g_rows = a_rows
    sh = 1
    while sh < cg:
        xf = jnp.where(pos >= sh, pltpu.roll(g_rows, sh, 1), ninf)
        xb = jnp.where(pos < cg - sh, pltpu.roll(g_rows, rows - sh, 1), ninf)
        g_rows = jnp.maximum(g_rows, jnp.where(is_fwd, xf, xb))
        sh *= 2
    m_rows = jnp.zeros((nch, rows), F32)
    for c in range(nsub):
        in_chunk = (lane >= c * cg) & (lane < (c + 1) * cg)
        for ch in range(nch):
            m_ref = mf_ref if ch < ML_HEADS else mb_ref
            m_rows = jnp.where(in_chunk & (chain == ch), m_ref[step * nsub + c, ch % ML_HEADS], m_rows)
    mt = jnp.maximum(g_rows, m_rows)
    rf = jnp.exp2(g_rows - mt)
    wp = jnp.exp2(m_rows - mt)
    emt = jnp.exp2(-b_rows - mt)

    a_cols = []
    for d in range(2):
        lo = nch + ML_HEADS * d
        a_cols.append(gc[:, ML_HEADS * d:ML_HEADS * (d + 1)] - _cumsum3(tri_ref[d], gc[:, lo:lo + ML_HEADS]))

    ss = lax.broadcasted_iota(jnp.int32, (cg, cg), 0)
    tt = lax.broadcasted_iota(jnp.int32, (cg, cg), 1)
    gnt = gnt_ref[...]
    og = o_ref[...]
    def head_stack(x):
        lane = lax.broadcasted_iota(jnp.int32, x.shape, 1)
        zero = jnp.zeros_like(x)
        return jnp.concatenate([jnp.where((lane >= ML_DK * h) & (lane < ML_DK * (h + 1)), x, zero)
                                for h in range(ML_HEADS)], axis=0)

    for c in range(nsub):
        rs = slice(c * cg, (c + 1) * cg)
        q = qk[rs, :ML_QK]
        st_all = _dot_nt(head_stack(qk[rs, ML_QK:]), q)
        inter_all = [_dot_nt(head_stack(c_ref[c]), q) for c_ref in (cf_ref, cb_ref)]
        nn = [t for x in (nf_ref[c], nb_ref[c]) for t in _split3(x)[:2]]
        dn_all = _dot_nt(head_stack(jnp.concatenate(nn + [jnp.zeros((12, ML_QK), BF16)], axis=0)), q)
        outs = []
        for hh in range(ML_HEADS):
            dv = slice(ML_DV * hh, ML_DV * (hh + 1))
            st = st_all[hh * cg:(hh + 1) * cg]
            dn2 = dn_all[16 * hh:16 * (hh + 1)]
            ht = None
            for d in range(2):
                ch = d * ML_HEADS + hh
                mask = (ss <= tt) if d == 0 else (ss >= tt)
                e = jnp.exp2(jnp.where(mask, a_cols[d][rs, hh:hh + 1] - g_rows[ch:ch + 1, rs], ninf))
                sd = st * e
                den_i = jnp.sum(sd, axis=0, keepdims=True)
                num_t = _dot(vt[dv, rs], sd.astype(BF16))
                inter_t = inter_all[d][dv]
                wpr, rfr = wp[ch:ch + 1, rs], rf[ch:ch + 1, rs]
                den = wpr * (dn2[2 * d:2 * d + 1] + dn2[2 * d + 1:2 * d + 2]) + rfr * den_i
                inv = 1.0 / jnp.maximum(jnp.abs(den), emt[ch:ch + 1, rs])
                part = (wpr * inv) * inter_t + (rfr * inv) * num_t
                ht = part if ht is None else ht + part
            ms = jnp.mean(ht * ht, axis=0, keepdims=True)
            y = (ht * lax.rsqrt(ms + EPS) * gnt[dv]).T
            outs.append(y * jax.nn.sigmoid(og[rs, dv]))
        out_ref[rs, :] = jnp.concatenate(outs, axis=1).astype(BF16)


def _mix_rows(lay):
    return min(256, lay.lc), min(128, lay.lc), min(128, lay.lc)


def _scan_kernel(*refs, rows, ml_cg, hg_cg):
    ml_in, hg_in = refs[0:7], refs[7:14]
    ml_out, hg_out = refs[14:20], refs[20:22]
    ml_scratch, hg_scratch = refs[22:25], refs[25:26]

    @pl.when(pl.program_id(1) == 0)
    def _():
        for s in ml_scratch + hg_scratch:
            s[...] = jnp.zeros_like(s)

    _ml_scan_body(*ml_in, *ml_out, *ml_scratch, rows=rows, cg=ml_cg)
    _hg_scan_body(*hg_in, *hg_out, *hg_scratch, rows=rows, cg=hg_cg)


def _scans(lay, mqk, mvt, gc, hk, hlf, hvt):
    orows, ml_cg, hg_cg = _mix_rows(lay)
    ck = _Chunks(lay, orows)
    ml_sub, hg_sub = orows // ml_cg, orows // hg_cg
    ml_nt, hg_nt = lay.n // ml_cg, lay.n // hg_cg

    def spec(cols, fn, colblk=0):
        return pl.BlockSpec((orows, cols), lambda b, j: (fn(b, j), colblk))

    def tspec(rows_, fn):
        return pl.BlockSpec((rows_, orows), lambda b, j: (0, fn(b, j)))

    def st(shape, fn):
        nd = len(shape)
        return pl.BlockSpec(shape, lambda b, j: (fn(b, j),) + (0,) * (nd - 1))

    def ml_states(fn):
        return [st((ml_sub, ML_DV, ML_QK), fn), st((ml_sub, 1, ML_QK), fn), st((ml_sub, 1, ML_HEADS), fn)]

    ml_shapes = [jax.ShapeDtypeStruct((ml_nt, ML_DV, ML_QK), BF16),
                 jax.ShapeDtypeStruct((ml_nt, 1, ML_QK), F32),
                 jax.ShapeDtypeStruct((ml_nt, 1, ML_HEADS), F32)]
    hg_shape = jax.ShapeDtypeStruct((hg_nt, HG_DV, HG_K), BF16)
    tri_spec = pl.BlockSpec((2, orows, orows), lambda b, j: (0, 0, 0))
    res = pl.pallas_call(
        functools.partial(_scan_kernel, rows=orows, ml_cg=ml_cg, hg_cg=hg_cg),
        grid=(lay.batch, ck.steps),
        in_specs=[spec(ML_QK, ck.fwd, 1), tspec(ML_V, ck.fwd), spec(N_GATE, ck.fwd),
                  spec(ML_QK, ck.bwd, 1), tspec(ML_V, ck.bwd), spec(N_GATE, ck.bwd), tri_spec,
                  spec(HG_K, ck.fwd, 0), spec(HG_K, ck.fwd, 0), tspec(HG_V, ck.fwd),
                  spec(HG_K, ck.bwd, 1), spec(HG_K, ck.bwd, 1), tspec(HG_V, ck.bwd), tri_spec],
        out_specs=ml_states(ck.fwd) + ml_states(ck.bwd)
        + [st((hg_sub, HG_DV, HG_K), ck.fwd), st((hg_sub, HG_DV, HG_K), ck.bwd)],
        out_shape=ml_shapes + ml_shapes + [hg_shape, hg_shape],
        scratch_shapes=[pltpu.VMEM((2, ML_DV, ML_QK), F32),
                        pltpu.VMEM((2, 1, ML_QK), F32),
                        pltpu.VMEM((2, 1, ML_HEADS), F32),
                        pltpu.VMEM((2, HG_DV, HG_K), F32)],
        compiler_params=_params(("parallel", "arbitrary")),
        name="scan",
    )(mqk, mvt, gc, mqk, mvt, gc, _chunk_tri(orows, ml_cg),
      hk, hlf, hvt, hk, hlf, hvt, _chunk_tri(orows, hg_cg))
    return res[:6], res[6:]


def _mlstm_out(lay, states, mqk, mvt, mo, gc, gr, gnorm, with_ctx):
    cf, nf, mf, cb, nb, mb = states
    orows, cg, _ = _mix_rows(lay)
    nsub = orows // cg
    tri = _chunk_tri(orows, cg)
    nt = lay.n // cg
    c0 = 0 if with_ctx else lay.nc // orows
    nout = lay.n // orows - c0
    smem = pl.BlockSpec(memory_space=pltpu.SMEM)

    def rows(cols):
        return pl.BlockSpec((orows, cols), lambda c: (c + c0, 0))

    cst = pl.BlockSpec((nsub, ML_DV, ML_QK), lambda c: (c + c0, 0, 0))
    nst = pl.BlockSpec((nsub, 1, ML_QK), lambda c: (c + c0, 0, 0))
    out = pl.pallas_call(
        functools.partial(_ml_out_kernel, c0=c0, rows=orows, cg=cg),
        grid=(nout,),
        in_specs=[smem, smem, rows(2 * ML_QK), pl.BlockSpec((ML_V, orows), lambda c: (0, c + c0)),
                  rows(ML_V), rows(N_GATE),
                  pl.BlockSpec((N_GATE, orows), lambda c: (0, c + c0)),
                  cst, cst, nst, nst,
                  pl.BlockSpec((2, orows, orows), lambda c: (0, 0, 0)),
                  pl.BlockSpec((ML_V, 1), lambda c: (0, 0))],
        out_specs=pl.BlockSpec((orows, ML_V), lambda c: (c, 0)),
        out_shape=jax.ShapeDtypeStruct((nout * orows, ML_V), BF16),
        compiler_params=_params(("parallel",)),
        name="mlstm_out",
    )(mf.reshape(nt, ML_HEADS), mb.reshape(nt, ML_HEADS), mqk, mvt, mo, gc, gr, cf, cb, nf, nb, tri,
      gnorm.reshape(ML_V, 1))
    return out


def _chunk_tri(rows, cg):
    t = np.arange(rows)[:, None]
    u = np.arange(rows)[None, :]
    same = (t // cg) == (u // cg)
    return jnp.asarray(np.stack([same & (u <= t), same & (u >= t)]).astype(np.float32)).astype(BF16)


def _cumsum3(tri, x):
    w = x.shape[1]
    hi = x.astype(BF16)
    r1 = x - hi.astype(F32)
    mid = r1.astype(BF16)
    lo = (r1 - mid.astype(F32)).astype(BF16)
    bb = _dot(tri, jnp.concatenate([hi, mid, lo], axis=1))
    return bb[:, :w] + bb[:, w:2 * w] + bb[:, 2 * w:]


def _level_ref(b, bs, d):
    rows, w = b.shape
    off = bs // 2 - 1 + d
    if bs >= 8:
        pieces = [jnp.broadcast_to(b[s + off:s + off + 1], (bs, w)) for s in range(0, rows, bs)]
        return jnp.concatenate(pieces, axis=0)
    b8 = b.reshape(rows // 8, 8, w)
    u = lax.broadcasted_iota(jnp.int32, (1, 8, 1), 1)
    out = None
    for s in range(0, 8, bs):
        piece = jnp.broadcast_to(b8[:, s + off:s + off + 1, :], b8.shape)
        out = piece if out is None else jnp.where(u >= s, piece, out)
    return out.reshape(rows, w)


def _hg_scan_body(kf_ref, lff_ref, vf_ref, kb_ref, lfb_ref, vb_ref, tri_ref, sf_ref, sb_ref, s_s, *, rows, cg):
    nsub = rows // cg
    streams = ((kf_ref, lff_ref, vf_ref, sf_ref), (kb_ref, lfb_ref, vb_ref, sb_ref))
    for d, (k_ref, lf_ref, v_ref, s_out) in enumerate(streams):
        b = _cumsum3(tri_ref[d], lf_ref[...])
        last = cg - 1 if d == 0 else 0
        bls = [b[c * cg + last:c * cg + last + 1] for c in range(nsub)]
        bl_rows = jnp.concatenate([jnp.broadcast_to(bl, (cg, HG_K)) for bl in bls], axis=0)
        kd = (k_ref[...] * jnp.exp2(bl_rows - b)).astype(BF16)
        vt = v_ref[...]
        s = s_s[d]
        for c in (range(nsub) if d == 0 else reversed(range(nsub))):
            rs = slice(c * cg, (c + 1) * cg)
            s_out[c] = s.astype(BF16)
            s = jnp.exp2(bls[c]) * s + _head_outer(vt[:, rs], kd[rs], HG_HEADS)
        s_s[d] = s


def _hg_out_kernel(q_ref, k_ref, lf_ref, v_ref, g_ref, sf_ref, sb_ref, tri_ref, gn_ref, out_ref,
                   *, rows, cg, nlev):
    q = q_ref[...]
    kk = k_ref[...]
    lff = lf_ref[...]
    v = v_ref[...]
    nsub = rows // cg
    qb = q.astype(BF16)
    t_i = lax.broadcasted_iota(jnp.int32, (cg, cg), 0)
    s_i = lax.broadcasted_iota(jnp.int32, (cg, cg), 1)
    xr = t_i ^ s_i
    level = jnp.zeros((cg, cg), jnp.int32)
    for l in range(nlev):
        level = level + (xr >= 2 ** l).astype(jnp.int32)
    amat = [[None] * HG_HEADS for _ in range(nsub)]
    inter = [[None] * HG_HEADS for _ in range(nsub)]
    for d in range(2):
        k = kk[:, HG_K * d:HG_K * (d + 1)]
        b = _cumsum3(tri_ref[d], lff[:, HG_K * d:HG_K * (d + 1)])
        qe = (q * jnp.exp2(b)).astype(BF16)
        ops = [(qb, k.astype(BF16))]
        for l in range(1, nlev + 1):
            e = jnp.exp2(-jnp.abs(b - _level_ref(b, 2 ** l, d)))
            ops.append(((q * e).astype(BF16), (k * e).astype(BF16)))
        lev_d = jnp.where((t_i > s_i) if d == 0 else (t_i < s_i), level, -1)
        lev_d = jnp.where(t_i == s_i, 0, lev_d)
        s_ref = sf_ref if d == 0 else sb_ref
        for c in range(nsub):
            rs = slice(c * cg, (c + 1) * cg)
            st = s_ref[c]
            for hh in range(HG_HEADS):
                sl = slice(HG_DK * hh, HG_DK * (hh + 1))
                a = jnp.zeros((cg, cg), F32)
                for l in range(nlev + 1):
                    a = jnp.where(lev_d == l, _dot_nt(ops[l][0][rs, sl], ops[l][1][rs, sl]), a)
                it = _dot_nt(qe[rs, sl], st[:, sl])
                amat[c][hh] = a if d == 0 else amat[c][hh] + a
                inter[c][hh] = it if d == 0 else inter[c][hh] + it
    gn = gn_ref[...]
    gate = g_ref[...]
    for c in range(nsub):
        rs = slice(c * cg, (c + 1) * cg)
        outs = []
        for hh in range(HG_HEADS):
            sl = slice(HG_DV * hh, HG_DV * (hh + 1))
            o = inter[c][hh] + _dot(amat[c][hh].astype(BF16), v[rs, sl])
            outs.append(_rms(o, gn[:, sl]) * _silu(gate[rs, sl]))
        out_ref[rs, :] = jnp.concatenate(outs, axis=1).astype(BF16)


def _hgrn2_out(lay, states, hq, hk, hlf, hv, hgate, gnorm, with_ctx):
    sf, sb = states
    orows, _, cg = _mix_rows(lay)
    nsub = orows // cg
    nlev = int(np.log2(cg))
    tri = _chunk_tri(orows, cg)
    c0 = 0 if with_ctx else lay.nc // orows
    nout = lay.n // orows - c0

    def rows(cols):
        return pl.BlockSpec((orows, cols), lambda c: (c + c0, 0))

    sst = pl.BlockSpec((nsub, HG_DV, HG_K), lambda c: (c + c0, 0, 0))
    out = pl.pallas_call(
        functools.partial(_hg_out_kernel, rows=orows, cg=cg, nlev=nlev),
        grid=(nout,),
        in_specs=[rows(HG_K), rows(2 * HG_K), rows(2 * HG_K), rows(HG_V), rows(HG_V), sst, sst,
                  pl.BlockSpec((2, orows, orows), lambda c: (0, 0, 0)),
                  pl.BlockSpec((1, HG_V), lambda c: (0, 0))],
        out_specs=pl.BlockSpec((orows, HG_V), lambda c: (c, 0)),
        out_shape=jax.ShapeDtypeStruct((nout * orows, HG_V), BF16),
        compiler_params=_params(("parallel",)),
        name="hgrn2_out",
    )(hq, hk, hlf, hv, hgate, sf, sb, tri, gnorm)
    return out


def _rope_table(lay):
    l = lay.l
    rows = l // GRID_W
    row = jnp.repeat(jnp.arange(rows, dtype=F32), GRID_W)
    col = jnp.tile(jnp.arange(GRID_W, dtype=F32), rows)
    n_freq = ATT_HEAD_DIM // 4
    inv = ROPE_BASE ** (-jnp.arange(n_freq, dtype=F32) / n_freq)
    ar, ac = row[:, None] * inv, col[:, None] * inv
    cos = jnp.concatenate([jnp.cos(ar), jnp.cos(ar), jnp.cos(ac), jnp.cos(ac)], axis=1)
    sin = jnp.concatenate([-jnp.sin(ar), jnp.sin(ar), -jnp.sin(ac), jnp.sin(ac)], axis=1)
    lat = jnp.concatenate([cos, cos, sin, sin], axis=1)
    lat = jnp.tile(lat, (lay.batch, 1))
    ctx = jnp.concatenate([jnp.ones((lay.nc, 128), F32), jnp.zeros((lay.nc, 128), F32)], axis=1)
    return jnp.concatenate([ctx, lat], axis=0)


def kernel(x, c, ctx, c_ctx, w_ada, b_ada, norm_pre, norm_post, ffn_w1, ffn_w3, ffn_w2, w_in, att_sink,
           ml_conv, ml_f_bias, ml_norm, hg_lb_logits, hg_norm, w_branch_att, w_branch_ml, w_branch_hg,
           w_out):
    batch, l, d = x.shape
    lc = ctx.shape[1]
    depth = w_ada.shape[0]
    assert l % GRID_W == 0 and l & (l - 1) == 0 and lc & (lc - 1) == 0
    assert l % 256 == 0 and lc % 128 == 0 and batch + 1 <= MOD_ROWS
    lay = _Layout(batch, lc, l)
    lay_ffn = _Layout(batch, lc, l, FFN_TM)

    xs = (ctx.reshape(batch * lc, d), x.reshape(batch * l, d))
    cs = jnp.concatenate([c, c_ctx[None, :], jnp.zeros((MOD_ROWS - batch - 1, d), F32)], axis=0)
    mod_all = _modulation(cs, w_ada, b_ada).reshape(depth, MOD_ROWS, N_MOD, d)
    rope = _rope_table(lay)

    lb_all = jnp.cumsum(jax.nn.softmax(hg_lb_logits.astype(F32), axis=0), axis=0)
    lb_all = lb_all - lb_all[0:1]

    sizes = (ATT_Q, ATT_KV, ATT_KV, ML_QK, ML_QK, ML_V, ML_V, 2 * ML_HEADS, 2 * ML_HEADS,
             HG_K, HG_K, HG_K, HG_V, HG_V, 3 * d)
    offs = np.concatenate([[0], np.cumsum(sizes)])
    g0, g1 = int(offs[7]), int(offs[9])

    for layer in range(depth):
        last = layer == depth - 1
        mod = mod_all[layer]
        wl = w_in[layer]
        w_gate = wl[:, g0:g1]
        w_proj = jnp.concatenate(
            [wl[:, :g0], wl[:, g1:int(offs[14])], w_gate, jnp.zeros((d, GATE_PAD - N_GATE), F32)],
            axis=1).astype(BF16)
        w_gate_t = jnp.concatenate([wl[:, int(offs[5]):int(offs[6])], wl[:, int(offs[12]):int(offs[13])],
                                    w_gate], axis=1).T.astype(BF16)
        w_brg = wl[:, int(offs[14]):].astype(BF16)
        fb = ml_f_bias[layer].reshape(1, 2 * ML_HEADS)
        fbrow = jnp.concatenate([jnp.zeros((1, 2 * ML_HEADS), F32), fb], axis=1)
        fbcol = fbrow.reshape(N_GATE, 1)
        lb = lb_all[layer]
        lbv = jnp.stack([jnp.tile(jnp.log(lb), 2), jnp.tile(jnp.log1p(-lb), 2), jnp.tile(1.0 - lb, 2)])

        def npre(i):
            return norm_pre[layer, i].reshape(1, d)

        def npost(i):
            return norm_post[layer, i].reshape(1, d)

        def ffn(xin, j, i, latent_only=False):
            return _ffn(lay_ffn, xin, mod, j, npre(j), npost(j), ffn_w1[layer, i].astype(BF16),
                        ffn_w3[layer, i].astype(BF16), ffn_w2[layer, i].astype(BF16), latent_only)

        xs = ffn(xs, 0, 0)
        (aq, akv, mqk, mo, gc, gr, hq, hk, hlf, hv, hgate, mvt, hvt) = _project(
            lay, xs, mod, npre(1), w_proj, w_gate_t, rope, ml_conv[layer], fbrow, fbcol, lbv)
        att = _attention(lay, aq, akv, att_sink[layer], not last)
        ml_states, hg_states = _scans(lay, mqk, mvt, gc, hk, hlf, hvt)
        ml = _mlstm_out(lay, ml_states, mqk, mvt, mo, gc, gr, ml_norm[layer], not last)
        hg = _hgrn2_out(lay, hg_states, hq, hk, hlf, hv, hgate, hg_norm[layer].reshape(1, HG_V), not last)
        xs = _merge(lay, xs, att, ml, hg, mod, npre(1), npost(1), w_brg,
                    w_branch_att[layer].astype(BF16), w_branch_ml[layer].astype(BF16),
                    w_branch_hg[layer].astype(BF16), w_out[layer].astype(BF16), last)
        xs = ffn(xs, 2, 1, last)
    return xs.reshape(batch, l, d)
```

```python
import functools

import numpy as np
import jax
import jax.numpy as jnp
from jax import lax
from jax.experimental import pallas as pl
from jax.experimental.pallas import tpu as pltpu

F32 = jnp.float32
BF16 = jnp.bfloat16

ATT_HEADS = 8
ATT_KV_HEADS = 2
ATT_GROUP = ATT_HEADS // ATT_KV_HEADS
ATT_HEAD_DIM = 64
ATT_BLOCK = 128
ML_HEADS = 4
ML_DK = 64
ML_DV = 128
HG_HEADS = 4
HG_DK = 64
HG_DV = 128
GRID_W = 64
ROPE_BASE = 10000.0
EPS = 1e-6
N_MOD = 9
MOD_ROWS = 16

ATT_Q = ATT_HEADS * ATT_HEAD_DIM
ATT_KV = ATT_KV_HEADS * ATT_HEAD_DIM
ML_QK = ML_HEADS * ML_DK
ML_V = ML_HEADS * ML_DV
HG_K = HG_HEADS * HG_DK
HG_V = HG_HEADS * HG_DV
N_GATE = 4 * ML_HEADS
GATE_PAD = 128

OFF_ATT = 0
OFF_MQK = OFF_ATT + ATT_Q + 2 * ATT_KV
OFF_MVO = OFF_MQK + 2 * ML_QK
OFF_HG = OFF_MVO + 2 * ML_V
OFF_GATE = OFF_HG + 3 * HG_K + 2 * HG_V
W_PROJ_COLS = OFF_GATE + GATE_PAD

LOG2E = 1.4426950408889634
VMEM_LIMIT = 56 * 1024 * 1024
FFN_TM = 1024
FFN_SPLIT = 11


def _dot(a, b):
    return jnp.dot(a, b, preferred_element_type=F32)


def _dot_nt(a, b):
    return lax.dot_general(a, b, (((1,), (1,)), ((), ())), preferred_element_type=F32)


def _dot_tn(a, b):
    return lax.dot_general(a, b, (((0,), (0,)), ((), ())), preferred_element_type=F32)


def _rms(x, g):
    ms = jnp.mean(x * x, axis=-1, keepdims=True)
    return x * lax.rsqrt(ms + EPS) * g


def _silu(x):
    return x * jax.nn.sigmoid(x)


def _log_sigmoid(x):
    return jnp.minimum(x, 0.0) - jnp.log(1.0 + jnp.exp(-jnp.abs(x)))


def _head_lanes(x, w):
    r, h = x.shape
    return jnp.concatenate([jnp.broadcast_to(x[:, i:i + 1], (r, w)) for i in range(h)], axis=1)


def _split3(x):
    hi = x.astype(BF16)
    r1 = x - hi.astype(F32)
    mid = r1.astype(BF16)
    return hi, mid, (r1 - mid.astype(F32)).astype(BF16)


def _head_outer(vt, k, heads):
    dv, dk = vt.shape[0] // heads, k.shape[1] // heads
    big = _dot(vt, k)
    lane = lax.broadcasted_iota(jnp.int32, (dv, heads * dk), 1)
    out = big[:dv]
    for h in range(1, heads):
        out = jnp.where(lane >= h * dk, big[h * dv:(h + 1) * dv], out)
    return out


def _head_lanes_mxu(x, w):
    h = x.shape[1]
    row = lax.broadcasted_iota(jnp.int32, (3 * h, h * w), 0)
    lane = lax.broadcasted_iota(jnp.int32, (3 * h, h * w), 1)
    sel = jnp.zeros((3 * h, h * w), jnp.bool_)
    for t in range(3):
        for i in range(h):
            sel = sel | ((row == t * h + i) & (lane >= i * w) & (lane < (i + 1) * w))
    return _dot(jnp.concatenate(_split3(x), axis=1), jnp.where(sel, 1.0, 0.0).astype(BF16))


def _params(sem):
    return pltpu.CompilerParams(dimension_semantics=sem, vmem_limit_bytes=VMEM_LIMIT)


def _resident(shape):
    nd = len(shape)
    return pl.BlockSpec(shape, lambda *_: (0,) * nd, pipeline_mode=pl.Buffered(1))


def _mod_kernel(c_ref, w_ref, b_ref, o_ref):
    s = _silu(c_ref[...]).astype(BF16)
    o_ref[0] = _dot(s, w_ref[0].astype(BF16)) + b_ref[0]


def _modulation(cs, w_ada, b_ada):
    depth, d, nm = w_ada.shape
    tn = 1024
    return pl.pallas_call(
        _mod_kernel,
        grid=(depth, nm // tn),
        in_specs=[
            pl.BlockSpec((MOD_ROWS, d), lambda l, n: (0, 0)),
            pl.BlockSpec((1, d, tn), lambda l, n: (l, 0, n)),
            pl.BlockSpec((1, 1, tn), lambda l, n: (l, 0, n)),
        ],
        out_specs=pl.BlockSpec((1, MOD_ROWS, tn), lambda l, n: (l, 0, n)),
        out_shape=jax.ShapeDtypeStruct((depth, MOD_ROWS, nm), F32),
        compiler_params=_params(("parallel", "parallel")),
        name="modulation",
    )(cs, w_ada, b_ada.reshape(depth, 1, nm))


class _Layout:
    def __init__(self, batch, lc, l, tm=512):
        self.batch, self.lc, self.l = batch, lc, l
        self.nc = batch * lc
        self.n = self.nc + batch * l
        while self.nc % tm or l % tm:
            tm //= 2
        self.tm = tm
        self.nct = self.nc // tm
        self.tpb = l // tm
        self.ntiles = self.n // tm

    def mod_index(self, t):
        return jnp.where(t < self.nct, self.batch, (t - self.nct) // self.tpb)


def _ffn_split_kernel(c_ref, x_ref, *rest, nct, **kw):
    _ffn_body(jnp.where(pl.program_id(0) < nct, c_ref[...], x_ref[...]), *rest, **kw)


def _ffn_kernel(x_ref, *rest, **kw):
    _ffn_body(x_ref[...], *rest, **kw)


def _ffn_body(x, mod_ref, gpre_ref, gpost_ref, w1_ref, w3_ref, w2_ref, o_ref, *, j, halves):
    mod = mod_ref[0]
    shift, scale, gate = mod[3 * j:3 * j + 1], mod[3 * j + 1:3 * j + 2], mod[3 * j + 2:3 * j + 3]
    h = (_rms(x, gpre_ref[...]) * (1.0 + scale) + shift).astype(BF16)
    dff = w1_ref.shape[1]
    step = dff // halves
    y = None
    for c in range(halves):
        sl = slice(c * step, (c + 1) * step)
        a = _dot(h, w1_ref[:, sl])
        b = _dot(h, w3_ref[:, sl])
        part = _dot((_silu(a) * b).astype(BF16), w2_ref[sl, :])
        y = part if y is None else y + part
    o_ref[...] = x + 0.5 * gate * _rms(y, gpost_ref[...])


def _ffn(lay, xin, mod, j, gpre, gpost, w1, w3, w2, latent_only=False):
    d, dff = w1.shape
    tm = lay.tm
    halves = FFN_SPLIT if (dff // FFN_SPLIT) % 128 == 0 else 1
    tile0 = lay.nct if latent_only else 0
    ntiles = lay.ntiles - tile0
    row = pl.BlockSpec((tm, d), lambda t: (t, 0))
    if isinstance(xin, tuple):
        body = functools.partial(_ffn_split_kernel, nct=lay.nct, j=j, halves=halves)
        xs_specs = [pl.BlockSpec((tm, d), lambda t: (jnp.minimum(t, lay.nct - 1), 0)),
                    pl.BlockSpec((tm, d), lambda t: (jnp.maximum(t - lay.nct, 0), 0))]
    else:
        body = functools.partial(_ffn_kernel, j=j, halves=halves)
        xs_specs, xin = [row], (xin,)
    return pl.pallas_call(
        body,
        grid=(ntiles,),
        in_specs=xs_specs + [
            pl.BlockSpec((1, N_MOD, d), lambda t: (lay.mod_index(t + tile0), 0, 0)),
            _resident((1, d)), _resident((1, d)),
            _resident((d, dff)), _resident((d, dff)), _resident((dff, d)),
        ],
        out_specs=row,
        out_shape=jax.ShapeDtypeStruct((ntiles * tm, d), F32),
        compiler_params=_params(("parallel",)),
        name="ffn",
    )(*xin, mod, gpre, gpost, w1, w3, w2)


def _proj_kernel(x_ref, xp_ref, xn_ref, mod_ref, gpre_ref, w_ref, wgt_ref, rope_ref, conv_ref,
                 fbrow_ref, fbcol_ref, lbv_ref,
                 aq_ref, akv_ref, mqk_ref, mo_ref, gc_ref, gr_ref,
                 hq_ref, hk_ref, hlf_ref, hv_ref, hg_ref, mvt_ref, hvt_ref, *, nct, lc, l, tm):
    t = pl.program_id(0)
    mod = mod_ref[0]
    shift, scale = mod[3:4], mod[4:5]
    gpre = gpre_ref[...]

    def pre(xx):
        return (_rms(xx, gpre) * (1.0 + scale) + shift).astype(BF16)

    h = pre(x_ref[...])

    pa = _dot(h, w_ref[:, OFF_ATT:OFF_MQK])
    rope = rope_ref[...]
    cos, sin = rope[:, :128], rope[:, 128:]

    def swap(v):
        w = v.shape[1]
        lane = lax.broadcasted_iota(jnp.int32, v.shape, 1)
        return jnp.where((lane & 16) == 0, pltpu.roll(v, w - 16, 1), pltpu.roll(v, 16, 1))

    q = pa[:, :ATT_Q]
    k = pa[:, ATT_Q:ATT_Q + ATT_KV]
    cos4 = jnp.concatenate([cos] * (ATT_Q // 128), axis=1)
    sin4 = jnp.concatenate([sin] * (ATT_Q // 128), axis=1)
    q = (q * cos4 + swap(q) * sin4) * (ATT_HEAD_DIM ** -0.5 * LOG2E)
    k = k * cos + swap(k) * sin
    aq_ref[...] = q.astype(BF16)
    akv_ref[...] = jnp.concatenate([k, pa[:, ATT_Q + ATT_KV:]], axis=1).astype(BF16)

    wm = w_ref[:, OFF_MQK:OFF_MVO]
    pm = _dot(h, wm)
    pprev = _dot(pre(xp_ref[...]), wm)[7:8]
    pnext = _dot(pre(xn_ref[...]), wm)[0:1]
    r = lax.broadcasted_iota(jnp.int32, (tm, 1), 0)
    grow = t * tm + r
    is_ctx = t < nct
    pos = jnp.where(is_ctx, grow & (lc - 1), (grow - nct * tm) & (l - 1))
    last = jnp.where(is_ctx, lc - 1, l - 1)
    dn = jnp.where(r == 0, pprev, pltpu.roll(pm, 1, 0))
    dn = jnp.where(pos == 0, 0.0, dn)
    up = jnp.where(r == tm - 1, pnext, pltpu.roll(pm, tm - 1, 0))
    up = jnp.where(pos == last, 0.0, up)
    cw = conv_ref[...]
    cv = dn * cw[0:1] + pm * cw[1:2] + up * cw[2:3]
    lane = lax.broadcasted_iota(jnp.int32, (1, 2 * ML_QK), 1)
    mqk_ref[...] = (_silu(cv) * jnp.where(lane < ML_QK, 1.0, ML_DK ** -0.5)).astype(BF16)

    mo_ref[...] = _dot(h, w_ref[:, OFF_MVO + ML_V:OFF_HG])

    pg = _dot(h, w_ref[:, OFF_GATE:OFF_GATE + GATE_PAD])[:, :N_GATE] + fbrow_ref[...]
    lane16 = lax.broadcasted_iota(jnp.int32, (1, N_GATE), 1)
    gc_ref[...] = jnp.where(lane16 < 2 * ML_HEADS, pg, _log_sigmoid(pg)) * LOG2E
    pt = _dot_nt(wgt_ref[...], h)
    mvt_ref[...] = pt[:ML_V].astype(BF16)
    hvt_ref[...] = pt[ML_V:ML_V + HG_V].astype(BF16)
    pgr = pt[ML_V + HG_V:] + fbcol_ref[...]
    row16 = lax.broadcasted_iota(jnp.int32, (N_GATE, 1), 0)
    gr_ref[...] = jnp.where(row16 < 2 * ML_HEADS, pgr, _log_sigmoid(pgr)) * LOG2E

    ph = _dot(h, w_ref[:, OFF_HG:OFF_GATE])
    hq_ref[...] = _silu(ph[:, :HG_K]) * (HG_DK ** -0.5)
    z = ph[:, HG_K:3 * HG_K]
    lbv = lbv_ref[...]
    log_lb, log_1m_lb, one_m_lb = lbv[0:1], lbv[1:2], lbv[2:3]
    ls = _log_sigmoid(z)
    bv = log_1m_lb + ls
    hlf_ref[...] = (jnp.maximum(log_lb, bv) + jnp.log(1.0 + jnp.exp(-jnp.abs(log_lb - bv)))) * LOG2E
    hk_ref[...] = one_m_lb * jnp.exp(ls - z)
    hv_ref[...] = ph[:, 3 * HG_K:3 * HG_K + HG_V].astype(BF16)
    hg_ref[...] = ph[:, 3 * HG_K + HG_V:]


def _project(lay, x, mod, gpre, w, wgt, rope, conv, fbrow, fbcol, lbv):
    n, d = x.shape
    tm = lay.tm
    nb8 = n // 8

    def row(c):
        return pl.BlockSpec((tm, c), lambda t: (t, 0))

    out_cols = [(ATT_Q, BF16), (2 * ATT_KV, BF16), (2 * ML_QK, BF16), (ML_V, F32),
                (N_GATE, F32), (N_GATE, F32, None), (HG_K, F32), (2 * HG_K, F32), (2 * HG_K, F32),
                (HG_V, BF16), (HG_V, F32), (ML_V, BF16, None), (HG_V, BF16, None)]
    out_specs, out_shape = [], []
    for oc in out_cols:
        if len(oc) == 3:
            out_specs.append(pl.BlockSpec((oc[0], tm), lambda t: (0, t)))
            out_shape.append(jax.ShapeDtypeStruct((oc[0], n), oc[1]))
        else:
            out_specs.append(row(oc[0]))
            out_shape.append(jax.ShapeDtypeStruct((n, oc[0]), oc[1]))
    return pl.pallas_call(
        functools.partial(_proj_kernel, nct=lay.nct, lc=lay.lc, l=lay.l, tm=tm),
        grid=(lay.ntiles,),
        in_specs=[
            row(d),
            pl.BlockSpec((8, d), lambda t: (jnp.maximum(t * (tm // 8) - 1, 0), 0)),
            pl.BlockSpec((8, d), lambda t: (jnp.minimum((t + 1) * (tm // 8), nb8 - 1), 0)),
            pl.BlockSpec((1, N_MOD, d), lambda t: (lay.mod_index(t), 0, 0)),
            _resident((1, d)),
            _resident(w.shape), _resident(wgt.shape),
            row(256),
            _resident(conv.shape), _resident(fbrow.shape), _resident(fbcol.shape), _resident(lbv.shape),
        ],
        out_specs=out_specs,
        out_shape=out_shape,
        compiler_params=_params(("parallel",)),
        name="project",
    )(x, x, x, mod, gpre, w, wgt, rope, conv, fbrow, fbcol, lbv)


def _merge_kernel(x_ref, att_ref, ml_ref, hg_ref, mod_ref, gpre_ref, gpost_ref,
                  wg_ref, wa_ref, wm_ref, wh_ref, wo_ref, o_ref):
    x = x_ref[...]
    d = x.shape[1]
    mod = mod_ref[0]
    shift, scale, gate = mod[3:4], mod[4:5], mod[5:6]
    h = (_rms(x, gpre_ref[...]) * (1.0 + scale) + shift).astype(BF16)
    y = None
    for i, (b_ref, w_ref) in enumerate(((att_ref, wa_ref), (ml_ref, wm_ref), (hg_ref, wh_ref))):
        g = jax.nn.sigmoid(_dot(h, wg_ref[:, i * d:(i + 1) * d]))
        part = g * _dot(b_ref[...], w_ref[...])
        y = part if y is None else y + part
    yy = _dot(y.astype(BF16), wo_ref[...])
    o_ref[...] = x + gate * _rms(yy, gpost_ref[...])


def _merge(lay, x, att, ml, hg, mod, gpre, gpost, wg, wa, wm, wh, wo, latent_only=False):
    d = x.shape[1]
    tm = lay.tm
    tile0 = lay.nct if latent_only else 0
    ntiles = lay.ntiles - tile0

    def row(c, off=0):
        return pl.BlockSpec((tm, c), lambda t: (t + off, 0))

    return pl.pallas_call(
        _merge_kernel,
        grid=(ntiles,),
        in_specs=[
            row(d, tile0), row(ATT_Q), row(ML_V), row(HG_V),
            pl.BlockSpec((1, N_MOD, d), lambda t: (lay.mod_index(t + tile0), 0, 0)),
            _resident((1, d)), _resident((1, d)),
            _resident(wg.shape), _resident(wa.shape), _resident(wm.shape), _resident(wh.shape),
            _resident(wo.shape),
        ],
        out_specs=row(d),
        out_shape=jax.ShapeDtypeStruct((ntiles * tm, d), F32),
        compiler_params=_params(("parallel",)),
        name="merge",
    )(x, att, ml, hg, mod, gpre, gpost, wg, wa, wm, wh, wo)


def _swap_lane_halves(x):
    return pltpu.bitcast(pltpu.roll(pltpu.bitcast(x, jnp.uint32), 64, 1), BF16)


def _attend(q, kv, bias, sink_ref):
    tq = q.shape[0]
    assert ATT_GROUP == 4 and 2 * ATT_HEAD_DIM == 128 and ATT_KV == 128
    k, v = kv[:, :ATT_KV], kv[:, ATT_KV:]
    ks, vs = _swap_lane_halves(k), _swap_lane_halves(v)
    lo_k = lax.broadcasted_iota(jnp.int32, k.shape, 1) < ATT_HEAD_DIM
    lo_q = lax.broadcasted_iota(jnp.int32, (tq, 128), 1) < ATT_HEAD_DIM
    row = lax.broadcasted_iota(jnp.int32, (4 * tq, 1), 0)
    zero = jnp.zeros_like(k)
    zq = jnp.zeros((tq, 128), BF16)
    bias4 = None if bias is None else jnp.concatenate([bias] * 4, axis=0)
    outs = []
    for g in range(ATT_KV_HEADS):
        own, other = (k, ks) if g == 0 else (ks, k)
        k2 = jnp.where(lo_k, own, other)
        vown, voth = (v, vs) if g == 0 else (vs, v)
        va = jnp.where(lo_k, vown, zero)
        vb = jnp.where(lo_k, zero, voth)
        p0 = q[:, 256 * g:256 * g + 128]
        p1 = q[:, 256 * g + 128:256 * g + 256]
        qg = jnp.concatenate([jnp.where(lo_q, p0, zq), jnp.where(lo_q, p1, zq),
                              jnp.where(lo_q, zq, p0), jnp.where(lo_q, zq, p1)], axis=0)
        s = _dot_nt(qg, k2)
        if bias4 is not None:
            nbk = bias4.shape[1]
            s = jnp.concatenate([s[:, :nbk] + bias4, s[:, nbk:]], axis=1)
        h0 = 4 * g
        sk = jnp.where(row < tq, sink_ref[h0],
                       jnp.where(row < 2 * tq, sink_ref[h0 + 2],
                                 jnp.where(row < 3 * tq, sink_ref[h0 + 1], sink_ref[h0 + 3]))) * LOG2E
        m = jnp.maximum(jnp.max(s, axis=1, keepdims=True), sk)
        p = jnp.exp2(s - m)
        inv = 1.0 / (jnp.sum(p, axis=1, keepdims=True) + jnp.exp2(sk - m))
        pb = p.astype(BF16)
        o = _dot(pb[:2 * tq], va) * inv[:2 * tq] + _dot(pb[2 * tq:], vb) * inv[2 * tq:]
        outs += [o[:tq], o[tq:]]
    return jnp.concatenate(outs, axis=1).astype(BF16)


def _attn_kernel(sink_ref, q_ref, kl_ref, km_ref, kr_ref, kc_ref, o_ref, *, nb):
    i = pl.program_id(1)

    @pl.when(i < nb)
    def _():
        kv = jnp.concatenate([kl_ref[...], kr_ref[...], km_ref[...], kc_ref[...]], axis=0)
        r = lax.broadcasted_iota(jnp.int32, (ATT_BLOCK, ATT_BLOCK), 0)
        j = lax.broadcasted_iota(jnp.int32, (ATT_BLOCK, ATT_BLOCK), 1)
        ninf = jnp.float32(-jnp.inf)
        left = jnp.where((j >= r) & (i > 0), 0.0, ninf)
        right = jnp.where((j <= r) & (i < nb - 1), 0.0, ninf)
        o_ref[...] = _attend(q_ref[...], kv, jnp.concatenate([left, right], axis=1), sink_ref)

    @pl.when(i >= nb)
    def _():
        o_ref[...] = _attend(q_ref[...], kc_ref[...], None, sink_ref)


def _attention(lay, aq, akv, sink, with_ctx):
    nb = lay.l // ATT_BLOCK
    base = lay.nc // ATT_BLOCK
    lc = lay.lc
    ncb = lc // ATT_BLOCK if with_ctx else 0
    row0 = base if with_ctx else 0
    smem = pl.BlockSpec(memory_space=pltpu.SMEM)
    kvw = 2 * ATT_KV

    def band(off):
        return pl.BlockSpec((ATT_BLOCK, kvw),
                            lambda b, i: (base + b * nb + jnp.clip(i + off, 0, nb - 1), 0))

    def qrow(b, i):
        return jnp.where(i < nb, base + b * nb + i, b * ncb + (i - nb))

    def orow(b, i):
        return jnp.where(i < nb, row0 + b * nb + i, b * ncb + (i - nb))

    return pl.pallas_call(
        functools.partial(_attn_kernel, nb=nb),
        grid=(lay.batch, nb + ncb),
        in_specs=[smem,
                  pl.BlockSpec((ATT_BLOCK, ATT_Q), lambda b, i: (qrow(b, i), 0)),
                  band(-1), band(0), band(1),
                  pl.BlockSpec((lc, kvw), lambda b, i: (b, 0))],
        out_specs=pl.BlockSpec((ATT_BLOCK, ATT_Q), lambda b, i: (orow(b, i), 0)),
        out_shape=jax.ShapeDtypeStruct(((row0 + lay.batch * nb) * ATT_BLOCK, ATT_Q), BF16),
        compiler_params=_params(("parallel", "parallel")),
        name="attention",
    )(sink, aq, akv, akv, akv, akv)


class _Chunks:
    def __init__(self, lay, ch):
        self.ch = ch
        self.nctx = lay.lc // ch
        self.nlat = lay.l // ch
        self.base = lay.nc // ch
        self.steps = self.nctx + self.nlat
        self.total = lay.n // ch

    def fwd(self, b, j):
        return jnp.where(j < self.nctx, b * self.nctx + j, self.base + b * self.nlat + (j - self.nctx))

    def bwd(self, b, j):
        return jnp.where(j < self.nctx, b * self.nctx + (self.nctx - 1 - j),
                         self.base + b * self.nlat + (self.nlat - 1 - (j - self.nctx)))


def _ml_scan_body(kf_ref, vf_ref, gf_ref, kb_ref, vb_ref, gb_ref, tri_ref,
                  cf_ref, nf_ref, mf_ref, cb_ref, nb_ref, mb_ref, c_s, n_s, m_s, *, rows, cg):
    nsub = rows // cg
    streams = ((kf_ref, vf_ref, gf_ref, cf_ref, nf_ref, mf_ref),
               (kb_ref, vb_ref, gb_ref, cb_ref, nb_ref, mb_ref))
    for d, (k_ref, v_ref, g_ref, c_out, n_out, m_out) in enumerate(streams):
        gc = g_ref[...]
        ig = gc[:, ML_HEADS * d:ML_HEADS * (d + 1)]
        lf = gc[:, 2 * ML_HEADS + ML_HEADS * d:2 * ML_HEADS + ML_HEADS * (d + 1)]
        b = _cumsum3(tri_ref[d], lf)
        a = ig - b
        last = cg - 1 if d == 0 else 0
        gmaxs = [jnp.max(a[c * cg:(c + 1) * cg], axis=0, keepdims=True) for c in range(nsub)]
        rel = a - jnp.concatenate([jnp.broadcast_to(g, (cg, ML_HEADS)) for g in gmaxs], axis=0)
        ku = k_ref[...].astype(F32) * jnp.exp2(_head_lanes_mxu(rel, ML_DK))
        kub = ku.astype(BF16)
        vt = v_ref[...]
        m = m_s[d]
        ct = c_s[d]
        n = n_s[d]
        for c in (range(nsub) if d == 0 else reversed(range(nsub))):
            rs = slice(c * cg, (c + 1) * cg)
            mx = jnp.maximum(m, gmaxs[c])
            alpha = _head_lanes(jnp.exp2(m - mx), ML_DK)
            beta = _head_lanes(jnp.exp2(gmaxs[c] - mx), ML_DK)
            m_out[c] = m
            c_out[c] = ct.astype(BF16)
            n_out[c] = n
            ct = alpha * ct + beta * _head_outer(vt[:, rs], kub[rs], ML_HEADS)
            n = alpha * n + beta * jnp.sum(ku[rs], axis=0, keepdims=True)
            m = b[c * cg + last:c * cg + last + 1] + mx
        m_s[d] = m
        c_s[d] = ct
        n_s[d] = n


def _ml_out_kernel(mf_ref, mb_ref, qk_ref, vt_ref, o_ref, gc_ref, gr_ref, cf_ref, cb_ref, nf_ref, nb_ref,
                   tri_ref, gnt_ref, out_ref, *, c0, rows, cg):
    step = pl.program_id(0) + c0
    nsub = rows // cg
    nch = 2 * ML_HEADS
    qk = qk_ref[...]
    vt = vt_ref[...]
    gc = gc_ref[...]
    gr = gr_ref[...]
    ninf = jnp.float32(-jnp.inf)
    chain =lax.broadcasted_iota(jnp.int32, (nch, rows), 0)
    lane = lax.broadcasted_iota(jnp.int32, (nch, rows), 1)
    pos = lane & (cg - 1)
    is_fwd = chain < ML_HEADS

    l3 = jnp.concatenate(_split3(gr[nch:]), axis=0)

    def sum3(bb):
        return bb[:nch] + bb[nch:2 * nch] + bb[2 * nch:]

    b_rows = jnp.where(is_fwd, sum3(_dot_nt(l3, tri_ref[0])), sum3(_dot_nt(l3, tri_ref[1])))
    a_rows = gr[:nch] - b_rows
    g_rows = a_rows
    sh = 1
    while sh < cg:
        xf = jnp.where(pos >= sh, pltpu.roll(g_rows, sh, 1), ninf)
        xb = jnp.where(pos < cg - sh, pltpu.roll(g_rows, rows - sh, 1), ninf)
        g_rows = jnp.maximum(g_rows, jnp.where(is_fwd, xf, xb))
        sh *= 2
    m_rows = jnp.zeros((nch, rows), F32)
    for c in range(nsub):
        in_chunk = (lane >= c * cg) & (lane < (c + 1) * cg)
        for ch in range(nch):
            m_ref = mf_ref if ch < ML_HEADS else mb_ref
            m_rows = jnp.where(in_chunk & (chain == ch), m_ref[step * nsub + c, ch % ML_HEADS], m_rows)
    mt = jnp.maximum(g_rows, m_rows)
    rf = jnp.exp2(g_rows - mt)
    wp = jnp.exp2(m_rows - mt)
    emt = jnp.exp2(-b_rows - mt)

    a_cols = []
    for d in range(2):
        lo = nch + ML_HEADS * d
        a_cols.append(gc[:, ML_HEADS * d:ML_HEADS * (d + 1)] - _cumsum3(tri_ref[d], gc[:, lo:lo + ML_HEADS]))

    ss = lax.broadcasted_iota(jnp.int32, (cg, cg), 0)
    tt = lax.broadcasted_iota(jnp.int32, (cg, cg), 1)
    gnt = gnt_ref[...]
    og = o_ref[...]
    def head_stack(x):
        lane = lax.broadcasted_iota(jnp.int32, x.shape, 1)
        zero = jnp.zeros_like(x)
        return jnp.concatenate([jnp.where((lane >= ML_DK * h) & (lane < ML_DK * (h + 1)), x, zero)
                                for h in range(ML_HEADS)], axis=0)

    for c in range(nsub):
        rs = slice(c * cg, (c + 1) * cg)
        q = qk[rs, :ML_QK]
        st_all = _dot_nt(head_stack(qk[rs, ML_QK:]), q)
        inter_all = [_dot_nt(head_stack(c_ref[c]), q) for c_ref in (cf_ref, cb_ref)]
        nn = [t for x in (nf_ref[c], nb_ref[c]) for t in _split3(x)[:2]]
        dn_all = _dot_nt(head_stack(jnp.concatenate(nn + [jnp.zeros((12, ML_QK), BF16)], axis=0)), q)
        outs = []
        for hh in range(ML_HEADS):
            dv = slice(ML_DV * hh, ML_DV * (hh + 1))
            st = st_all[hh * cg:(hh + 1) * cg]
            dn2 = dn_all[16 * hh:16 * (hh + 1)]
            ht = None
            for d in range(2):
                ch = d * ML_HEADS + hh
                mask = (ss <= tt) if d == 0 else (ss >= tt)
                e = jnp.exp2(jnp.where(mask, a_cols[d][rs, hh:hh + 1] - g_rows[ch:ch + 1, rs], ninf))
                sd = st * e
                den_i = jnp.sum(sd, axis=0, keepdims=True)
                num_t = _dot(vt[dv, rs], sd.astype(BF16))
                inter_t = inter_all[d][dv]
                wpr, rfr = wp[ch:ch + 1, rs], rf[ch:ch + 1, rs]
                den = wpr * (dn2[2 * d:2 * d + 1] + dn2[2 * d + 1:2 * d + 2]) + rfr * den_i
                inv = 1.0 / jnp.maximum(jnp.abs(den), emt[ch:ch + 1, rs])
                part = (wpr * inv) * inter_t + (rfr * inv) * num_t
                ht = part if ht is None else ht + part
            ms = jnp.mean(ht * ht, axis=0, keepdims=True)
            y = (ht * lax.rsqrt(ms + EPS) * gnt[dv]).T
            outs.append(y * jax.nn.sigmoid(og[rs, dv]))
        out_ref[rs, :] = jnp.concatenate(outs, axis=1).astype(BF16)


def _mix_rows(lay):
    return min(256, lay.lc), min(128, lay.lc), min(64, lay.lc)


def _scan_kernel(*refs, rows, ml_cg, hg_cg):
    ml_in, hg_in = refs[0:7], refs[7:14]
    ml_out, hg_out = refs[14:20], refs[20:22]
    ml_scratch, hg_scratch = refs[22:25], refs[25:26]

    @pl.when(pl.program_id(1) == 0)
    def _():
        for s in ml_scratch + hg_scratch:
            s[...] = jnp.zeros_like(s)

    _ml_scan_body(*ml_in, *ml_out, *ml_scratch, rows=rows, cg=ml_cg)
    _hg_scan_body(*hg_in, *hg_out, *hg_scratch, rows=rows, cg=hg_cg)


def _scans(lay, mqk, mvt, gc, hk, hlf, hvt):
    orows, ml_cg, hg_cg = _mix_rows(lay)
    ck = _Chunks(lay, orows)
    ml_sub, hg_sub = orows // ml_cg, orows // hg_cg
    ml_nt, hg_nt = lay.n // ml_cg, lay.n // hg_cg

    def spec(cols, fn, colblk=0):
        return pl.BlockSpec((orows, cols), lambda b, j: (fn(b, j), colblk))

    def tspec(rows_, fn):
        return pl.BlockSpec((rows_, orows), lambda b, j: (0, fn(b, j)))

    def st(shape, fn):
        nd = len(shape)
        return pl.BlockSpec(shape, lambda b, j: (fn(b, j),) + (0,) * (nd - 1))

    def ml_states(fn):
        return [st((ml_sub, ML_DV, ML_QK), fn), st((ml_sub, 1, ML_QK), fn), st((ml_sub, 1, ML_HEADS), fn)]

    ml_shapes = [jax.ShapeDtypeStruct((ml_nt, ML_DV, ML_QK), BF16),
                 jax.ShapeDtypeStruct((ml_nt, 1, ML_QK), F32),
                 jax.ShapeDtypeStruct((ml_nt, 1, ML_HEADS), F32)]
    hg_shape = jax.ShapeDtypeStruct((hg_nt, HG_DV, HG_K), BF16)
    tri_spec = pl.BlockSpec((2, orows, orows), lambda b, j: (0, 0, 0))
    res = pl.pallas_call(
        functools.partial(_scan_kernel, rows=orows, ml_cg=ml_cg, hg_cg=hg_cg),
        grid=(lay.batch, ck.steps),
        in_specs=[spec(ML_QK, ck.fwd, 1), tspec(ML_V, ck.fwd), spec(N_GATE, ck.fwd),
                  spec(ML_QK, ck.bwd, 1), tspec(ML_V, ck.bwd), spec(N_GATE, ck.bwd), tri_spec,
                  spec(HG_K, ck.fwd, 0), spec(HG_K, ck.fwd, 0), tspec(HG_V, ck.fwd),
                  spec(HG_K, ck.bwd, 1), spec(HG_K, ck.bwd, 1), tspec(HG_V, ck.bwd), tri_spec],
        out_specs=ml_states(ck.fwd) + ml_states(ck.bwd)
        + [st((hg_sub, HG_DV, HG_K), ck.fwd), st((hg_sub, HG_DV, HG_K), ck.bwd)],
        out_shape=ml_shapes + ml_shapes + [hg_shape, hg_shape],
        scratch_shapes=[pltpu.VMEM((2, ML_DV, ML_QK), F32),
                        pltpu.VMEM((2, 1, ML_QK), F32),
                        pltpu.VMEM((2, 1, ML_HEADS), F32),
                        pltpu.VMEM((2, HG_DV, HG_K), F32)],
        compiler_params=_params(("parallel", "arbitrary")),
        name="scan",
    )(mqk, mvt, gc, mqk, mvt, gc, _chunk_tri(orows, ml_cg),
      hk, hlf, hvt, hk, hlf, hvt, _chunk_tri(orows, hg_cg))
    return res[:6], res[6:]


def _mlstm_out(lay, states, mqk, mvt, mo, gc, gr, gnorm, with_ctx):
    cf, nf, mf, cb, nb, mb = states
    orows, cg, _ = _mix_rows(lay)
    nsub = orows // cg
    tri = _chunk_tri(orows, cg)
    nt = lay.n // cg
    c0 = 0 if with_ctx else lay.nc // orows
    nout = lay.n // orows - c0
    smem = pl.BlockSpec(memory_space=pltpu.SMEM)

    def rows(cols):
        return pl.BlockSpec((orows, cols), lambda c: (c + c0, 0))

    cst = pl.BlockSpec((nsub, ML_DV, ML_QK), lambda c: (c + c0, 0, 0))
    nst = pl.BlockSpec((nsub, 1, ML_QK), lambda c: (c + c0, 0, 0))
    out = pl.pallas_call(
        functools.partial(_ml_out_kernel, c0=c0, rows=orows, cg=cg),
        grid=(nout,),
        in_specs=[smem, smem, rows(2 * ML_QK), pl.BlockSpec((ML_V, orows), lambda c: (0, c + c0)),
                  rows(ML_V), rows(N_GATE),
                  pl.BlockSpec((N_GATE, orows), lambda c: (0, c + c0)),
                  cst, cst, nst, nst,
                  pl.BlockSpec((2, orows, orows), lambda c: (0, 0, 0)),
                  pl.BlockSpec((ML_V, 1), lambda c: (0, 0))],
        out_specs=pl.BlockSpec((orows, ML_V), lambda c: (c, 0)),
        out_shape=jax.ShapeDtypeStruct((nout * orows, ML_V), BF16),
        compiler_params=_params(("parallel",)),
        name="mlstm_out",
    )(mf.reshape(nt, ML_HEADS), mb.reshape(nt, ML_HEADS), mqk, mvt, mo, gc, gr, cf, cb, nf, nb, tri,
      gnorm.reshape(ML_V, 1))
    return out


def _chunk_tri(rows, cg):
    t = np.arange(rows)[:, None]
    u = np.arange(rows)[None, :]
    same = (t // cg) == (u // cg)
    return jnp.asarray(np.stack([same & (u <= t), same & (u >= t)]).astype(np.float32)).astype(BF16)


def _cumsum3(tri, x):
    w = x.shape[1]
    hi = x.astype(BF16)
    r1 = x - hi.astype(F32)
    mid = r1.astype(BF16)
    lo = (r1 - mid.astype(F32)).astype(BF16)
    bb = _dot(tri, jnp.concatenate([hi, mid, lo], axis=1))
    return bb[:, :w] + bb[:, w:2 * w] + bb[:, 2 * w:]


def _level_ref(b, bs, d):
    rows, w = b.shape
    off = bs // 2 - 1 + d
    if bs >= 8:
        pieces = [jnp.broadcast_to(b[s + off:s + off + 1], (bs, w)) for s in range(0, rows, bs)]
        return jnp.concatenate(pieces, axis=0)
    b8 = b.reshape(rows // 8, 8, w)
    u = lax.broadcasted_iota(jnp.int32, (1, 8, 1), 1)
    out = None
    for s in range(0, 8, bs):
        piece = jnp.broadcast_to(b8[:, s + off:s + off + 1, :], b8.shape)
        out = piece if out is None else jnp.where(u >= s, piece, out)
    return out.reshape(rows, w)


def _hg_scan_body(kf_ref, lff_ref, vf_ref, kb_ref, lfb_ref, vb_ref, tri_ref, sf_ref, sb_ref, s_s, *, rows, cg):
    nsub = rows // cg
    streams = ((kf_ref, lff_ref, vf_ref, sf_ref), (kb_ref, lfb_ref, vb_ref, sb_ref))
    for d, (k_ref, lf_ref, v_ref, s_out) in enumerate(streams):
        b = _cumsum3(tri_ref[d], lf_ref[...])
        last = cg - 1 if d == 0 else 0
        bls = [b[c * cg + last:c * cg + last + 1] for c in range(nsub)]
        bl_rows = jnp.concatenate([jnp.broadcast_to(bl, (cg, HG_K)) for bl in bls], axis=0)
        kd = (k_ref[...] * jnp.exp2(bl_rows - b)).astype(BF16)
        vt = v_ref[...]
        s = s_s[d]
        for c in (range(nsub) if d == 0 else reversed(range(nsub))):
            rs = slice(c * cg, (c + 1) * cg)
            s_out[c] = s.astype(BF16)
            s = jnp.exp2(bls[c]) * s + _head_outer(vt[:, rs], kd[rs], HG_HEADS)
        s_s[d] = s


def _hg_out_kernel(q_ref, k_ref, lf_ref, v_ref, g_ref, sf_ref, sb_ref, tri_ref, gn_ref, out_ref,
                   *, rows, cg, nlev):
    q = q_ref[...]
    kk = k_ref[...]
    lff = lf_ref[...]
    v = v_ref[...]
    nsub = rows // cg
    qb = q.astype(BF16)
    t_i = lax.broadcasted_iota(jnp.int32, (cg, cg), 0)
    s_i = lax.broadcasted_iota(jnp.int32, (cg, cg), 1)
    xr = t_i ^ s_i
    level = jnp.zeros((cg, cg), jnp.int32)
    for l in range(nlev):
        level = level + (xr >= 2 ** l).astype(jnp.int32)
    heads = [slice(HG_DK * hh, HG_DK * (hh + 1)) for hh in range(HG_HEADS)]
    bs, levs = [], []
    for d in range(2):
        bs.append(_cumsum3(tri_ref[d], lff[:, HG_K * d:HG_K * (d + 1)]))
        lev_d = jnp.where((t_i > s_i) if d == 0 else (t_i < s_i), level, -1)
        levs.append(jnp.where(t_i == s_i, 0, lev_d))
    gn = gn_ref[...]
    gate = g_ref[...]
    for c in range(nsub):
        rs = slice(c * cg, (c + 1) * cg)
        qc = q[rs]
        amat = [None] * HG_HEADS
        inter = [None] * HG_HEADS
        for d in range(2):
            kc = kk[rs, HG_K * d:HG_K * (d + 1)]
            bc = bs[d][rs]
            a = [jnp.zeros((cg, cg), F32)] * HG_HEADS
            for l in range(nlev + 1):
                if l == 0:
                    qt, kt = qb[rs], kc.astype(BF16)
                else:
                    e = jnp.exp2(-jnp.abs(bc - _level_ref(bc, 2 ** l, d)))
                    qt, kt = (qc * e).astype(BF16), (kc * e).astype(BF16)
                a = [jnp.where(levs[d] == l, _dot_nt(qt[:, sl], kt[:, sl]), a[hh])
                     for hh, sl in enumerate(heads)]
            qe = (qc * jnp.exp2(bc)).astype(BF16)
            st = (sf_ref if d == 0 else sb_ref)[c]
            it = [_dot_nt(qe[:, sl], st[:, sl]) for sl in heads]
            amat = a if d == 0 else [x + y for x, y in zip(amat, a)]
            inter = it if d == 0 else [x + y for x, y in zip(inter, it)]
        outs = []
        for hh in range(HG_HEADS):
            sl = slice(HG_DV * hh, HG_DV * (hh + 1))
            o = inter[hh] + _dot(amat[hh].astype(BF16), v[rs, sl])
            outs.append(_rms(o, gn[:, sl]) * _silu(gate[rs, sl]))
        out_ref[rs, :] = jnp.concatenate(outs, axis=1).astype(BF16)


def _hgrn2_out(lay, states, hq, hk, hlf, hv, hgate, gnorm, with_ctx):
    sf, sb = states
    orows, _, cg = _mix_rows(lay)
    nsub = orows // cg
    nlev = int(np.log2(cg))
    tri = _chunk_tri(orows, cg)
    c0 = 0 if with_ctx else lay.nc // orows
    nout = lay.n // orows - c0

    def rows(cols):
        return pl.BlockSpec((orows, cols), lambda c: (c + c0, 0))

    sst = pl.BlockSpec((nsub, HG_DV, HG_K), lambda c: (c + c0, 0, 0))
    out = pl.pallas_call(
        functools.partial(_hg_out_kernel, rows=orows, cg=cg, nlev=nlev),
        grid=(nout,),
        in_specs=[rows(HG_K), rows(2 * HG_K), rows(2 * HG_K), rows(HG_V), rows(HG_V), sst, sst,
                  pl.BlockSpec((2, orows, orows), lambda c: (0, 0, 0)),
                  pl.BlockSpec((1, HG_V), lambda c: (0, 0))],
        out_specs=pl.BlockSpec((orows, HG_V), lambda c: (c, 0)),
        out_shape=jax.ShapeDtypeStruct((nout * orows, HG_V), BF16),
        compiler_params=_params(("parallel",)),
        name="hgrn2_out",
    )(hq, hk, hlf, hv, hgate, sf, sb, tri, gnorm)
    return out


def _rope_table(lay):
    l = lay.l
    rows = l // GRID_W
    row = jnp.repeat(jnp.arange(rows, dtype=F32), GRID_W)
    col = jnp.tile(jnp.arange(GRID_W, dtype=F32), rows)
    n_freq = ATT_HEAD_DIM // 4
    inv = ROPE_BASE ** (-jnp.arange(n_freq, dtype=F32) / n_freq)
    ar, ac = row[:, None] * inv, col[:, None] * inv
    cos = jnp.concatenate([jnp.cos(ar), jnp.cos(ar), jnp.cos(ac), jnp.cos(ac)], axis=1)
    sin = jnp.concatenate([-jnp.sin(ar), jnp.sin(ar), -jnp.sin(ac), jnp.sin(ac)], axis=1)
    lat = jnp.concatenate([cos, cos, sin, sin], axis=1)
    lat = jnp.tile(lat, (lay.batch, 1))
    ctx = jnp.concatenate([jnp.ones((lay.nc, 128), F32), jnp.zeros((lay.nc, 128), F32)], axis=1)
    return jnp.concatenate([ctx, lat], axis=0)


def kernel(x, c, ctx, c_ctx, w_ada, b_ada, norm_pre, norm_post, ffn_w1, ffn_w3, ffn_w2, w_in, att_sink,
           ml_conv, ml_f_bias, ml_norm, hg_lb_logits, hg_norm, w_branch_att, w_branch_ml, w_branch_hg,
           w_out):
    batch, l, d = x.shape
    lc = ctx.shape[1]
    depth = w_ada.shape[0]
    assert l % GRID_W == 0 and l & (l - 1) == 0 and lc & (lc - 1) == 0
    assert l % 256 == 0 and lc % 128 == 0 and batch + 1 <= MOD_ROWS
    lay = _Layout(batch, lc, l)
    lay_ffn = _Layout(batch, lc, l, FFN_TM)

    xs = (ctx.reshape(batch * lc, d), x.reshape(batch * l, d))
    cs = jnp.concatenate([c, c_ctx[None, :], jnp.zeros((MOD_ROWS - batch - 1, d), F32)], axis=0)
    mod_all = _modulation(cs, w_ada, b_ada).reshape(depth, MOD_ROWS, N_MOD, d)
    rope = _rope_table(lay)

    lb_all = jnp.cumsum(jax.nn.softmax(hg_lb_logits.astype(F32), axis=0), axis=0)
    lb_all = lb_all - lb_all[0:1]

    sizes = (ATT_Q, ATT_KV, ATT_KV, ML_QK, ML_QK, ML_V, ML_V, 2 * ML_HEADS, 2 * ML_HEADS,
             HG_K, HG_K, HG_K, HG_V, HG_V, 3 * d)
    offs = np.concatenate([[0], np.cumsum(sizes)])
    g0, g1 = int(offs[7]), int(offs[9])

    for layer in range(depth):
        last = layer == depth - 1
        mod = mod_all[layer]
        wl = w_in[layer]
        w_gate = wl[:, g0:g1]
        w_proj = jnp.concatenate(
            [wl[:, :g0], wl[:, g1:int(offs[14])], w_gate, jnp.zeros((d, GATE_PAD - N_GATE), F32)],
            axis=1).astype(BF16)
        w_gate_t = jnp.concatenate([wl[:, int(offs[5]):int(offs[6])], wl[:, int(offs[12]):int(offs[13])],
                                    w_gate], axis=1).T.astype(BF16)
        w_brg = wl[:, int(offs[14]):].astype(BF16)
        fb = ml_f_bias[layer].reshape(1, 2 * ML_HEADS)
        fbrow = jnp.concatenate([jnp.zeros((1, 2 * ML_HEADS), F32), fb], axis=1)
        fbcol = fbrow.reshape(N_GATE, 1)
        lb = lb_all[layer]
        lbv = jnp.stack([jnp.tile(jnp.log(lb), 2), jnp.tile(jnp.log1p(-lb), 2), jnp.tile(1.0 - lb, 2)])

        def npre(i):
            return norm_pre[layer, i].reshape(1, d)

        def npost(i):
            return norm_post[layer, i].reshape(1, d)

        def ffn(xin, j, i, latent_only=False):
            return _ffn(lay_ffn, xin, mod, j, npre(j), npost(j), ffn_w1[layer, i].astype(BF16),
                        ffn_w3[layer, i].astype(BF16), ffn_w2[layer, i].astype(BF16), latent_only)

        xs = ffn(xs, 0, 0)
        (aq, akv, mqk, mo, gc, gr, hq, hk, hlf, hv, hgate, mvt, hvt) = _project(
            lay, xs, mod, npre(1), w_proj, w_gate_t, rope, ml_conv[layer], fbrow, fbcol, lbv)
        att = _attention(lay, aq, akv, att_sink[layer], not last)
        ml_states, hg_states = _scans(lay, mqk, mvt, gc, hk, hlf, hvt)
        ml = _mlstm_out(lay, ml_states, mqk, mvt, mo, gc, gr, ml_norm[layer], not last)
        hg = _hgrn2_out(lay, hg_states, hq, hk, hlf, hv, hgate, hg_norm[layer].reshape(1, HG_V), not last)
        xs = _merge(lay_ffn, xs, att, ml, hg, mod, npre(1), npost(1), w_brg,
                    w_branch_att[layer].astype(BF16), w_branch_ml[layer].astype(BF16),
                    w_branch_hg[layer].astype(BF16), w_out[layer].astype(BF16), last)
        xs = ffn(xs, 2, 1, last)
    return xs.reshape(batch, l, d)
```

```python
import functools

import numpy as np
import jax
import jax.numpy as jnp
from jax import lax
from jax.experimental import pallas as pl
from jax.experimental.pallas import tpu as pltpu

F32 = jnp.float32
BF16 = jnp.bfloat16

ATT_HEADS = 8
ATT_KV_HEADS = 2
ATT_GROUP = ATT_HEADS // ATT_KV_HEADS
ATT_HEAD_DIM = 64
ATT_BLOCK = 128
ML_HEADS = 4
ML_DK = 64
ML_DV = 128
HG_HEADS = 4
HG_DK = 64
HG_DV = 128
GRID_W = 64
ROPE_BASE = 10000.0
EPS = 1e-6
N_MOD = 9
MOD_ROWS = 16

ATT_Q = ATT_HEADS * ATT_HEAD_DIM
ATT_KV = ATT_KV_HEADS * ATT_HEAD_DIM
ML_QK = ML_HEADS * ML_DK
ML_V = ML_HEADS * ML_DV
HG_K = HG_HEADS * HG_DK
HG_V = HG_HEADS * HG_DV
N_GATE = 4 * ML_HEADS
GATE_PAD = 128

OFF_ATT = 0
OFF_MQK = OFF_ATT + ATT_Q + 2 * ATT_KV
OFF_MVO = OFF_MQK + 2 * ML_QK
OFF_HG = OFF_MVO + 2 * ML_V
OFF_GATE = OFF_HG + 3 * HG_K + 2 * HG_V
W_PROJ_COLS = OFF_GATE + GATE_PAD

LOG2E = 1.4426950408889634
VMEM_LIMIT = 56 * 1024 * 1024
FFN_TM = 1024
FFN_SPLIT = 11


def _dot(a, b):
    return jnp.dot(a, b, preferred_element_type=F32)


def _dot_nt(a, b):
    return lax.dot_general(a, b, (((1,), (1,)), ((), ())), preferred_element_type=F32)


def _dot_tn(a, b):
    return lax.dot_general(a, b, (((0,), (0,)), ((), ())), preferred_element_type=F32)


def _rms(x, g):
    ms = jnp.mean(x * x, axis=-1, keepdims=True)
    return x * lax.rsqrt(ms + EPS) * g


def _silu(x):
    return x * jax.nn.sigmoid(x)


def _log_sigmoid(x):
    return jnp.minimum(x, 0.0) - jnp.log(1.0 + jnp.exp(-jnp.abs(x)))


def _head_lanes(x, w):
    r, h = x.shape
    return jnp.concatenate([jnp.broadcast_to(x[:, i:i + 1], (r, w)) for i in range(h)], axis=1)


def _split3(x):
    hi = x.astype(BF16)
    r1 = x - hi.astype(F32)
    mid = r1.astype(BF16)
    return hi, mid, (r1 - mid.astype(F32)).astype(BF16)


def _head_outer(vt, k, heads):
    dv, dk = vt.shape[0] // heads, k.shape[1] // heads
    big = _dot(vt, k)
    lane = lax.broadcasted_iota(jnp.int32, (dv, heads * dk), 1)
    out = big[:dv]
    for h in range(1, heads):
        out = jnp.where(lane >= h * dk, big[h * dv:(h + 1) * dv], out)
    return out


def _head_lanes_mxu(x, w):
    h = x.shape[1]
    row = lax.broadcasted_iota(jnp.int32, (3 * h, h * w), 0)
    lane = lax.broadcasted_iota(jnp.int32, (3 * h, h * w), 1)
    sel = jnp.zeros((3 * h, h * w), jnp.bool_)
    for t in range(3):
        for i in range(h):
            sel = sel | ((row == t * h + i) & (lane >= i * w) & (lane < (i + 1) * w))
    return _dot(jnp.concatenate(_split3(x), axis=1), jnp.where(sel, 1.0, 0.0).astype(BF16))


def _params(sem):
    return pltpu.CompilerParams(dimension_semantics=sem, vmem_limit_bytes=VMEM_LIMIT)


def _resident(shape):
    nd = len(shape)
    return pl.BlockSpec(shape, lambda *_: (0,) * nd, pipeline_mode=pl.Buffered(1))


def _mod_kernel(c_ref, w_ref, b_ref, o_ref):
    s = _silu(c_ref[...]).astype(BF16)
    o_ref[0] = _dot(s, w_ref[0].astype(BF16)) + b_ref[0]


def _modulation(cs, w_ada, b_ada):
    depth, d, nm = w_ada.shape
    tn = 1024
    return pl.pallas_call(
        _mod_kernel,
        grid=(depth, nm // tn),
        in_specs=[
            pl.BlockSpec((MOD_ROWS, d), lambda l, n: (0, 0)),
            pl.BlockSpec((1, d, tn), lambda l, n: (l, 0, n)),
            pl.BlockSpec((1, 1, tn), lambda l, n: (l, 0, n)),
        ],
        out_specs=pl.BlockSpec((1, MOD_ROWS, tn), lambda l, n: (l, 0, n)),
        out_shape=jax.ShapeDtypeStruct((depth, MOD_ROWS, nm), F32),
        compiler_params=_params(("parallel", "parallel")),
        name="modulation",
    )(cs, w_ada, b_ada.reshape(depth, 1, nm))


class _Layout:
    def __init__(self, batch, lc, l, tm=512):
        self.batch, self.lc, self.l = batch, lc, l
        self.nc = batch * lc
        self.n = self.nc + batch * l
        while self.nc % tm or l % tm:
            tm //= 2
        self.tm = tm
        self.nct = self.nc // tm
        self.tpb = l // tm
        self.ntiles = self.n // tm

    def mod_index(self, t):
        return jnp.where(t < self.nct, self.batch, (t - self.nct) // self.tpb)


def _ffn_split_kernel(c_ref, x_ref, *rest, nct, **kw):
    _ffn_body(jnp.where(pl.program_id(0) < nct, c_ref[...], x_ref[...]), *rest, **kw)


def _ffn_kernel(x_ref, *rest, **kw):
    _ffn_body(x_ref[...], *rest, **kw)


def _ffn_body(x, mod_ref, gpre_ref, gpost_ref, w1_ref, w3_ref, w2_ref, o_ref, *, j, halves):
    mod = mod_ref[0]
    shift, scale, gate = mod[3 * j:3 * j + 1], mod[3 * j + 1:3 * j + 2], mod[3 * j + 2:3 * j + 3]
    h = (_rms(x, gpre_ref[...] * (1.0 + scale)) + shift).astype(BF16)
    dff = w1_ref.shape[1]
    step = dff // halves
    y = None
    for c in range(halves):
        sl = slice(c * step, (c + 1) * step)
        a = _dot(h, w1_ref[:, sl])
        b = _dot(h, w3_ref[:, sl])
        part = _dot((_silu(a) * b).astype(BF16), w2_ref[sl, :])
        y = part if y is None else y + part
    o_ref[...] = x + 0.5 * gate * _rms(y, gpost_ref[...])


def _ffn(lay, xin, mod, j, gpre, gpost, w1, w3, w2, latent_only=False):
    d, dff = w1.shape
    tm = lay.tm
    halves = FFN_SPLIT if (dff // FFN_SPLIT) % 128 == 0 else 1
    tile0 = lay.nct if latent_only else 0
    ntiles = lay.ntiles - tile0
    row = pl.BlockSpec((tm, d), lambda t: (t, 0))
    if isinstance(xin, tuple):
        body = functools.partial(_ffn_split_kernel, nct=lay.nct, j=j, halves=halves)
        xs_specs = [pl.BlockSpec((tm, d), lambda t: (jnp.minimum(t, lay.nct - 1), 0)),
                    pl.BlockSpec((tm, d), lambda t: (jnp.maximum(t - lay.nct, 0), 0))]
    else:
        body = functools.partial(_ffn_kernel, j=j, halves=halves)
        xs_specs, xin = [row], (xin,)
    return pl.pallas_call(
        body,
        grid=(ntiles,),
        in_specs=xs_specs + [
            pl.BlockSpec((1, N_MOD, d), lambda t: (lay.mod_index(t + tile0), 0, 0)),
            _resident((1, d)), _resident((1, d)),
            _resident((d, dff)), _resident((d, dff)), _resident((dff, d)),
        ],
        out_specs=row,
        out_shape=jax.ShapeDtypeStruct((ntiles * tm, d), F32),
        compiler_params=_params(("parallel",)),
        name="ffn",
    )(*xin, mod, gpre, gpost, w1, w3, w2)


def _proj_kernel(x_ref, xp_ref, xn_ref, mod_ref, gpre_ref, w_ref, wgt_ref, rope_ref, conv_ref,
                 fbrow_ref, fbcol_ref, lbv_ref,
                 aq_ref, akv_ref, mqk_ref, mo_ref, gc_ref, gr_ref,
                 hq_ref, hk_ref, hlf_ref, hv_ref, hg_ref, mvt_ref, hvt_ref, *, nct, lc, l, tm):
    t = pl.program_id(0)
    mod = mod_ref[0]
    shift, scale = mod[3:4], mod[4:5]
    gpre = gpre_ref[...]

    gain = gpre * (1.0 + scale)

    def pre(xx):
        return (_rms(xx, gain) + shift).astype(BF16)

    h = pre(x_ref[...])

    pa = _dot(h, w_ref[:, OFF_ATT:OFF_MQK])
    rope = rope_ref[...]
    cos, sin = rope[:, :128], rope[:, 128:]

    def swap(v):
        w = v.shape[1]
        lane = lax.broadcasted_iota(jnp.int32, v.shape, 1)
        return jnp.where((lane & 16) == 0, pltpu.roll(v, w - 16, 1), pltpu.roll(v, 16, 1))

    q = pa[:, :ATT_Q]
    k = pa[:, ATT_Q:ATT_Q + ATT_KV]
    cos4 = jnp.concatenate([cos] * (ATT_Q // 128), axis=1)
    sin4 = jnp.concatenate([sin] * (ATT_Q // 128), axis=1)
    q = (q * cos4 + swap(q) * sin4) * (ATT_HEAD_DIM ** -0.5 * LOG2E)
    k = k * cos + swap(k) * sin
    aq_ref[...] = q.astype(BF16)
    akv_ref[...] = jnp.concatenate([k, pa[:, ATT_Q + ATT_KV:]], axis=1).astype(BF16)

    wm = w_ref[:, OFF_MQK:OFF_MVO]
    pm = _dot(h, wm)
    pprev = _dot(pre(xp_ref[...]), wm)[7:8]
    pnext = _dot(pre(xn_ref[...]), wm)[0:1]
    r = lax.broadcasted_iota(jnp.int32, (tm, 1), 0)
    grow = t * tm + r
    is_ctx = t < nct
    pos = jnp.where(is_ctx, grow & (lc - 1), (grow - nct * tm) & (l - 1))
    last = jnp.where(is_ctx, lc - 1, l - 1)
    dn = jnp.where(r == 0, pprev, pltpu.roll(pm, 1, 0))
    dn = jnp.where(pos == 0, 0.0, dn)
    up = jnp.where(r == tm - 1, pnext, pltpu.roll(pm, tm - 1, 0))
    up = jnp.where(pos == last, 0.0, up)
    cw = conv_ref[...]
    cv = dn * cw[0:1] + pm * cw[1:2] + up * cw[2:3]
    lane = lax.broadcasted_iota(jnp.int32, (1, 2 * ML_QK), 1)
    mqk_ref[...] = (_silu(cv) * jnp.where(lane < ML_QK, 1.0, ML_DK ** -0.5)).astype(BF16)

    mo_ref[...] = _dot(h, w_ref[:, OFF_MVO + ML_V:OFF_HG])

    pg = _dot(h, w_ref[:, OFF_GATE:OFF_GATE + GATE_PAD])[:, :N_GATE] + fbrow_ref[...]
    lane16 = lax.broadcasted_iota(jnp.int32, (1, N_GATE), 1)
    gc_ref[...] = jnp.where(lane16 < 2 * ML_HEADS, pg, _log_sigmoid(pg)) * LOG2E
    pt = _dot_nt(wgt_ref[...], h)
    mvt_ref[...] = pt[:ML_V].astype(BF16)
    hvt_ref[...] = pt[ML_V:ML_V + HG_V].astype(BF16)
    pgr = pt[ML_V + HG_V:] + fbcol_ref[...]
    row16 = lax.broadcasted_iota(jnp.int32, (N_GATE, 1), 0)
    gr_ref[...] = jnp.where(row16 < 2 * ML_HEADS, pgr, _log_sigmoid(pgr)) * LOG2E

    ph = _dot(h, w_ref[:, OFF_HG:OFF_GATE])
    hq_ref[...] = _silu(ph[:, :HG_K]) * (HG_DK ** -0.5)
    z = ph[:, HG_K:3 * HG_K]
    lbv = lbv_ref[...]
    log_lb, log_1m_lb, one_m_lb = lbv[0:1], lbv[1:2], lbv[2:3]
    ls = _log_sigmoid(z)
    bv = log_1m_lb + ls
    hlf_ref[...] = (jnp.maximum(log_lb, bv) + jnp.log(1.0 + jnp.exp(-jnp.abs(log_lb - bv)))) * LOG2E
    hk_ref[...] = one_m_lb * jnp.exp(ls - z)
    hv_ref[...] = ph[:, 3 * HG_K:3 * HG_K + HG_V].astype(BF16)
    hg_ref[...] = ph[:, 3 * HG_K + HG_V:]


def _project(lay, x, mod, gpre, w, wgt, rope, conv, fbrow, fbcol, lbv):
    n, d = x.shape
    tm = lay.tm
    nb8 = n // 8

    def row(c):
        return pl.BlockSpec((tm, c), lambda t: (t, 0))

    out_cols = [(ATT_Q, BF16), (2 * ATT_KV, BF16), (2 * ML_QK, BF16), (ML_V, F32),
                (N_GATE, F32), (N_GATE, F32, None), (HG_K, F32), (2 * HG_K, F32), (2 * HG_K, F32),
                (HG_V, BF16), (HG_V, F32), (ML_V, BF16, None), (HG_V, BF16, None)]
    out_specs, out_shape = [], []
    for oc in out_cols:
        if len(oc) == 3:
            out_specs.append(pl.BlockSpec((oc[0], tm), lambda t: (0, t)))
            out_shape.append(jax.ShapeDtypeStruct((oc[0], n), oc[1]))
        else:
            out_specs.append(row(oc[0]))
            out_shape.append(jax.ShapeDtypeStruct((n, oc[0]), oc[1]))
    return pl.pallas_call(
        functools.partial(_proj_kernel, nct=lay.nct, lc=lay.lc, l=lay.l, tm=tm),
        grid=(lay.ntiles,),
        in_specs=[
            row(d),
            pl.BlockSpec((8, d), lambda t: (jnp.maximum(t * (tm // 8) - 1, 0), 0)),
            pl.BlockSpec((8, d), lambda t: (jnp.minimum((t + 1) * (tm // 8), nb8 - 1), 0)),
            pl.BlockSpec((1, N_MOD, d), lambda t: (lay.mod_index(t), 0, 0)),
            _resident((1, d)),
            _resident(w.shape), _resident(wgt.shape),
            row(256),
            _resident(conv.shape), _resident(fbrow.shape), _resident(fbcol.shape), _resident(lbv.shape),
        ],
        out_specs=out_specs,
        out_shape=out_shape,
        compiler_params=_params(("parallel",)),
        name="project",
    )(x, x, x, mod, gpre, w, wgt, rope, conv, fbrow, fbcol, lbv)


def _merge_kernel(x_ref, att_ref, ml_ref, hg_ref, mod_ref, gpre_ref, gpost_ref,
                  wg_ref, wa_ref, wm_ref, wh_ref, wo_ref, o_ref):
    x = x_ref[...]
    d = x.shape[1]
    mod = mod_ref[0]
    shift, scale, gate = mod[3:4], mod[4:5], mod[5:6]
    h = (_rms(x, gpre_ref[...] * (1.0 + scale)) + shift).astype(BF16)
    y = None
    for i, (b_ref, w_ref) in enumerate(((att_ref, wa_ref), (ml_ref, wm_ref), (hg_ref, wh_ref))):
        g = jax.nn.sigmoid(_dot(h, wg_ref[:, i * d:(i + 1) * d]))
        part = g * _dot(b_ref[...], w_ref[...])
        y = part if y is None else y + part
    yy = _dot(y.astype(BF16), wo_ref[...])
    o_ref[...] = x + gate * _rms(yy, gpost_ref[...])


def _merge(lay, x, att, ml, hg, mod, gpre, gpost, wg, wa, wm, wh, wo, latent_only=False):
    d = x.shape[1]
    tm = lay.tm
    tile0 = lay.nct if latent_only else 0
    ntiles = lay.ntiles - tile0

    def row(c, off=0):
        return pl.BlockSpec((tm, c), lambda t: (t + off, 0))

    return pl.pallas_call(
        _merge_kernel,
        grid=(ntiles,),
        in_specs=[
            row(d, tile0), row(ATT_Q), row(ML_V), row(HG_V),
            pl.BlockSpec((1, N_MOD, d), lambda t: (lay.mod_index(t + tile0), 0, 0)),
            _resident((1, d)), _resident((1, d)),
            _resident(wg.shape), _resident(wa.shape), _resident(wm.shape), _resident(wh.shape),
            _resident(wo.shape),
        ],
        out_specs=row(d),
        out_shape=jax.ShapeDtypeStruct((ntiles * tm, d), F32),
        compiler_params=_params(("parallel",)),
        name="merge",
    )(x, att, ml, hg, mod, gpre, gpost, wg, wa, wm, wh, wo)


def _swap_lane_halves(x):
    return pltpu.bitcast(pltpu.roll(pltpu.bitcast(x, jnp.uint32), 64, 1), BF16)


def _attend(q, kv, bias, sink_ref):
    tq = q.shape[0]
    assert ATT_GROUP == 4 and 2 * ATT_HEAD_DIM == 128 and ATT_KV == 128
    k, v = kv[:, :ATT_KV], kv[:, ATT_KV:]
    ks, vs = _swap_lane_halves(k), _swap_lane_halves(v)
    lo_k = lax.broadcasted_iota(jnp.int32, k.shape, 1) < ATT_HEAD_DIM
    lo_q = lax.broadcasted_iota(jnp.int32, (tq, 128), 1) < ATT_HEAD_DIM
    row = lax.broadcasted_iota(jnp.int32, (4 * tq, 1), 0)
    zero = jnp.zeros_like(k)
    zq = jnp.zeros((tq, 128), BF16)
    bias4 = None if bias is None else jnp.concatenate([bias] * 4, axis=0)
    outs = []
    for g in range(ATT_KV_HEADS):
        own, other = (k, ks) if g == 0 else (ks, k)
        k2 = jnp.where(lo_k, own, other)
        vown, voth = (v, vs) if g == 0 else (vs, v)
        va = jnp.where(lo_k, vown, zero)
        vb = jnp.where(lo_k, zero, voth)
        p0 = q[:, 256 * g:256 * g + 128]
        p1 = q[:, 256 * g + 128:256 * g + 256]
        qg = jnp.concatenate([jnp.where(lo_q, p0, zq), jnp.where(lo_q, p1, zq),
                              jnp.where(lo_q, zq, p0), jnp.where(lo_q, zq, p1)], axis=0)
        s = _dot_nt(qg, k2)
        if bias4 is not None:
            nbk = bias4.shape[1]
            s = jnp.concatenate([s[:, :nbk] + bias4, s[:, nbk:]], axis=1)
        h0 = 4 * g
        sk = jnp.where(row < tq, sink_ref[h0],
                       jnp.where(row < 2 * tq, sink_ref[h0 + 2],
                                 jnp.where(row < 3 * tq, sink_ref[h0 + 1], sink_ref[h0 + 3]))) * LOG2E
        m = jnp.maximum(jnp.max(s, axis=1, keepdims=True), sk)
        p = jnp.exp2(s - m)
        inv = 1.0 / (jnp.sum(p, axis=1, keepdims=True) + jnp.exp2(sk - m))
        pb = p.astype(BF16)
        o = _dot(pb[:2 * tq], va) * inv[:2 * tq] + _dot(pb[2 * tq:], vb) * inv[2 * tq:]
        outs += [o[:tq], o[tq:]]
    return jnp.concatenate(outs, axis=1).astype(BF16)


def _attn_kernel(sink_ref, q_ref, *refs, nb, nq, nlat):
    band, kc_ref, o_ref = refs[:nq + 2], refs[nq + 2], refs[nq + 3]
    i = pl.program_id(1)

    @pl.when(i < nlat)
    def _():
        r = lax.broadcasted_iota(jnp.int32, (ATT_BLOCK, ATT_BLOCK), 0)
        j = lax.broadcasted_iota(jnp.int32, (ATT_BLOCK, ATT_BLOCK), 1)
        ninf = jnp.float32(-jnp.inf)
        kc = kc_ref[...]
        for a in range(nq):
            blk = i * nq + a
            kv = jnp.concatenate([band[a][...], band[a + 2][...], band[a + 1][...], kc], axis=0)
            left = jnp.where((j >= r) & (blk > 0), 0.0, ninf)
            right = jnp.where((j <= r) & (blk < nb - 1), 0.0, ninf)
            rows = slice(a * ATT_BLOCK, (a + 1) * ATT_BLOCK)
            o_ref[rows, :] = _attend(q_ref[rows, :], kv, jnp.concatenate([left, right], axis=1), sink_ref)

    @pl.when(i >= nlat)
    def _():
        o_ref[...] = _attend(q_ref[...], kc_ref[...], None, sink_ref)


def _attention(lay, aq, akv, sink, with_ctx):
    nb = lay.l // ATT_BLOCK
    lc = lay.lc
    nq = 2 if (lc % (2 * ATT_BLOCK) == 0 and nb % 2 == 0) else 1
    step = nq * ATT_BLOCK
    nlat = nb // nq
    base = lay.nc // step
    nctx = lc // step if with_ctx else 0
    row0 = base if with_ctx else 0
    smem = pl.BlockSpec(memory_space=pltpu.SMEM)
    kvw = 2 * ATT_KV
    kbase = lay.nc // ATT_BLOCK

    def band(off):
        return pl.BlockSpec(
            (ATT_BLOCK, kvw),
            lambda b, i: (kbase + b * nb + jnp.clip(jnp.minimum(i, nlat - 1) * nq + off, 0, nb - 1), 0))

    def qrow(b, i):
        return jnp.where(i < nlat, base + b * nlat + i, b * nctx + (i - nlat))

    def orow(b, i):
        return jnp.where(i < nlat, row0 + b * nlat + i, b * nctx + (i - nlat))

    return pl.pallas_call(
        functools.partial(_attn_kernel, nb=nb, nq=nq, nlat=nlat),
        grid=(lay.batch, nlat + nctx),
        in_specs=[smem,
                  pl.BlockSpec((step, ATT_Q), lambda b, i: (qrow(b, i), 0))]
        + [band(off) for off in range(-1, nq + 1)]
        + [pl.BlockSpec((lc, kvw), lambda b, i: (b, 0))],
        out_specs=pl.BlockSpec((step, ATT_Q), lambda b, i: (orow(b, i), 0)),
        out_shape=jax.ShapeDtypeStruct(((row0 + lay.batch * nlat) * step, ATT_Q), BF16),
        compiler_params=_params(("parallel", "parallel")),
        name="attention",
    )(sink, aq, *([akv] * (nq + 3)))


class _Chunks:
    def __init__(self, lay, ch):
        self.ch = ch
        self.nctx = lay.lc // ch
        self.nlat = lay.l // ch
        self.base = lay.nc // ch
        self.steps = self.nctx + self.nlat
        self.total = lay.n // ch

    def fwd(self, b, j):
        return jnp.where(j < self.nctx, b * self.nctx + j, self.base + b * self.nlat + (j - self.nctx))

    def bwd(self, b, j):
        return jnp.where(j < self.nctx, b * self.nctx + (self.nctx - 1 - j),
                         self.base + b * self.nlat + (self.nlat - 1 - (j - self.nctx)))


def _ml_scan_body(kf_ref, vf_ref, gf_ref, kb_ref, vb_ref, gb_ref, tri_ref,
                  cf_ref, nf_ref, mf_ref, cb_ref, nb_ref, mb_ref, c_s, n_s, m_s, *, rows, cg):
    nsub = rows // cg
    streams = ((kf_ref, vf_ref, gf_ref, cf_ref, nf_ref, mf_ref),
               (kb_ref, vb_ref, gb_ref, cb_ref, nb_ref, mb_ref))
    for d, (k_ref, v_ref, g_ref, c_out, n_out, m_out) in enumerate(streams):
        gc = g_ref[...]
        ig = gc[:, ML_HEADS * d:ML_HEADS * (d + 1)]
        lf = gc[:, 2 * ML_HEADS + ML_HEADS * d:2 * ML_HEADS + ML_HEADS * (d + 1)]
        b = _cumsum3(tri_ref[d], lf)
        a = ig - b
        last = cg - 1 if d == 0 else 0
        gmaxs = [jnp.max(a[c * cg:(c + 1) * cg], axis=0, keepdims=True) for c in range(nsub)]
        rel = a - jnp.concatenate([jnp.broadcast_to(g, (cg, ML_HEADS)) for g in gmaxs], axis=0)
        ku = k_ref[...].astype(F32) * jnp.exp2(_head_lanes_mxu(rel, ML_DK))
        kub = ku.astype(BF16)
        vt = v_ref[...]
        m = m_s[d]
        ct = c_s[d]
        n = n_s[d]
        for c in (range(nsub) if d == 0 else reversed(range(nsub))):
            rs = slice(c * cg, (c + 1) * cg)
            mx = jnp.maximum(m, gmaxs[c])
            alpha = _head_lanes(jnp.exp2(m - mx), ML_DK)
            beta = _head_lanes(jnp.exp2(gmaxs[c] - mx), ML_DK)
            m_out[c] = m
            c_out[c] = ct.astype(BF16)
            n_out[c] = n
            ct = alpha * ct + beta * _head_outer(vt[:, rs], kub[rs], ML_HEADS)
            n = alpha * n + beta * jnp.sum(ku[rs], axis=0, keepdims=True)
            m = b[c * cg + last:c * cg + last + 1] + mx
        m_s[d] = m
        c_s[d] = ct
        n_s[d] = n


def _ml_out_kernel(mf_ref, mb_ref, qk_ref, vt_ref, o_ref, gc_ref, gr_ref, cf_ref, cb_ref, nf_ref, nb_ref,
                   tri_ref, gnt_ref, out_ref, *, c0, rows, cg):
    step = pl.program_id(0) + c0
    nsub = rows // cg
    nch = 2 * ML_HEADS
    qk = qk_ref[...]
    vt = vt_ref[...]
    gc = gc_ref[...]
    gr = gr_ref[...]
    ninf = jnp.float32(-jnp.inf)
    chain =lax.broadcasted_iota(jnp.int32, (nch, rows), 0)
    lane = lax.broadcasted_iota(jnp.int32, (nch, rows), 1)
    pos = lane & (cg - 1)
    is_fwd = chain < ML_HEADS

    l3 = jnp.concatenate(_split3(gr[nch:]), axis=0)

    def sum3(bb):
        return bb[:nch] + bb[nch:2 * nch] + bb[2 * nch:]

    b_rows = jnp.where(is_fwd, sum3(_dot_nt(l3, tri_ref[0])), sum3(_dot_nt(l3, tri_ref[1])))
    a_rows = gr[:nch] - b_rows
    g_rows = a_rows
    sh = 1
    while sh < cg:
        xf = jnp.where(pos >= sh, pltpu.roll(g_rows, sh, 1), ninf)
        xb = jnp.where(pos < cg - sh, pltpu.roll(g_rows, rows - sh, 1), ninf)
        g_rows = jnp.maximum(g_rows, jnp.where(is_fwd, xf, xb))
        sh *= 2
    m_rows = jnp.zeros((nch, rows), F32)
    for c in range(nsub):
        in_chunk = (lane >= c * cg) & (lane < (c + 1) * cg)
        for ch in range(nch):
            m_ref = mf_ref if ch < ML_HEADS else mb_ref
            m_rows = jnp.where(in_chunk & (chain == ch), m_ref[step * nsub + c, ch % ML_HEADS], m_rows)
    mt = jnp.maximum(g_rows, m_rows)
    rf = jnp.exp2(g_rows - mt)
    wp = jnp.exp2(m_rows - mt)
    emt = jnp.exp2(-b_rows - mt)

    a_cols = []
    for d in range(2):
        lo = nch + ML_HEADS * d
        a_cols.append(gc[:, ML_HEADS * d:ML_HEADS * (d + 1)] - _cumsum3(tri_ref[d], gc[:, lo:lo + ML_HEADS]))

    ss = lax.broadcasted_iota(jnp.int32, (cg, cg), 0)
    tt = lax.broadcasted_iota(jnp.int32, (cg, cg), 1)
    gnt = gnt_ref[...]
    og = o_ref[...]
    def head_stack(x):
        lane = lax.broadcasted_iota(jnp.int32, x.shape, 1)
        zero = jnp.zeros_like(x)
        return jnp.concatenate([jnp.where((lane >= ML_DK * h) & (lane < ML_DK * (h + 1)), x, zero)
                                for h in range(ML_HEADS)], axis=0)

    for c in range(nsub):
        rs = slice(c * cg, (c + 1) * cg)
        q = qk[rs, :ML_QK]
        st_all = _dot_nt(head_stack(qk[rs, ML_QK:]), q)
        inter_all = [_dot_nt(head_stack(c_ref[c]), q) for c_ref in (cf_ref, cb_ref)]
        nn = [t for x in (nf_ref[c], nb_ref[c]) for t in _split3(x)[:2]]
        dn_all = _dot_nt(head_stack(jnp.concatenate(nn + [jnp.zeros((12, ML_QK), BF16)], axis=0)), q)
        outs = []
        for hh in range(ML_HEADS):
            dv = slice(ML_DV * hh, ML_DV * (hh + 1))
            st = st_all[hh * cg:(hh + 1) * cg]
            dn2 = dn_all[16 * hh:16 * (hh + 1)]
            ht = None
            for d in range(2):
                ch = d * ML_HEADS + hh
                mask = (ss <= tt) if d == 0 else (ss >= tt)
                e = jnp.exp2(jnp.where(mask, a_cols[d][rs, hh:hh + 1] - g_rows[ch:ch + 1, rs], ninf))
                sd = st * e
                den_i = jnp.sum(sd, axis=0, keepdims=True)
                num_t = _dot(vt[dv, rs], sd.astype(BF16))
                inter_t = inter_all[d][dv]
                wpr, rfr = wp[ch:ch + 1, rs], rf[ch:ch + 1, rs]
                den = wpr * (dn2[2 * d:2 * d + 1] + dn2[2 * d + 1:2 * d + 2]) + rfr * den_i
                inv = 1.0 / jnp.maximum(jnp.abs(den), emt[ch:ch + 1, rs])
                part = (wpr * inv) * inter_t + (rfr * inv) * num_t
                ht = part if ht is None else ht + part
            ms = jnp.mean(ht * ht, axis=0, keepdims=True)
            y = (ht * lax.rsqrt(ms + EPS) * gnt[dv]).T
            outs.append(y * jax.nn.sigmoid(og[rs, dv]))
        out_ref[rs, :] = jnp.concatenate(outs, axis=1).astype(BF16)


def _mix_rows(lay):
    return min(256, lay.lc), min(128, lay.lc), min(64, lay.lc)


def _scan_kernel(*refs, rows, ml_cg, hg_cg):
    ml_in, hg_in = refs[0:7], refs[7:14]
    ml_out, hg_out = refs[14:20], refs[20:22]
    ml_scratch, hg_scratch = refs[22:25], refs[25:26]

    @pl.when(pl.program_id(1) == 0)
    def _():
        for s in ml_scratch + hg_scratch:
            s[...] = jnp.zeros_like(s)

    _ml_scan_body(*ml_in, *ml_out, *ml_scratch, rows=rows, cg=ml_cg)
    _hg_scan_body(*hg_in, *hg_out, *hg_scratch, rows=rows, cg=hg_cg)


def _scans(lay, mqk, mvt, gc, hk, hlf, hvt):
    orows, ml_cg, hg_cg = _mix_rows(lay)
    ck = _Chunks(lay, orows)
    ml_sub, hg_sub = orows // ml_cg, orows // hg_cg
    ml_nt, hg_nt = lay.n // ml_cg, lay.n // hg_cg

    def spec(cols, fn, colblk=0):
        return pl.BlockSpec((orows, cols), lambda b, j: (fn(b, j), colblk))

    def tspec(rows_, fn):
        return pl.BlockSpec((rows_, orows), lambda b, j: (0, fn(b, j)))

    def st(shape, fn):
        nd = len(shape)
        return pl.BlockSpec(shape, lambda b, j: (fn(b, j),) + (0,) * (nd - 1))

    def ml_states(fn):
        return [st((ml_sub, ML_DV, ML_QK), fn), st((ml_sub, 1, ML_QK), fn), st((ml_sub, 1, ML_HEADS), fn)]

    ml_shapes = [jax.ShapeDtypeStruct((ml_nt, ML_DV, ML_QK), BF16),
                 jax.ShapeDtypeStruct((ml_nt, 1, ML_QK), F32),
                 jax.ShapeDtypeStruct((ml_nt, 1, ML_HEADS), F32)]
    hg_shape = jax.ShapeDtypeStruct((hg_nt, HG_DV, HG_K), BF16)
    tri_spec = pl.BlockSpec((2, orows, orows), lambda b, j: (0, 0, 0))
    res = pl.pallas_call(
        functools.partial(_scan_kernel, rows=orows, ml_cg=ml_cg, hg_cg=hg_cg),
        grid=(lay.batch, ck.steps),
        in_specs=[spec(ML_QK, ck.fwd, 1), tspec(ML_V, ck.fwd), spec(N_GATE, ck.fwd),
                  spec(ML_QK, ck.bwd, 1), tspec(ML_V, ck.bwd), spec(N_GATE, ck.bwd), tri_spec,
                  spec(HG_K, ck.fwd, 0), spec(HG_K, ck.fwd, 0), tspec(HG_V, ck.fwd),
                  spec(HG_K, ck.bwd, 1), spec(HG_K, ck.bwd, 1), tspec(HG_V, ck.bwd), tri_spec],
        out_specs=ml_states(ck.fwd) + ml_states(ck.bwd)
        + [st((hg_sub, HG_DV, HG_K), ck.fwd), st((hg_sub, HG_DV, HG_K), ck.bwd)],
        out_shape=ml_shapes + ml_shapes + [hg_shape, hg_shape],
        scratch_shapes=[pltpu.VMEM((2, ML_DV, ML_QK), F32),
                        pltpu.VMEM((2, 1, ML_QK), F32),
                        pltpu.VMEM((2, 1, ML_HEADS), F32),
                        pltpu.VMEM((2, HG_DV, HG_K), F32)],
        compiler_params=_params(("parallel", "arbitrary")),
        name="scan",
    )(mqk, mvt, gc, mqk, mvt, gc, _chunk_tri(orows, ml_cg),
      hk, hlf, hvt, hk, hlf, hvt, _chunk_tri(orows, hg_cg))
    return res[:6], res[6:]


def _mlstm_out(lay, states, mqk, mvt, mo, gc, gr, gnorm, with_ctx):
    cf, nf, mf, cb, nb, mb = states
    orows, cg, _ = _mix_rows(lay)
    nsub = orows // cg
    tri = _chunk_tri(orows, cg)
    nt = lay.n // cg
    c0 = 0 if with_ctx else lay.nc // orows
    nout = lay.n // orows - c0
    smem = pl.BlockSpec(memory_space=pltpu.SMEM)

    def rows(cols):
        return pl.BlockSpec((orows, cols), lambda c: (c + c0, 0))

    cst = pl.BlockSpec((nsub, ML_DV, ML_QK), lambda c: (c + c0, 0, 0))
    nst = pl.BlockSpec((nsub, 1, ML_QK), lambda c: (c + c0, 0, 0))
    out = pl.pallas_call(
        functools.partial(_ml_out_kernel, c0=c0, rows=orows, cg=cg),
        grid=(nout,),
        in_specs=[smem, smem, rows(2 * ML_QK), pl.BlockSpec((ML_V, orows), lambda c: (0, c + c0)),
                  rows(ML_V), rows(N_GATE),
                  pl.BlockSpec((N_GATE, orows), lambda c: (0, c + c0)),
                  cst, cst, nst, nst,
                  pl.BlockSpec((2, orows, orows), lambda c: (0, 0, 0)),
                  pl.BlockSpec((ML_V, 1), lambda c: (0, 0))],
        out_specs=pl.BlockSpec((orows, ML_V), lambda c: (c, 0)),
        out_shape=jax.ShapeDtypeStruct((nout * orows, ML_V), BF16),
        compiler_params=_params(("parallel",)),
        name="mlstm_out",
    )(mf.reshape(nt, ML_HEADS), mb.reshape(nt, ML_HEADS), mqk, mvt, mo, gc, gr, cf, cb, nf, nb, tri,
      gnorm.reshape(ML_V, 1))
    return out


def _chunk_tri(rows, cg):
    t = np.arange(rows)[:, None]
    u = np.arange(rows)[None, :]
    same = (t // cg) == (u // cg)
    return jnp.asarray(np.stack([same & (u <= t), same & (u >= t)]).astype(np.float32)).astype(BF16)


def _cumsum3(tri, x):
    w = x.shape[1]
    hi = x.astype(BF16)
    r1 = x - hi.astype(F32)
    mid = r1.astype(BF16)
    lo = (r1 - mid.astype(F32)).astype(BF16)
    bb = _dot(tri, jnp.concatenate([hi, mid, lo], axis=1))
    return bb[:, :w] + bb[:, w:2 * w] + bb[:, 2 * w:]


def _level_ref(b, bs, d):
    rows, w = b.shape
    off = bs // 2 - 1 + d
    if bs >= 8:
        pieces = [jnp.broadcast_to(b[s + off:s + off + 1], (bs, w)) for s in range(0, rows, bs)]
        return jnp.concatenate(pieces, axis=0)
    b8 = b.reshape(rows // 8, 8, w)
    u = lax.broadcasted_iota(jnp.int32, (1, 8, 1), 1)
    out = None
    for s in range(0, 8, bs):
        piece = jnp.broadcast_to(b8[:, s + off:s + off + 1, :], b8.shape)
        out = piece if out is None else jnp.where(u >= s, piece, out)
    return out.reshape(rows, w)


def _hg_scan_body(kf_ref, lff_ref, vf_ref, kb_ref, lfb_ref, vb_ref, tri_ref, sf_ref, sb_ref, s_s, *, rows, cg):
    nsub = rows // cg
    streams = ((kf_ref, lff_ref, vf_ref, sf_ref), (kb_ref, lfb_ref, vb_ref, sb_ref))
    for d, (k_ref, lf_ref, v_ref, s_out) in enumerate(streams):
        b = _cumsum3(tri_ref[d], lf_ref[...])
        last = cg - 1 if d == 0 else 0
        bls = [b[c * cg + last:c * cg + last + 1] for c in range(nsub)]
        bl_rows = jnp.concatenate([jnp.broadcast_to(bl, (cg, HG_K)) for bl in bls], axis=0)
        kd = (k_ref[...] * jnp.exp2(bl_rows - b)).astype(BF16)
        vt = v_ref[...]
        s = s_s[d]
        for c in (range(nsub) if d == 0 else reversed(range(nsub))):
            rs = slice(c * cg, (c + 1) * cg)
            s_out[c] = s.astype(BF16)
            s = jnp.exp2(bls[c]) * s + _head_outer(vt[:, rs], kd[rs], HG_HEADS)
        s_s[d] = s


def _hg_out_kernel(q_ref, k_ref, lf_ref, v_ref, g_ref, sf_ref, sb_ref, tri_ref, gn_ref, out_ref,
                   *, rows, cg, nlev):
    q = q_ref[...]
    kk = k_ref[...]
    lff = lf_ref[...]
    v = v_ref[...]
    nsub = rows // cg
    qb = q.astype(BF16)
    t_i = lax.broadcasted_iota(jnp.int32, (cg, cg), 0)
    s_i = lax.broadcasted_iota(jnp.int32, (cg, cg), 1)
    xr = t_i ^ s_i
    level = jnp.zeros((cg, cg), jnp.int32)
    for l in range(nlev):
        level = level + (xr >= 2 ** l).astype(jnp.int32)
    heads = [slice(HG_DK * hh, HG_DK * (hh + 1)) for hh in range(HG_HEADS)]
    bs, levs = [], []
    for d in range(2):
        bs.append(_cumsum3(tri_ref[d], lff[:, HG_K * d:HG_K * (d + 1)]))
        lev_d = jnp.where((t_i > s_i) if d == 0 else (t_i < s_i), level, -1)
        levs.append(jnp.where(t_i == s_i, 0, lev_d))
    gn = gn_ref[...]
    gate = g_ref[...]
    for c in range(nsub):
        rs = slice(c * cg, (c + 1) * cg)
        qc = q[rs]
        amat = [None] * HG_HEADS
        inter = [None] * HG_HEADS
        for d in range(2):
            kc = kk[rs, HG_K * d:HG_K * (d + 1)]
            bc = bs[d][rs]
            a = [jnp.zeros((cg, cg), F32)] * HG_HEADS
            for l in range(nlev + 1):
                if l == 0:
                    qt, kt = qb[rs], kc.astype(BF16)
                else:
                    e = jnp.exp2(-jnp.abs(bc - _level_ref(bc, 2 ** l, d)))
                    qt, kt = (qc * e).astype(BF16), (kc * e).astype(BF16)
                a = [jnp.where(levs[d] == l, _dot_nt(qt[:, sl], kt[:, sl]), a[hh])
                     for hh, sl in enumerate(heads)]
            qe = (qc * jnp.exp2(bc)).astype(BF16)
            st = (sf_ref if d == 0 else sb_ref)[c]
            it = [_dot_nt(qe[:, sl], st[:, sl]) for sl in heads]
            amat = a if d == 0 else [x + y for x, y in zip(amat, a)]
            inter = it if d == 0 else [x + y for x, y in zip(inter, it)]
        outs = []
        for hh in range(HG_HEADS):
            sl = slice(HG_DV * hh, HG_DV * (hh + 1))
            o = inter[hh] + _dot(amat[hh].astype(BF16), v[rs, sl])
            outs.append(_rms(o, gn[:, sl]) * _silu(gate[rs, sl]))
        out_ref[rs, :] = jnp.concatenate(outs, axis=1).astype(BF16)


def _hgrn2_out(lay, states, hq, hk, hlf, hv, hgate, gnorm, with_ctx):
    sf, sb = states
    orows, _, cg = _mix_rows(lay)
    nsub = orows // cg
    nlev = int(np.log2(cg))
    tri = _chunk_tri(orows, cg)
    c0 = 0 if with_ctx else lay.nc // orows
    nout = lay.n // orows - c0

    def rows(cols):
        return pl.BlockSpec((orows, cols), lambda c: (c + c0, 0))

    sst = pl.BlockSpec((nsub, HG_DV, HG_K), lambda c: (c + c0, 0, 0))
    out = pl.pallas_call(
        functools.partial(_hg_out_kernel, rows=orows, cg=cg, nlev=nlev),
        grid=(nout,),
        in_specs=[rows(HG_K), rows(2 * HG_K), rows(2 * HG_K), rows(HG_V), rows(HG_V), sst, sst,
                  pl.BlockSpec((2, orows, orows), lambda c: (0, 0, 0)),
                  pl.BlockSpec((1, HG_V), lambda c: (0, 0))],
        out_specs=pl.BlockSpec((orows, HG_V), lambda c: (c, 0)),
        out_shape=jax.ShapeDtypeStruct((nout * orows, HG_V), BF16),
        compiler_params=_params(("parallel",)),
        name="hgrn2_out",
    )(hq, hk, hlf, hv, hgate, sf, sb, tri, gnorm)
    return out


def _rope_table(lay):
    l = lay.l
    rows = l // GRID_W
    row = jnp.repeat(jnp.arange(rows, dtype=F32), GRID_W)
    col = jnp.tile(jnp.arange(GRID_W, dtype=F32), rows)
    n_freq = ATT_HEAD_DIM // 4
    inv = ROPE_BASE ** (-jnp.arange(n_freq, dtype=F32) / n_freq)
    ar, ac = row[:, None] * inv, col[:, None] * inv
    cos = jnp.concatenate([jnp.cos(ar), jnp.cos(ar), jnp.cos(ac), jnp.cos(ac)], axis=1)
    sin = jnp.concatenate([-jnp.sin(ar), jnp.sin(ar), -jnp.sin(ac), jnp.sin(ac)], axis=1)
    lat = jnp.concatenate([cos, cos, sin, sin], axis=1)
    lat = jnp.tile(lat, (lay.batch, 1))
    ctx = jnp.concatenate([jnp.ones((lay.nc, 128), F32), jnp.zeros((lay.nc, 128), F32)], axis=1)
    return jnp.concatenate([ctx, lat], axis=0)


def kernel(x, c, ctx, c_ctx, w_ada, b_ada, norm_pre, norm_post, ffn_w1, ffn_w3, ffn_w2, w_in, att_sink,
           ml_conv, ml_f_bias, ml_norm, hg_lb_logits, hg_norm, w_branch_att, w_branch_ml, w_branch_hg,
           w_out):
    batch, l, d = x.shape
    lc = ctx.shape[1]
    depth = w_ada.shape[0]
    assert l % GRID_W == 0 and l & (l - 1) == 0 and lc & (lc - 1) == 0
    assert l % 256 == 0 and lc % 128 == 0 and batch + 1 <= MOD_ROWS
    lay = _Layout(batch, lc, l)
    lay_ffn = _Layout(batch, lc, l, FFN_TM)

    xs = (ctx.reshape(batch * lc, d), x.reshape(batch * l, d))
    cs = jnp.concatenate([c, c_ctx[None, :], jnp.zeros((MOD_ROWS - batch - 1, d), F32)], axis=0)
    mod_all = _modulation(cs, w_ada, b_ada).reshape(depth, MOD_ROWS, N_MOD, d)
    rope = _rope_table(lay)

    lb_all = jnp.cumsum(jax.nn.softmax(hg_lb_logits.astype(F32), axis=0), axis=0)
    lb_all = lb_all - lb_all[0:1]

    sizes = (ATT_Q, ATT_KV, ATT_KV, ML_QK, ML_QK, ML_V, ML_V, 2 * ML_HEADS, 2 * ML_HEADS,
             HG_K, HG_K, HG_K, HG_V, HG_V, 3 * d)
    offs = np.concatenate([[0], np.cumsum(sizes)])
    g0, g1 = int(offs[7]), int(offs[9])

    for layer in range(depth):
        last = layer == depth - 1
        mod = mod_all[layer]
        wl = w_in[layer]
        w_gate = wl[:, g0:g1]
        w_proj = jnp.concatenate(
            [wl[:, :g0], wl[:, g1:int(offs[14])], w_gate, jnp.zeros((d, GATE_PAD - N_GATE), F32)],
            axis=1).astype(BF16)
        w_gate_t = jnp.concatenate([wl[:, int(offs[5]):int(offs[6])], wl[:, int(offs[12]):int(offs[13])],
                                    w_gate], axis=1).T.astype(BF16)
        w_brg = wl[:, int(offs[14]):].astype(BF16)
        fb = ml_f_bias[layer].reshape(1, 2 * ML_HEADS)
        fbrow = jnp.concatenate([jnp.zeros((1, 2 * ML_HEADS), F32), fb], axis=1)
        fbcol = fbrow.reshape(N_GATE, 1)
        lb = lb_all[layer]
        lbv = jnp.stack([jnp.tile(jnp.log(lb), 2), jnp.tile(jnp.log1p(-lb), 2), jnp.tile(1.0 - lb, 2)])

        def npre(i):
            return norm_pre[layer, i].reshape(1, d)

        def npost(i):
            return norm_post[layer, i].reshape(1, d)

        def ffn(xin, j, i, latent_only=False):
            return _ffn(lay_ffn, xin, mod, j, npre(j), npost(j), ffn_w1[layer, i].astype(BF16),
                        ffn_w3[layer, i].astype(BF16), ffn_w2[layer, i].astype(BF16), latent_only)

        xs = ffn(xs, 0, 0)
        (aq, akv, mqk, mo, gc, gr, hq, hk, hlf, hv, hgate, mvt, hvt) = _project(
            lay, xs, mod, npre(1), w_proj, w_gate_t, rope, ml_conv[layer], fbrow, fbcol, lbv)
        att = _attention(lay, aq, akv, att_sink[layer], not last)
        ml_states, hg_states = _scans(lay, mqk, mvt, gc, hk, hlf, hvt)
        ml = _mlstm_out(lay, ml_states, mqk, mvt, mo, gc, gr, ml_norm[layer], not last)
        hg = _hgrn2_out(lay, hg_states, hq, hk, hlf, hv, hgate, hg_norm[layer].reshape(1, HG_V), not last)
        xs = _merge(lay_ffn, xs, att, ml, hg, mod, npre(1), npost(1), w_brg,
                    w_branch_att[layer].astype(BF16), w_branch_ml[layer].astype(BF16),
                    w_branch_hg[layer].astype(BF16), w_out[layer].astype(BF16), last)
        xs = ffn(xs, 2, 1, last)
    return xs.reshape(batch, l, d)
```

```python
import functools

import numpy as np
import jax
import jax.numpy as jnp
from jax import lax
from jax.experimental import pallas as pl
from jax.experimental.pallas import tpu as pltpu

F32 = jnp.float32
BF16 = jnp.bfloat16

ATT_HEADS = 8
ATT_KV_HEADS = 2
ATT_GROUP = ATT_HEADS // ATT_KV_HEADS
ATT_HEAD_DIM = 64
ATT_BLOCK = 128
ML_HEADS = 4
ML_DK = 64
ML_DV = 128
HG_HEADS = 4
HG_DK = 64
HG_DV = 128
GRID_W = 64
ROPE_BASE = 10000.0
EPS = 1e-6
N_MOD = 9
MOD_ROWS = 16

ATT_Q = ATT_HEADS * ATT_HEAD_DIM
ATT_KV = ATT_KV_HEADS * ATT_HEAD_DIM
ML_QK = ML_HEADS * ML_DK
ML_V = ML_HEADS * ML_DV
HG_K = HG_HEADS * HG_DK
HG_V = HG_HEADS * HG_DV
N_GATE = 4 * ML_HEADS
GATE_PAD = 128

OFF_ATT = 0
OFF_MQK = OFF_ATT + ATT_Q + 2 * ATT_KV
OFF_MVO = OFF_MQK + 2 * ML_QK
OFF_HG = OFF_MVO + 2 * ML_V
OFF_GATE = OFF_HG + 3 * HG_K + 2 * HG_V
W_PROJ_COLS = OFF_GATE + GATE_PAD

LOG2E = 1.4426950408889634
VMEM_LIMIT = 56 * 1024 * 1024
FFN_TM = 1024
FFN_SPLIT = 11


def _dot(a, b):
    return jnp.dot(a, b, preferred_element_type=F32)


def _dot_nt(a, b):
    return lax.dot_general(a, b, (((1,), (1,)), ((), ())), preferred_element_type=F32)


def _dot_tn(a, b):
    return lax.dot_general(a, b, (((0,), (0,)), ((), ())), preferred_element_type=F32)


def _rms(x, g):
    ms = jnp.mean(x * x, axis=-1, keepdims=True)
    return x * lax.rsqrt(ms + EPS) * g


def _silu(x):
    return x * jax.nn.sigmoid(x)


def _log_sigmoid(x):
    return jnp.minimum(x, 0.0) - jnp.log(1.0 + jnp.exp(-jnp.abs(x)))


def _head_lanes(x, w):
    r, h = x.shape
    return jnp.concatenate([jnp.broadcast_to(x[:, i:i + 1], (r, w)) for i in range(h)], axis=1)


def _split3(x):
    hi = x.astype(BF16)
    r1 = x - hi.astype(F32)
    mid = r1.astype(BF16)
    return hi, mid, (r1 - mid.astype(F32)).astype(BF16)


def _head_outer(vt, k, heads):
    dv, dk = vt.shape[0] // heads, k.shape[1] // heads
    assert heads % 2 == 0 and 2 * dk == 128
    lo = lax.broadcasted_iota(jnp.int32, (dv, 2 * dk), 1) < dk
    outs = []
    for p in range(heads // 2):
        big = _dot(vt[2 * p * dv:(2 * p + 2) * dv], k[:, 2 * p * dk:(2 * p + 2) * dk])
        outs.append(jnp.where(lo, big[:dv], big[dv:]))
    return jnp.concatenate(outs, axis=1)


def _head_lanes_mxu(x, w):
    h = x.shape[1]
    row = lax.broadcasted_iota(jnp.int32, (3 * h, h * w), 0)
    lane = lax.broadcasted_iota(jnp.int32, (3 * h, h * w), 1)
    sel = jnp.zeros((3 * h, h * w), jnp.bool_)
    for t in range(3):
        for i in range(h):
            sel = sel | ((row == t * h + i) & (lane >= i * w) & (lane < (i + 1) * w))
    return _dot(jnp.concatenate(_split3(x), axis=1), jnp.where(sel, 1.0, 0.0).astype(BF16))


def _params(sem):
    return pltpu.CompilerParams(dimension_semantics=sem, vmem_limit_bytes=VMEM_LIMIT)


def _resident(shape):
    nd = len(shape)
    return pl.BlockSpec(shape, lambda *_: (0,) * nd, pipeline_mode=pl.Buffered(1))


def _mod_kernel(c_ref, w_ref, b_ref, o_ref):
    s = _silu(c_ref[...]).astype(BF16)
    o_ref[0] = _dot(s, w_ref[0].astype(BF16)) + b_ref[0]


def _modulation(cs, w_ada, b_ada):
    depth, d, nm = w_ada.shape
    tn = 1024
    return pl.pallas_call(
        _mod_kernel,
        grid=(depth, nm // tn),
        in_specs=[
            pl.BlockSpec((MOD_ROWS, d), lambda l, n: (0, 0)),
            pl.BlockSpec((1, d, tn), lambda l, n: (l, 0, n)),
            pl.BlockSpec((1, 1, tn), lambda l, n: (l, 0, n)),
        ],
        out_specs=pl.BlockSpec((1, MOD_ROWS, tn), lambda l, n: (l, 0, n)),
        out_shape=jax.ShapeDtypeStruct((depth, MOD_ROWS, nm), F32),
        compiler_params=_params(("parallel", "parallel")),
        name="modulation",
    )(cs, w_ada, b_ada.reshape(depth, 1, nm))


class _Layout:
    def __init__(self, batch, lc, l, tm=512):
        self.batch, self.lc, self.l = batch, lc, l
        self.nc = batch * lc
        self.n = self.nc + batch * l
        while self.nc % tm or l % tm:
            tm //= 2
        self.tm = tm
        self.nct = self.nc // tm
        self.tpb = l // tm
        self.ntiles = self.n // tm

    def mod_index(self, t):
        return jnp.where(t < self.nct, self.batch, (t - self.nct) // self.tpb)


def _ffn_split_kernel(c_ref, x_ref, *rest, nct, **kw):
    _ffn_body(jnp.where(pl.program_id(0) < nct, c_ref[...], x_ref[...]), *rest, **kw)


def _ffn_kernel(x_ref, *rest, **kw):
    _ffn_body(x_ref[...], *rest, **kw)


def _ffn_body(x, mod_ref, gpre_ref, gpost_ref, w1_ref, w3_ref, w2_ref, o_ref, *, j, halves):
    mod = mod_ref[0]
    shift, scale, gate = mod[3 * j:3 * j + 1], mod[3 * j + 1:3 * j + 2], mod[3 * j + 2:3 * j + 3]
    h = (_rms(x, gpre_ref[...] * (1.0 + scale)) + shift).astype(BF16)
    dff = w1_ref.shape[1]
    step = dff // halves
    y = None
    for c in range(halves):
        sl = slice(c * step, (c + 1) * step)
        a = _dot(h, w1_ref[:, sl])
        b = _dot(h, w3_ref[:, sl])
        part = _dot((_silu(a) * b).astype(BF16), w2_ref[sl, :])
        y = part if y is None else y + part
    o_ref[...] = x + 0.5 * gate * _rms(y, gpost_ref[...])


def _ffn(lay, xin, mod, j, gpre, gpost, w1, w3, w2, latent_only=False):
    d, dff = w1.shape
    tm = lay.tm
    halves = FFN_SPLIT if (dff // FFN_SPLIT) % 128 == 0 else 1
    tile0 = lay.nct if latent_only else 0
    ntiles = lay.ntiles - tile0
    row = pl.BlockSpec((tm, d), lambda t: (t, 0))
    if isinstance(xin, tuple):
        body = functools.partial(_ffn_split_kernel, nct=lay.nct, j=j, halves=halves)
        xs_specs = [pl.BlockSpec((tm, d), lambda t: (jnp.minimum(t, lay.nct - 1), 0)),
                    pl.BlockSpec((tm, d), lambda t: (jnp.maximum(t - lay.nct, 0), 0))]
    else:
        body = functools.partial(_ffn_kernel, j=j, halves=halves)
        xs_specs, xin = [row], (xin,)
    return pl.pallas_call(
        body,
        grid=(ntiles,),
        in_specs=xs_specs + [
            pl.BlockSpec((1, N_MOD, d), lambda t: (lay.mod_index(t + tile0), 0, 0)),
            _resident((1, d)), _resident((1, d)),
            _resident((d, dff)), _resident((d, dff)), _resident((dff, d)),
        ],
        out_specs=row,
        out_shape=jax.ShapeDtypeStruct((ntiles * tm, d), F32),
        compiler_params=_params(("parallel",)),
        name="ffn",
    )(*xin, mod, gpre, gpost, w1, w3, w2)


def _proj_kernel(x_ref, xp_ref, xn_ref, mod_ref, gpre_ref, w_ref, wgt_ref, rope_ref, conv_ref,
                 fbrow_ref, fbcol_ref, lbv_ref,
                 aq_ref, akv_ref, mqk_ref, mo_ref, gc_ref, gr_ref,
                 hq_ref, hk_ref, hlf_ref, hv_ref, hg_ref, mvt_ref, hvt_ref, *, nct, lc, l, tm):
    t = pl.program_id(0)
    mod = mod_ref[0]
    shift, scale = mod[3:4], mod[4:5]
    gpre = gpre_ref[...]

    gain = gpre * (1.0 + scale)

    def pre(xx):
        return (_rms(xx, gain) + shift).astype(BF16)

    h = pre(x_ref[...])

    pa = _dot(h, w_ref[:, OFF_ATT:OFF_MQK])
    rope = rope_ref[...]
    cos, sin = rope[:, :128], rope[:, 128:]

    def swap(v):
        w = v.shape[1]
        lane = lax.broadcasted_iota(jnp.int32, v.shape, 1)
        return jnp.where((lane & 16) == 0, pltpu.roll(v, w - 16, 1), pltpu.roll(v, 16, 1))

    q = pa[:, :ATT_Q]
    k = pa[:, ATT_Q:ATT_Q + ATT_KV]
    cos4 = jnp.concatenate([cos] * (ATT_Q // 128), axis=1)
    sin4 = jnp.concatenate([sin] * (ATT_Q // 128), axis=1)
    q = (q * cos4 + swap(q) * sin4) * (ATT_HEAD_DIM ** -0.5 * LOG2E)
    k = k * cos + swap(k) * sin
    aq_ref[...] = q.astype(BF16)
    akv_ref[...] = jnp.concatenate([k, pa[:, ATT_Q + ATT_KV:]], axis=1).astype(BF16)

    wm = w_ref[:, OFF_MQK:OFF_MVO]
    pm = _dot(h, wm)
    pprev = _dot(pre(xp_ref[...]), wm)[7:8]
    pnext = _dot(pre(xn_ref[...]), wm)[0:1]
    r = lax.broadcasted_iota(jnp.int32, (tm, 1), 0)
    grow = t * tm + r
    is_ctx = t < nct
    pos = jnp.where(is_ctx, grow & (lc - 1), (grow - nct * tm) & (l - 1))
    last = jnp.where(is_ctx, lc - 1, l - 1)
    dn = jnp.where(r == 0, pprev, pltpu.roll(pm, 1, 0))
    dn = jnp.where(pos == 0, 0.0, dn)
    up = jnp.where(r == tm - 1, pnext, pltpu.roll(pm, tm - 1, 0))
    up = jnp.where(pos == last, 0.0, up)
    cw = conv_ref[...]
    cv = dn * cw[0:1] + pm * cw[1:2] + up * cw[2:3]
    lane = lax.broadcasted_iota(jnp.int32, (1, 2 * ML_QK), 1)
    mqk_ref[...] = (_silu(cv) * jnp.where(lane < ML_QK, 1.0, ML_DK ** -0.5)).astype(BF16)

    mo_ref[...] = _dot(h, w_ref[:, OFF_MVO + ML_V:OFF_HG])

    pg = _dot(h, w_ref[:, OFF_GATE:OFF_GATE + GATE_PAD])[:, :N_GATE] + fbrow_ref[...]
    lane16 = lax.broadcasted_iota(jnp.int32, (1, N_GATE), 1)
    gc_ref[...] = jnp.where(lane16 < 2 * ML_HEADS, pg, _log_sigmoid(pg)) * LOG2E
    pt = _dot_nt(wgt_ref[...], h)
    mvt_ref[...] = pt[:ML_V].astype(BF16)
    hvt_ref[...] = pt[ML_V:ML_V + HG_V].astype(BF16)
    pgr = pt[ML_V + HG_V:] + fbcol_ref[...]
    row16 = lax.broadcasted_iota(jnp.int32, (N_GATE, 1), 0)
    gr_ref[...] = jnp.where(row16 < 2 * ML_HEADS, pgr, _log_sigmoid(pgr)) * LOG2E

    ph = _dot(h, w_ref[:, OFF_HG:OFF_GATE])
    hq_ref[...] = _silu(ph[:, :HG_K]) * (HG_DK ** -0.5)
    z = ph[:, HG_K:3 * HG_K]
    lbv = lbv_ref[...]
    log_lb, log_1m_lb, one_m_lb = lbv[0:1], lbv[1:2], lbv[2:3]
    ls = _log_sigmoid(z)
    bv = log_1m_lb + ls
    hlf_ref[...] = (jnp.maximum(log_lb, bv) + jnp.log(1.0 + jnp.exp(-jnp.abs(log_lb - bv)))) * LOG2E
    hk_ref[...] = one_m_lb * jnp.exp(ls - z)
    hv_ref[...] = ph[:, 3 * HG_K:3 * HG_K + HG_V].astype(BF16)
    hg_ref[...] = ph[:, 3 * HG_K + HG_V:]


def _project(lay, x, mod, gpre, w, wgt, rope, conv, fbrow, fbcol, lbv):
    n, d = x.shape
    tm = lay.tm
    nb8 = n // 8

    def row(c):
        return pl.BlockSpec((tm, c), lambda t: (t, 0))

    out_cols = [(ATT_Q, BF16), (2 * ATT_KV, BF16), (2 * ML_QK, BF16), (ML_V, F32),
                (N_GATE, F32), (N_GATE, F32, None), (HG_K, F32), (2 * HG_K, F32), (2 * HG_K, F32),
                (HG_V, BF16), (HG_V, F32), (ML_V, BF16, None), (HG_V, BF16, None)]
    out_specs, out_shape = [], []
    for oc in out_cols:
        if len(oc) == 3:
            out_specs.append(pl.BlockSpec((oc[0], tm), lambda t: (0, t)))
            out_shape.append(jax.ShapeDtypeStruct((oc[0], n), oc[1]))
        else:
            out_specs.append(row(oc[0]))
            out_shape.append(jax.ShapeDtypeStruct((n, oc[0]), oc[1]))
    return pl.pallas_call(
        functools.partial(_proj_kernel, nct=lay.nct, lc=lay.lc, l=lay.l, tm=tm),
        grid=(lay.ntiles,),
        in_specs=[
            row(d),
            pl.BlockSpec((8, d), lambda t: (jnp.maximum(t * (tm // 8) - 1, 0), 0)),
            pl.BlockSpec((8, d), lambda t: (jnp.minimum((t + 1) * (tm // 8), nb8 - 1), 0)),
            pl.BlockSpec((1, N_MOD, d), lambda t: (lay.mod_index(t), 0, 0)),
            _resident((1, d)),
            _resident(w.shape), _resident(wgt.shape),
            row(256),
            _resident(conv.shape), _resident(fbrow.shape), _resident(fbcol.shape), _resident(lbv.shape),
        ],
        out_specs=out_specs,
        out_shape=out_shape,
        compiler_params=_params(("parallel",)),
        name="project",
    )(x, x, x, mod, gpre, w, wgt, rope, conv, fbrow, fbcol, lbv)


def _merge_kernel(x_ref, att_ref, ml_ref, hg_ref, mod_ref, gpre_ref, gpost_ref,
                  wg_ref, wa_ref, wm_ref, wh_ref, wo_ref, o_ref):
    x = x_ref[...]
    d = x.shape[1]
    mod = mod_ref[0]
    shift, scale, gate = mod[3:4], mod[4:5], mod[5:6]
    h = (_rms(x, gpre_ref[...] * (1.0 + scale)) + shift).astype(BF16)
    y = None
    for i, (b_ref, w_ref) in enumerate(((att_ref, wa_ref), (ml_ref, wm_ref), (hg_ref, wh_ref))):
        g = jax.nn.sigmoid(_dot(h, wg_ref[:, i * d:(i + 1) * d]))
        part = g * _dot(b_ref[...], w_ref[...])
        y = part if y is None else y + part
    yy = _dot(y.astype(BF16), wo_ref[...])
    o_ref[...] = x + gate * _rms(yy, gpost_ref[...])


def _merge(lay, x, att, ml, hg, mod, gpre, gpost, wg, wa, wm, wh, wo, latent_only=False):
    d = x.shape[1]
    tm = lay.tm
    tile0 = lay.nct if latent_only else 0
    ntiles = lay.ntiles - tile0

    def row(c, off=0):
        return pl.BlockSpec((tm, c), lambda t: (t + off, 0))

    return pl.pallas_call(
        _merge_kernel,
        grid=(ntiles,),
        in_specs=[
            row(d, tile0), row(ATT_Q), row(ML_V), row(HG_V),
            pl.BlockSpec((1, N_MOD, d), lambda t: (lay.mod_index(t + tile0), 0, 0)),
            _resident((1, d)), _resident((1, d)),
            _resident(wg.shape), _resident(wa.shape), _resident(wm.shape), _resident(wh.shape),
            _resident(wo.shape),
        ],
        out_specs=row(d),
        out_shape=jax.ShapeDtypeStruct((ntiles * tm, d), F32),
        compiler_params=_params(("parallel",)),
        name="merge",
    )(x, att, ml, hg, mod, gpre, gpost, wg, wa, wm, wh, wo)


def _swap_lane_halves(x):
    return pltpu.bitcast(pltpu.roll(pltpu.bitcast(x, jnp.uint32), 64, 1), BF16)


def _attend(q, kv, bias, sink_ref):
    tq = q.shape[0]
    assert ATT_GROUP == 4 and 2 * ATT_HEAD_DIM == 128 and ATT_KV == 128
    k, v = kv[:, :ATT_KV], kv[:, ATT_KV:]
    ks, vs = _swap_lane_halves(k), _swap_lane_halves(v)
    lo_k = lax.broadcasted_iota(jnp.int32, k.shape, 1) < ATT_HEAD_DIM
    lo_q = lax.broadcasted_iota(jnp.int32, (tq, 128), 1) < ATT_HEAD_DIM
    row = lax.broadcasted_iota(jnp.int32, (4 * tq, 1), 0)
    zero = jnp.zeros_like(k)
    zq = jnp.zeros((tq, 128), BF16)
    bias4 = None if bias is None else jnp.concatenate([bias] * 4, axis=0)
    outs = []
    for g in range(ATT_KV_HEADS):
        own, other = (k, ks) if g == 0 else (ks, k)
        k2 = jnp.where(lo_k, own, other)
        vown, voth = (v, vs) if g == 0 else (vs, v)
        va = jnp.where(lo_k, vown, zero)
        vb = jnp.where(lo_k, zero, voth)
        p0 = q[:, 256 * g:256 * g + 128]
        p1 = q[:, 256 * g + 128:256 * g + 256]
        qg = jnp.concatenate([jnp.where(lo_q, p0, zq), jnp.where(lo_q, p1, zq),
                              jnp.where(lo_q, zq, p0), jnp.where(lo_q, zq, p1)], axis=0)
        s = _dot_nt(qg, k2)
        if bias4 is not None:
            nbk = bias4.shape[1]
            s = jnp.concatenate([s[:, :nbk] + bias4, s[:, nbk:]], axis=1)
        h0 = 4 * g
        sk = jnp.where(row < tq, sink_ref[h0],
                       jnp.where(row < 2 * tq, sink_ref[h0 + 2],
                                 jnp.where(row < 3 * tq, sink_ref[h0 + 1], sink_ref[h0 + 3]))) * LOG2E
        m = jnp.maximum(jnp.max(s, axis=1, keepdims=True), sk)
        p = jnp.exp2(s - m)
        inv = 1.0 / (jnp.sum(p, axis=1, keepdims=True) + jnp.exp2(sk - m))
        pb = p.astype(BF16)
        o = _dot(pb[:2 * tq], va) * inv[:2 * tq] + _dot(pb[2 * tq:], vb) * inv[2 * tq:]
        outs += [o[:tq], o[tq:]]
    return jnp.concatenate(outs, axis=1).astype(BF16)


def _attn_kernel(sink_ref, q_ref, *refs, nb, nq, nlat):
    band, kc_ref, o_ref = refs[:nq + 2], refs[nq + 2], refs[nq + 3]
    i = pl.program_id(1)

    @pl.when(i < nlat)
    def _():
        r = lax.broadcasted_iota(jnp.int32, (ATT_BLOCK, ATT_BLOCK), 0)
        j = lax.broadcasted_iota(jnp.int32, (ATT_BLOCK, ATT_BLOCK), 1)
        ninf = jnp.float32(-jnp.inf)
        kc = kc_ref[...]
        for a in range(nq):
            blk = i * nq + a
            kv = jnp.concatenate([band[a][...], band[a + 2][...], band[a + 1][...], kc], axis=0)
            left = jnp.where((j >= r) & (blk > 0), 0.0, ninf)
            right = jnp.where((j <= r) & (blk < nb - 1), 0.0, ninf)
            rows = slice(a * ATT_BLOCK, (a + 1) * ATT_BLOCK)
            o_ref[rows, :] = _attend(q_ref[rows, :], kv, jnp.concatenate([left, right], axis=1), sink_ref)

    @pl.when(i >= nlat)
    def _():
        o_ref[...] = _attend(q_ref[...], kc_ref[...], None, sink_ref)


def _attention(lay, aq, akv, sink, with_ctx):
    nb = lay.l // ATT_BLOCK
    lc = lay.lc
    nq = 2 if (lc % (2 * ATT_BLOCK) == 0 and nb % 2 == 0) else 1
    step = nq * ATT_BLOCK
    nlat = nb // nq
    base = lay.nc // step
    nctx = lc // step if with_ctx else 0
    row0 = base if with_ctx else 0
    smem = pl.BlockSpec(memory_space=pltpu.SMEM)
    kvw = 2 * ATT_KV
    kbase = lay.nc // ATT_BLOCK

    def band(off):
        return pl.BlockSpec(
            (ATT_BLOCK, kvw),
            lambda b, i: (kbase + b * nb + jnp.clip(jnp.minimum(i, nlat - 1) * nq + off, 0, nb - 1), 0))

    def qrow(b, i):
        return jnp.where(i < nlat, base + b * nlat + i, b * nctx + (i - nlat))

    def orow(b, i):
        return jnp.where(i < nlat, row0 + b * nlat + i, b * nctx + (i - nlat))

    return pl.pallas_call(
        functools.partial(_attn_kernel, nb=nb, nq=nq, nlat=nlat),
        grid=(lay.batch, nlat + nctx),
        in_specs=[smem,
                  pl.BlockSpec((step, ATT_Q), lambda b, i: (qrow(b, i), 0))]
        + [band(off) for off in range(-1, nq + 1)]
        + [pl.BlockSpec((lc, kvw), lambda b, i: (b, 0))],
        out_specs=pl.BlockSpec((step, ATT_Q), lambda b, i: (orow(b, i), 0)),
        out_shape=jax.ShapeDtypeStruct(((row0 + lay.batch * nlat) * step, ATT_Q), BF16),
        compiler_params=_params(("parallel", "parallel")),
        name="attention",
    )(sink, aq, *([akv] * (nq + 3)))


class _Chunks:
    def __init__(self, lay, ch):
        self.ch = ch
        self.nctx = lay.lc // ch
        self.nlat = lay.l // ch
        self.base = lay.nc // ch
        self.steps = self.nctx + self.nlat
        self.total = lay.n // ch

    def fwd(self, b, j):
        return jnp.where(j < self.nctx, b * self.nctx + j, self.base + b * self.nlat + (j - self.nctx))

    def bwd(self, b, j):
        return jnp.where(j < self.nctx, b * self.nctx + (self.nctx - 1 - j),
                         self.base + b * self.nlat + (self.nlat - 1 - (j - self.nctx)))


def _ml_scan_body(kf_ref, vf_ref, gf_ref, kb_ref, vb_ref, gb_ref, tri_ref,
                  cf_ref, nf_ref, mf_ref, cb_ref, nb_ref, mb_ref, c_s, n_s, m_s, *, rows, cg):
    nsub = rows // cg
    streams = ((kf_ref, vf_ref, gf_ref, cf_ref, nf_ref, mf_ref),
               (kb_ref, vb_ref, gb_ref, cb_ref, nb_ref, mb_ref))
    for d, (k_ref, v_ref, g_ref, c_out, n_out, m_out) in enumerate(streams):
        gc = g_ref[...]
        ig = gc[:, ML_HEADS * d:ML_HEADS * (d + 1)]
        lf = gc[:, 2 * ML_HEADS + ML_HEADS * d:2 * ML_HEADS + ML_HEADS * (d + 1)]
        b = _cumsum3(tri_ref[d], lf)
        a = ig - b
        last = cg - 1 if d == 0 else 0
        gmaxs = [jnp.max(a[c * cg:(c + 1) * cg], axis=0, keepdims=True) for c in range(nsub)]
        rel = a - jnp.concatenate([jnp.broadcast_to(g, (cg, ML_HEADS)) for g in gmaxs], axis=0)
        ku = k_ref[...].astype(F32) * jnp.exp2(_head_lanes_mxu(rel, ML_DK))
        kub = ku.astype(BF16)
        vt = v_ref[...]
        m = m_s[d]
        ct = c_s[d]
        n = n_s[d]
        for c in (range(nsub) if d == 0 else reversed(range(nsub))):
            rs = slice(c * cg, (c + 1) * cg)
            mx = jnp.maximum(m, gmaxs[c])
            alpha = _head_lanes(jnp.exp2(m - mx), ML_DK)
            beta = _head_lanes(jnp.exp2(gmaxs[c] - mx), ML_DK)
            m_out[c] = m
            c_out[c] = ct.astype(BF16)
            n_out[c] = n
            ct = alpha * ct + beta * _head_outer(vt[:, rs], kub[rs], ML_HEADS)
            n = alpha * n + beta * jnp.sum(ku[rs], axis=0, keepdims=True)
            m = b[c * cg + last:c * cg + last + 1] + mx
        m_s[d] = m
        c_s[d] = ct
        n_s[d] = n


def _ml_out_kernel(mf_ref, mb_ref, qk_ref, vt_ref, o_ref, gc_ref, gr_ref, cf_ref, cb_ref, nf_ref, nb_ref,
                   tri_ref, gnt_ref, out_ref, *, c0, rows, cg):
    step = pl.program_id(0) + c0
    nsub = rows // cg
    nch = 2 * ML_HEADS
    qk = qk_ref[...]
    vt = vt_ref[...]
    gc = gc_ref[...]
    gr = gr_ref[...]
    ninf = jnp.float32(-jnp.inf)
    chain =lax.broadcasted_iota(jnp.int32, (nch, rows), 0)
    lane = lax.broadcasted_iota(jnp.int32, (nch, rows), 1)
    pos = lane & (cg - 1)
    is_fwd = chain < ML_HEADS

    l3 = jnp.concatenate(_split3(gr[nch:]), axis=0)

    def sum3(bb):
        return bb[:nch] + bb[nch:2 * nch] + bb[2 * nch:]

    b_rows = jnp.where(is_fwd, sum3(_dot_nt(l3, tri_ref[0])), sum3(_dot_nt(l3, tri_ref[1])))
    a_rows = gr[:nch] - b_rows
    g_rows = a_rows
    sh = 1
    while sh < cg:
        xf = jnp.where(pos >= sh, pltpu.roll(g_rows, sh, 1), ninf)
        xb = jnp.where(pos < cg - sh, pltpu.roll(g_rows, rows - sh, 1), ninf)
        g_rows = jnp.maximum(g_rows, jnp.where(is_fwd, xf, xb))
        sh *= 2
    m_rows = jnp.zeros((nch, rows), F32)
    for c in range(nsub):
        in_chunk = (lane >= c * cg) & (lane < (c + 1) * cg)
        for ch in range(nch):
            m_ref = mf_ref if ch < ML_HEADS else mb_ref
            m_rows = jnp.where(in_chunk & (chain == ch), m_ref[step * nsub + c, ch % ML_HEADS], m_rows)
    mt = jnp.maximum(g_rows, m_rows)
    rf = jnp.exp2(g_rows - mt)
    wp = jnp.exp2(m_rows - mt)
    emt = jnp.exp2(-b_rows - mt)

    a_cols = []
    for d in range(2):
        lo = nch + ML_HEADS * d
        a_cols.append(gc[:, ML_HEADS * d:ML_HEADS * (d + 1)] - _cumsum3(tri_ref[d], gc[:, lo:lo + ML_HEADS]))

    ss = lax.broadcasted_iota(jnp.int32, (cg, cg), 0)
    tt = lax.broadcasted_iota(jnp.int32, (cg, cg), 1)
    gnt = gnt_ref[...]
    og = o_ref[...]
    def head_stack(x):
        lane = lax.broadcasted_iota(jnp.int32, x.shape, 1)
        zero = jnp.zeros_like(x)
        return jnp.concatenate([jnp.where((lane >= ML_DK * h) & (lane < ML_DK * (h + 1)), x, zero)
                                for h in range(ML_HEADS)], axis=0)

    for c in range(nsub):
        rs = slice(c * cg, (c + 1) * cg)
        q = qk[rs, :ML_QK]
        st_all = _dot_nt(head_stack(qk[rs, ML_QK:]), q)
        inter_all = [_dot_nt(head_stack(c_ref[c]), q) for c_ref in (cf_ref, cb_ref)]
        nn = [t for x in (nf_ref[c], nb_ref[c]) for t in _split3(x)[:2]]
        dn_all = _dot_nt(head_stack(jnp.concatenate(nn + [jnp.zeros((12, ML_QK), BF16)], axis=0)), q)
        outs = []
        for hh in range(ML_HEADS):
            dv = slice(ML_DV * hh, ML_DV * (hh + 1))
            st = st_all[hh * cg:(hh + 1) * cg]
            dn2 = dn_all[16 * hh:16 * (hh + 1)]
            ht = None
            for d in range(2):
                ch = d * ML_HEADS + hh
                mask = (ss <= tt) if d == 0 else (ss >= tt)
                e = jnp.exp2(jnp.where(mask, a_cols[d][rs, hh:hh + 1] - g_rows[ch:ch + 1, rs], ninf))
                sd = st * e
                den_i = jnp.sum(sd, axis=0, keepdims=True)
                num_t = _dot(vt[dv, rs], sd.astype(BF16))
                inter_t = inter_all[d][dv]
                wpr, rfr = wp[ch:ch + 1, rs], rf[ch:ch + 1, rs]
                den = wpr * (dn2[2 * d:2 * d + 1] + dn2[2 * d + 1:2 * d + 2]) + rfr * den_i
                inv = 1.0 / jnp.maximum(jnp.abs(den), emt[ch:ch + 1, rs])
                part = (wpr * inv) * inter_t + (rfr * inv) * num_t
                ht = part if ht is None else ht + part
            ms = jnp.mean(ht * ht, axis=0, keepdims=True)
            y = (ht * lax.rsqrt(ms + EPS) * gnt[dv]).T
            outs.append(y * jax.nn.sigmoid(og[rs, dv]))
        out_ref[rs, :] = jnp.concatenate(outs, axis=1).astype(BF16)


def _mix_rows(lay):
    return min(256, lay.lc), min(128, lay.lc), min(64, lay.lc)


def _scan_kernel(*refs, rows, ml_cg, hg_cg):
    ml_in, hg_in = refs[0:7], refs[7:14]
    ml_out, hg_out = refs[14:20], refs[20:22]
    ml_scratch, hg_scratch = refs[22:25], refs[25:26]

    @pl.when(pl.program_id(1) == 0)
    def _():
        for s in ml_scratch + hg_scratch:
            s[...] = jnp.zeros_like(s)

    _ml_scan_body(*ml_in, *ml_out, *ml_scratch, rows=rows, cg=ml_cg)
    _hg_scan_body(*hg_in, *hg_out, *hg_scratch, rows=rows, cg=hg_cg)


def _scans(lay, mqk, mvt, gc, hk, hlf, hvt):
    orows, ml_cg, hg_cg = _mix_rows(lay)
    ck = _Chunks(lay, orows)
    ml_sub, hg_sub = orows // ml_cg, orows // hg_cg
    ml_nt, hg_nt = lay.n // ml_cg, lay.n // hg_cg

    def spec(cols, fn, colblk=0):
        return pl.BlockSpec((orows, cols), lambda b, j: (fn(b, j), colblk))

    def tspec(rows_, fn):
        return pl.BlockSpec((rows_, orows), lambda b, j: (0, fn(b, j)))

    def st(shape, fn):
        nd = len(shape)
        return pl.BlockSpec(shape, lambda b, j: (fn(b, j),) + (0,) * (nd - 1))

    def ml_states(fn):
        return [st((ml_sub, ML_DV, ML_QK), fn), st((ml_sub, 1, ML_QK), fn), st((ml_sub, 1, ML_HEADS), fn)]

    ml_shapes = [jax.ShapeDtypeStruct((ml_nt, ML_DV, ML_QK), BF16),
                 jax.ShapeDtypeStruct((ml_nt, 1, ML_QK), F32),
                 jax.ShapeDtypeStruct((ml_nt, 1, ML_HEADS), F32)]
    hg_shape = jax.ShapeDtypeStruct((hg_nt, HG_DV, HG_K), BF16)
    tri_spec = pl.BlockSpec((2, orows, orows), lambda b, j: (0, 0, 0))
    res = pl.pallas_call(
        functools.partial(_scan_kernel, rows=orows, ml_cg=ml_cg, hg_cg=hg_cg),
        grid=(lay.batch, ck.steps),
        in_specs=[spec(ML_QK, ck.fwd, 1), tspec(ML_V, ck.fwd), spec(N_GATE, ck.fwd),
                  spec(ML_QK, ck.bwd, 1), tspec(ML_V, ck.bwd), spec(N_GATE, ck.bwd), tri_spec,
                  spec(HG_K, ck.fwd, 0), spec(HG_K, ck.fwd, 0), tspec(HG_V, ck.fwd),
                  spec(HG_K, ck.bwd, 1), spec(HG_K, ck.bwd, 1), tspec(HG_V, ck.bwd), tri_spec],
        out_specs=ml_states(ck.fwd) + ml_states(ck.bwd)
        + [st((hg_sub, HG_DV, HG_K), ck.fwd), st((hg_sub, HG_DV, HG_K), ck.bwd)],
        out_shape=ml_shapes + ml_shapes + [hg_shape, hg_shape],
        scratch_shapes=[pltpu.VMEM((2, ML_DV, ML_QK), F32),
                        pltpu.VMEM((2, 1, ML_QK), F32),
                        pltpu.VMEM((2, 1, ML_HEADS), F32),
                        pltpu.VMEM((2, HG_DV, HG_K), F32)],
        compiler_params=_params(("parallel", "arbitrary")),
        name="scan",
    )(mqk, mvt, gc, mqk, mvt, gc, _chunk_tri(orows, ml_cg),
      hk, hlf, hvt, hk, hlf, hvt, _chunk_tri(orows, hg_cg))
    return res[:6], res[6:]


def _mlstm_out(lay, states, mqk, mvt, mo, gc, gr, gnorm, with_ctx):
    cf, nf, mf, cb, nb, mb = states
    orows, cg, _ = _mix_rows(lay)
    nsub = orows // cg
    tri = _chunk_tri(orows, cg)
    nt = lay.n // cg
    c0 = 0 if with_ctx else lay.nc // orows
    nout = lay.n // orows - c0
    smem = pl.BlockSpec(memory_space=pltpu.SMEM)

    def rows(cols):
        return pl.BlockSpec((orows, cols), lambda c: (c + c0, 0))

    cst = pl.BlockSpec((nsub, ML_DV, ML_QK), lambda c: (c + c0, 0, 0))
    nst = pl.BlockSpec((nsub, 1, ML_QK), lambda c: (c + c0, 0, 0))
    out = pl.pallas_call(
        functools.partial(_ml_out_kernel, c0=c0, rows=orows, cg=cg),
        grid=(nout,),
        in_specs=[smem, smem, rows(2 * ML_QK), pl.BlockSpec((ML_V, orows), lambda c: (0, c + c0)),
                  rows(ML_V), rows(N_GATE),
                  pl.BlockSpec((N_GATE, orows), lambda c: (0, c + c0)),
                  cst, cst, nst, nst,
                  pl.BlockSpec((2, orows, orows), lambda c: (0, 0, 0)),
                  pl.BlockSpec((ML_V, 1), lambda c: (0, 0))],
        out_specs=pl.BlockSpec((orows, ML_V), lambda c: (c, 0)),
        out_shape=jax.ShapeDtypeStruct((nout * orows, ML_V), BF16),
        compiler_params=_params(("parallel",)),
        name="mlstm_out",
    )(mf.reshape(nt, ML_HEADS), mb.reshape(nt, ML_HEADS), mqk, mvt, mo, gc, gr, cf, cb, nf, nb, tri,
      gnorm.reshape(ML_V, 1))
    return out


def _chunk_tri(rows, cg):
    t = np.arange(rows)[:, None]
    u = np.arange(rows)[None, :]
    same = (t // cg) == (u // cg)
    return jnp.asarray(np.stack([same & (u <= t), same & (u >= t)]).astype(np.float32)).astype(BF16)


def _cumsum3(tri, x):
    w = x.shape[1]
    hi = x.astype(BF16)
    r1 = x - hi.astype(F32)
    mid = r1.astype(BF16)
    lo = (r1 - mid.astype(F32)).astype(BF16)
    bb = _dot(tri, jnp.concatenate([hi, mid, lo], axis=1))
    return bb[:, :w] + bb[:, w:2 * w] + bb[:, 2 * w:]


def _level_ref(b, bs, d):
    rows, w = b.shape
    off = bs // 2 - 1 + d
    if bs >= 8:
        pieces = [jnp.broadcast_to(b[s + off:s + off + 1], (bs, w)) for s in range(0, rows, bs)]
        return jnp.concatenate(pieces, axis=0)
    b8 = b.reshape(rows // 8, 8, w)
    u = lax.broadcasted_iota(jnp.int32, (1, 8, 1), 1)
    out = None
    for s in range(0, 8, bs):
        piece = jnp.broadcast_to(b8[:, s + off:s + off + 1, :], b8.shape)
        out = piece if out is None else jnp.where(u >= s, piece, out)
    return out.reshape(rows, w)


def _hg_scan_body(kf_ref, lff_ref, vf_ref, kb_ref, lfb_ref, vb_ref, tri_ref, sf_ref, sb_ref, s_s, *, rows, cg):
    nsub = rows // cg
    streams = ((kf_ref, lff_ref, vf_ref, sf_ref), (kb_ref, lfb_ref, vb_ref, sb_ref))
    for d, (k_ref, lf_ref, v_ref, s_out) in enumerate(streams):
        b = _cumsum3(tri_ref[d], lf_ref[...])
        last = cg - 1 if d == 0 else 0
        bls = [b[c * cg + last:c * cg + last + 1] for c in range(nsub)]
        bl_rows = jnp.concatenate([jnp.broadcast_to(bl, (cg, HG_K)) for bl in bls], axis=0)
        kd = (k_ref[...] * jnp.exp2(bl_rows - b)).astype(BF16)
        vt = v_ref[...]
        s = s_s[d]
        for c in (range(nsub) if d == 0 else reversed(range(nsub))):
            rs = slice(c * cg, (c + 1) * cg)
            s_out[c] = s.astype(BF16)
            s = jnp.exp2(bls[c]) * s + _head_outer(vt[:, rs], kd[rs], HG_HEADS)
        s_s[d] = s


def _hg_out_kernel(q_ref, k_ref, lf_ref, v_ref, g_ref, sf_ref, sb_ref, tri_ref, gn_ref, out_ref,
                   *, rows, cg, nlev):
    q = q_ref[...]
    kk = k_ref[...]
    lff = lf_ref[...]
    v = v_ref[...]
    nsub = rows // cg
    qb = q.astype(BF16)
    t_i = lax.broadcasted_iota(jnp.int32, (cg, cg), 0)
    s_i = lax.broadcasted_iota(jnp.int32, (cg, cg), 1)
    xr = t_i ^ s_i
    level = jnp.zeros((cg, cg), jnp.int32)
    for l in range(nlev):
        level = level + (xr >= 2 ** l).astype(jnp.int32)
    assert 2 * HG_DK == 128 and HG_HEADS % 2 == 0
    npair = HG_HEADS // 2
    pairs = [slice(128 * p, 128 * (p + 1)) for p in range(npair)]
    bs, levs = [], []
    for d in range(2):
        bs.append(_cumsum3(tri_ref[d], lff[:, HG_K * d:HG_K * (d + 1)]))
        lev_d = jnp.where((t_i > s_i) if d == 0 else (t_i < s_i), level, -1)
        lev_d = jnp.where(t_i == s_i, 0, lev_d)
        levs.append(jnp.concatenate([lev_d, lev_d], axis=1))
    gn = gn_ref[...]
    gate = g_ref[...]
    lo = lax.broadcasted_iota(jnp.int32, (cg, 128), 1) < HG_DK
    zk = jnp.zeros((cg, 128), BF16)

    def split_heads(x):
        return jnp.concatenate([jnp.where(lo, x, zk), jnp.where(lo, zk, x)], axis=0)

    for c in range(nsub):
        rs = slice(c * cg, (c + 1) * cg)
        qc = q[rs]
        amat = [None] * npair
        inter = [None] * npair
        for d in range(2):
            kc = kk[rs, HG_K * d:HG_K * (d + 1)]
            bc = bs[d][rs]
            a = [jnp.zeros((cg, 2 * cg), F32)] * npair
            for l in range(nlev + 1):
                if l == 0:
                    qt, kt = qb[rs], kc.astype(BF16)
                else:
                    e = jnp.exp2(-jnp.abs(bc - _level_ref(bc, 2 ** l, d)))
                    qt, kt = (qc * e).astype(BF16), (kc * e).astype(BF16)
                a = [jnp.where(levs[d] == l, _dot_nt(qt[:, pr], split_heads(kt[:, pr])), a[p])
                     for p, pr in enumerate(pairs)]
            qe = (qc * jnp.exp2(bc)).astype(BF16)
            st = (sf_ref if d == 0 else sb_ref)[c]
            it = []
            for pr in pairs:
                qs = split_heads(qe[:, pr])
                r2 = _dot_nt(qs, st[:, pr])
                it.append(jnp.concatenate([r2[:cg], r2[cg:]], axis=1))
            amat = a if d == 0 else [x + y for x, y in zip(amat, a)]
            inter = it if d == 0 else [x + y for x, y in zip(inter, it)]
        outs = []
        for p in range(npair):
            vp = v[rs, 2 * HG_DV * p:2 * HG_DV * (p + 1)]
            zv = jnp.zeros((cg, HG_DV), BF16)
            vbd = jnp.concatenate([jnp.concatenate([vp[:, :HG_DV], zv], axis=1),
                                   jnp.concatenate([zv, vp[:, HG_DV:]], axis=1)], axis=0)
            o = inter[p] + _dot(amat[p].astype(BF16), vbd)
            for hh in (2 * p, 2 * p + 1):
                sl = slice(HG_DV * hh, HG_DV * (hh + 1))
                oh = o[:, HG_DV * (hh - 2 * p):HG_DV * (hh - 2 * p + 1)]
                outs.append(_rms(oh, gn[:, sl]) * _silu(gate[rs, sl]))
        out_ref[rs, :] = jnp.concatenate(outs, axis=1).astype(BF16)


def _hgrn2_out(lay, states, hq, hk, hlf, hv, hgate, gnorm, with_ctx):
    sf, sb = states
    orows, _, cg = _mix_rows(lay)
    nsub = orows // cg
    nlev = int(np.log2(cg))
    tri = _chunk_tri(orows, cg)
    c0 = 0 if with_ctx else lay.nc // orows
    nout = lay.n // orows - c0

    def rows(cols):
        return pl.BlockSpec((orows, cols), lambda c: (c + c0, 0))

    sst = pl.BlockSpec((nsub, HG_DV, HG_K), lambda c: (c + c0, 0, 0))
    out = pl.pallas_call(
        functools.partial(_hg_out_kernel, rows=orows, cg=cg, nlev=nlev),
        grid=(nout,),
        in_specs=[rows(HG_K), rows(2 * HG_K), rows(2 * HG_K), rows(HG_V), rows(HG_V), sst, sst,
                  pl.BlockSpec((2, orows, orows), lambda c: (0, 0, 0)),
                  pl.BlockSpec((1, HG_V), lambda c: (0, 0))],
        out_specs=pl.BlockSpec((orows, HG_V), lambda c: (c, 0)),
        out_shape=jax.ShapeDtypeStruct((nout * orows, HG_V), BF16),
        compiler_params=_params(("parallel",)),
        name="hgrn2_out",
    )(hq, hk, hlf, hv, hgate, sf, sb, tri, gnorm)
    return out


def _rope_table(lay):
    l = lay.l
    rows = l // GRID_W
    row = jnp.repeat(jnp.arange(rows, dtype=F32), GRID_W)
    col = jnp.tile(jnp.arange(GRID_W, dtype=F32), rows)
    n_freq = ATT_HEAD_DIM // 4
    inv = ROPE_BASE ** (-jnp.arange(n_freq, dtype=F32) / n_freq)
    ar, ac = row[:, None] * inv, col[:, None] * inv
    cos = jnp.concatenate([jnp.cos(ar), jnp.cos(ar), jnp.cos(ac), jnp.cos(ac)], axis=1)
    sin = jnp.concatenate([-jnp.sin(ar), jnp.sin(ar), -jnp.sin(ac), jnp.sin(ac)], axis=1)
    lat = jnp.concatenate([cos, cos, sin, sin], axis=1)
    lat = jnp.tile(lat, (lay.batch, 1))
    ctx = jnp.concatenate([jnp.ones((lay.nc, 128), F32), jnp.zeros((lay.nc, 128), F32)], axis=1)
    return jnp.concatenate([ctx, lat], axis=0)


def kernel(x, c, ctx, c_ctx, w_ada, b_ada, norm_pre, norm_post, ffn_w1, ffn_w3, ffn_w2, w_in, att_sink,
           ml_conv, ml_f_bias, ml_norm, hg_lb_logits, hg_norm, w_branch_att, w_branch_ml, w_branch_hg,
           w_out):
    batch, l, d = x.shape
    lc = ctx.shape[1]
    depth = w_ada.shape[0]
    assert l % GRID_W == 0 and l & (l - 1) == 0 and lc & (lc - 1) == 0
    assert l % 256 == 0 and lc % 128 == 0 and batch + 1 <= MOD_ROWS
    lay = _Layout(batch, lc, l)
    lay_ffn = _Layout(batch, lc, l, FFN_TM)

    xs = (ctx.reshape(batch * lc, d), x.reshape(batch * l, d))
    cs = jnp.concatenate([c, c_ctx[None, :], jnp.zeros((MOD_ROWS - batch - 1, d), F32)], axis=0)
    mod_all = _modulation(cs, w_ada, b_ada).reshape(depth, MOD_ROWS, N_MOD, d)
    rope = _rope_table(lay)

    lb_all = jnp.cumsum(jax.nn.softmax(hg_lb_logits.astype(F32), axis=0), axis=0)
    lb_all = lb_all - lb_all[0:1]

    sizes = (ATT_Q, ATT_KV, ATT_KV, ML_QK, ML_QK, ML_V, ML_V, 2 * ML_HEADS, 2 * ML_HEADS,
             HG_K, HG_K, HG_K, HG_V, HG_V, 3 * d)
    offs = np.concatenate([[0], np.cumsum(sizes)])
    g0, g1 = int(offs[7]), int(offs[9])

    for layer in range(depth):
        last = layer == depth - 1
        mod = mod_all[layer]
        wl = w_in[layer]
        w_gate = wl[:, g0:g1]
        w_proj = jnp.concatenate(
            [wl[:, :g0], wl[:, g1:int(offs[14])], w_gate, jnp.zeros((d, GATE_PAD - N_GATE), F32)],
            axis=1).astype(BF16)
        w_gate_t = jnp.concatenate([wl[:, int(offs[5]):int(offs[6])], wl[:, int(offs[12]):int(offs[13])],
                                    w_gate], axis=1).T.astype(BF16)
        w_brg = wl[:, int(offs[14]):].astype(BF16)
        fb = ml_f_bias[layer].reshape(1, 2 * ML_HEADS)
        fbrow = jnp.concatenate([jnp.zeros((1, 2 * ML_HEADS), F32), fb], axis=1)
        fbcol = fbrow.reshape(N_GATE, 1)
        lb = lb_all[layer]
        lbv = jnp.stack([jnp.tile(jnp.log(lb), 2), jnp.tile(jnp.log1p(-lb), 2), jnp.tile(1.0 - lb, 2)])

        def npre(i):
            return norm_pre[layer, i].reshape(1, d)

        def npost(i):
            return norm_post[layer, i].reshape(1, d)

        def ffn(xin, j, i, latent_only=False):
            return _ffn(lay_ffn, xin, mod, j, npre(j), npost(j), ffn_w1[layer, i].astype(BF16),
                        ffn_w3[layer, i].astype(BF16), ffn_w2[layer, i].astype(BF16), latent_only)

        xs = ffn(xs, 0, 0)
        (aq, akv, mqk, mo, gc, gr, hq, hk, hlf, hv, hgate, mvt, hvt) = _project(
            lay, xs, mod, npre(1), w_proj, w_gate_t, rope, ml_conv[layer], fbrow, fbcol, lbv)
        att = _attention(lay, aq, akv, att_sink[layer], not last)
        ml_states, hg_states = _scans(lay, mqk, mvt, gc, hk, hlf, hvt)
        ml = _mlstm_out(lay, ml_states, mqk, mvt, mo, gc, gr, ml_norm[layer], not last)
        hg = _hgrn2_out(lay, hg_states, hq, hk, hlf, hv, hgate, hg_norm[layer].reshape(1, HG_V), not last)
        xs = _merge(lay_ffn, xs, att, ml, hg, mod, npre(1), npost(1), w_brg,
                    w_branch_att[layer].astype(BF16), w_branch_ml[layer].astype(BF16),
                    w_branch_hg[layer].astype(BF16), w_out[layer].astype(BF16), last)
        xs = ffn(xs, 2, 1, last)
    return xs.reshape(batch, l, d)
```

```python
import functools

import numpy as np
import jax
import jax.numpy as jnp
from jax import lax
from jax.experimental import pallas as pl
from jax.experimental.pallas import tpu as pltpu

F32 = jnp.float32
BF16 = jnp.bfloat16

ATT_HEADS = 8
ATT_KV_HEADS = 2
ATT_GROUP = ATT_HEADS // ATT_KV_HEADS
ATT_HEAD_DIM = 64
ATT_BLOCK = 128
ML_HEADS = 4
ML_DK = 64
ML_DV = 128
HG_HEADS = 4
HG_DK = 64
HG_DV = 128
GRID_W = 64
ROPE_BASE = 10000.0
EPS = 1e-6
N_MOD = 9
MOD_ROWS = 16

ATT_Q = ATT_HEADS * ATT_HEAD_DIM
ATT_KV = ATT_KV_HEADS * ATT_HEAD_DIM
ML_QK = ML_HEADS * ML_DK
ML_V = ML_HEADS * ML_DV
HG_K = HG_HEADS * HG_DK
HG_V = HG_HEADS * HG_DV
N_GATE = 4 * ML_HEADS
GATE_PAD = 128

OFF_ATT = 0
OFF_MQK = OFF_ATT + ATT_Q + 2 * ATT_KV
OFF_MVO = OFF_MQK + 2 * ML_QK
OFF_HG = OFF_MVO + 2 * ML_V
OFF_GATE = OFF_HG + 3 * HG_K + 2 * HG_V
W_PROJ_COLS = OFF_GATE + GATE_PAD

LOG2E = 1.4426950408889634
VMEM_LIMIT = 56 * 1024 * 1024
FFN_TM = 1024
FFN_SPLIT = 11


def _dot(a, b):
    return jnp.dot(a, b, preferred_element_type=F32)


def _dot_nt(a, b):
    return lax.dot_general(a, b, (((1,), (1,)), ((), ())), preferred_element_type=F32)


def _dot_tn(a, b):
    return lax.dot_general(a, b, (((0,), (0,)), ((), ())), preferred_element_type=F32)


def _rms(x, g):
    ms = jnp.mean(x * x, axis=-1, keepdims=True)
    return x * lax.rsqrt(ms + EPS) * g


def _silu(x):
    return x * jax.nn.sigmoid(x)


def _log_sigmoid(x):
    return jnp.minimum(x, 0.0) - jnp.log(1.0 + jnp.exp(-jnp.abs(x)))


def _head_lanes(x, w):
    r, h = x.shape
    return jnp.concatenate([jnp.broadcast_to(x[:, i:i + 1], (r, w)) for i in range(h)], axis=1)


def _split3(x):
    hi = x.astype(BF16)
    r1 = x - hi.astype(F32)
    mid = r1.astype(BF16)
    return hi, mid, (r1 - mid.astype(F32)).astype(BF16)


def _head_outer(vt, k, heads):
    dv, dk = vt.shape[0] // heads, k.shape[1] // heads
    assert heads % 2 == 0 and 2 * dk == 128
    lo = lax.broadcasted_iota(jnp.int32, (dv, 2 * dk), 1) < dk
    outs = []
    for p in range(heads // 2):
        big = _dot(vt[2 * p * dv:(2 * p + 2) * dv], k[:, 2 * p * dk:(2 * p + 2) * dk])
        outs.append(jnp.where(lo, big[:dv], big[dv:]))
    return jnp.concatenate(outs, axis=1)


def _head_lanes_mxu(x, w):
    h = x.shape[1]
    row = lax.broadcasted_iota(jnp.int32, (3 * h, h * w), 0)
    lane = lax.broadcasted_iota(jnp.int32, (3 * h, h * w), 1)
    sel = jnp.zeros((3 * h, h * w), jnp.bool_)
    for t in range(3):
        for i in range(h):
            sel = sel | ((row == t * h + i) & (lane >= i * w) & (lane < (i + 1) * w))
    return _dot(jnp.concatenate(_split3(x), axis=1), jnp.where(sel, 1.0, 0.0).astype(BF16))


def _params(sem):
    return pltpu.CompilerParams(dimension_semantics=sem, vmem_limit_bytes=VMEM_LIMIT)


def _resident(shape):
    nd = len(shape)
    return pl.BlockSpec(shape, lambda *_: (0,) * nd, pipeline_mode=pl.Buffered(1))


def _mod_kernel(c_ref, w_ref, b_ref, o_ref):
    s = _silu(c_ref[...]).astype(BF16)
    o_ref[0] = _dot(s, w_ref[0].astype(BF16)) + b_ref[0]


def _modulation(cs, w_ada, b_ada):
    depth, d, nm = w_ada.shape
    tn = 1024
    return pl.pallas_call(
        _mod_kernel,
        grid=(depth, nm // tn),
        in_specs=[
            pl.BlockSpec((MOD_ROWS, d), lambda l, n: (0, 0)),
            pl.BlockSpec((1, d, tn), lambda l, n: (l, 0, n)),
            pl.BlockSpec((1, 1, tn), lambda l, n: (l, 0, n)),
        ],
        out_specs=pl.BlockSpec((1, MOD_ROWS, tn), lambda l, n: (l, 0, n)),
        out_shape=jax.ShapeDtypeStruct((depth, MOD_ROWS, nm), F32),
        compiler_params=_params(("parallel", "parallel")),
        name="modulation",
    )(cs, w_ada, b_ada.reshape(depth, 1, nm))


class _Layout:
    def __init__(self, batch, lc, l, tm=512):
        self.batch, self.lc, self.l = batch, lc, l
        self.nc = batch * lc
        self.n = self.nc + batch * l
        while self.nc % tm or l % tm:
            tm //= 2
        self.tm = tm
        self.nct = self.nc // tm
        self.tpb = l // tm
        self.ntiles = self.n // tm

    def mod_index(self, t):
        return jnp.where(t < self.nct, self.batch, (t - self.nct) // self.tpb)


def _ffn_split_kernel(c_ref, x_ref, *rest, nct, **kw):
    _ffn_body(jnp.where(pl.program_id(0) < nct, c_ref[...], x_ref[...]), *rest, **kw)


def _ffn_kernel(x_ref, *rest, **kw):
    _ffn_body(x_ref[...], *rest, **kw)


def _ffn_body(x, mod_ref, gpre_ref, gpost_ref, w1_ref, w3_ref, w2_ref, o_ref, *, j, halves):
    mod = mod_ref[0]
    shift, scale, gate = mod[3 * j:3 * j + 1], mod[3 * j + 1:3 * j + 2], mod[3 * j + 2:3 * j + 3]
    h = (_rms(x, gpre_ref[...] * (1.0 + scale)) + shift).astype(BF16)
    dff = w1_ref.shape[1]
    step = dff // halves
    y = None
    for c in range(halves):
        sl = slice(c * step, (c + 1) * step)
        a = _dot(h, w1_ref[:, sl])
        b = _dot(h, w3_ref[:, sl])
        part = _dot((_silu(a) * b).astype(BF16), w2_ref[sl, :])
        y = part if y is None else y + part
    o_ref[...] = x + 0.5 * gate * _rms(y, gpost_ref[...])


def _ffn(lay, xin, mod, j, gpre, gpost, w1, w3, w2, latent_only=False):
    d, dff = w1.shape
    tm = lay.tm
    halves = FFN_SPLIT if (dff // FFN_SPLIT) % 128 == 0 else 1
    tile0 = lay.nct if latent_only else 0
    ntiles = lay.ntiles - tile0
    row = pl.BlockSpec((tm, d), lambda t: (t, 0))
    if isinstance(xin, tuple):
        body = functools.partial(_ffn_split_kernel, nct=lay.nct, j=j, halves=halves)
        xs_specs = [pl.BlockSpec((tm, d), lambda t: (jnp.minimum(t, lay.nct - 1), 0)),
                    pl.BlockSpec((tm, d), lambda t: (jnp.maximum(t - lay.nct, 0), 0))]
    else:
        body = functools.partial(_ffn_kernel, j=j, halves=halves)
        xs_specs, xin = [row], (xin,)
    return pl.pallas_call(
        body,
        grid=(ntiles,),
        in_specs=xs_specs + [
            pl.BlockSpec((1, N_MOD, d), lambda t: (lay.mod_index(t + tile0), 0, 0)),
            _resident((1, d)), _resident((1, d)),
            _resident((d, dff)), _resident((d, dff)), _resident((dff, d)),
        ],
        out_specs=row,
        out_shape=jax.ShapeDtypeStruct((ntiles * tm, d), F32),
        compiler_params=_params(("parallel",)),
        name="ffn",
    )(*xin, mod, gpre, gpost, w1, w3, w2)


def _proj_kernel(x_ref, xp_ref, xn_ref, mod_ref, gpre_ref, w_ref, wgt_ref, rope_ref, conv_ref,
                 fbrow_ref, fbcol_ref, lbv_ref,
                 aq_ref, akv_ref, mqk_ref, mo_ref, gc_ref, gr_ref,
                 hq_ref, hk_ref, hlf_ref, hv_ref, hg_ref, mvt_ref, hvt_ref, *, nct, lc, l, tm):
    t = pl.program_id(0)
    mod = mod_ref[0]
    shift, scale = mod[3:4], mod[4:5]
    gpre = gpre_ref[...]

    gain = gpre * (1.0 + scale)

    def pre(xx):
        return (_rms(xx, gain) + shift).astype(BF16)

    h = pre(x_ref[...])

    pa = _dot(h, w_ref[:, OFF_ATT:OFF_MQK])
    rope = rope_ref[...]
    cos, sin = rope[:, :128], rope[:, 128:]

    def swap(v):
        w = v.shape[1]
        lane = lax.broadcasted_iota(jnp.int32, v.shape, 1)
        return jnp.where((lane & 16) == 0, pltpu.roll(v, w - 16, 1), pltpu.roll(v, 16, 1))

    q = pa[:, :ATT_Q]
    k = pa[:, ATT_Q:ATT_Q + ATT_KV]
    cos4 = jnp.concatenate([cos] * (ATT_Q // 128), axis=1)
    sin4 = jnp.concatenate([sin] * (ATT_Q // 128), axis=1)
    q = (q * cos4 + swap(q) * sin4) * (ATT_HEAD_DIM ** -0.5 * LOG2E)
    k = k * cos + swap(k) * sin
    aq_ref[...] = q.astype(BF16)
    akv_ref[...] = jnp.concatenate([k, pa[:, ATT_Q + ATT_KV:]], axis=1).astype(BF16)

    wm = w_ref[:, OFF_MQK:OFF_MVO]
    pm = _dot(h, wm)
    pprev = _dot(pre(xp_ref[...]), wm)[7:8]
    pnext = _dot(pre(xn_ref[...]), wm)[0:1]
    r = lax.broadcasted_iota(jnp.int32, (tm, 1), 0)
    grow = t * tm + r
    is_ctx = t < nct
    pos = jnp.where(is_ctx, grow & (lc - 1), (grow - nct * tm) & (l - 1))
    last = jnp.where(is_ctx, lc - 1, l - 1)
    dn = jnp.where(r == 0, pprev, pltpu.roll(pm, 1, 0))
    dn = jnp.where(pos == 0, 0.0, dn)
    up = jnp.where(r == tm - 1, pnext, pltpu.roll(pm, tm - 1, 0))
    up = jnp.where(pos == last, 0.0, up)
    cw = conv_ref[...]
    cv = dn * cw[0:1] + pm * cw[1:2] + up * cw[2:3]
    lane = lax.broadcasted_iota(jnp.int32, (1, 2 * ML_QK), 1)
    mqk_ref[...] = (_silu(cv) * jnp.where(lane < ML_QK, 1.0, ML_DK ** -0.5)).astype(BF16)

    mo_ref[...] = _dot(h, w_ref[:, OFF_MVO + ML_V:OFF_HG])

    pg = _dot(h, w_ref[:, OFF_GATE:OFF_GATE + GATE_PAD])[:, :N_GATE] + fbrow_ref[...]
    lane16 = lax.broadcasted_iota(jnp.int32, (1, N_GATE), 1)
    gc_ref[...] = jnp.where(lane16 < 2 * ML_HEADS, pg, _log_sigmoid(pg)) * LOG2E
    pt = _dot_nt(wgt_ref[...], h)
    mvt_ref[...] = pt[:ML_V].astype(BF16)
    hvt_ref[...] = pt[ML_V:ML_V + HG_V].astype(BF16)
    pgr = pt[ML_V + HG_V:] + fbcol_ref[...]
    row16 = lax.broadcasted_iota(jnp.int32, (N_GATE, 1), 0)
    gr_ref[...] = jnp.where(row16 < 2 * ML_HEADS, pgr, _log_sigmoid(pgr)) * LOG2E

    ph = _dot(h, w_ref[:, OFF_HG:OFF_GATE])
    hq_ref[...] = _silu(ph[:, :HG_K]) * (HG_DK ** -0.5)
    z = ph[:, HG_K:3 * HG_K]
    lbv = lbv_ref[...]
    log_lb, log_1m_lb, one_m_lb = lbv[0:1], lbv[1:2], lbv[2:3]
    ls = _log_sigmoid(z)
    bv = log_1m_lb + ls
    hlf_ref[...] = (jnp.maximum(log_lb, bv) + jnp.log(1.0 + jnp.exp(-jnp.abs(log_lb - bv)))) * LOG2E
    hk_ref[...] = one_m_lb * jnp.exp(ls - z)
    hv_ref[...] = ph[:, 3 * HG_K:3 * HG_K + HG_V].astype(BF16)
    hg_ref[...] = ph[:, 3 * HG_K + HG_V:]


def _project(lay, x, mod, gpre, w, wgt, rope, conv, fbrow, fbcol, lbv):
    n, d = x.shape
    tm = lay.tm
    nb8 = n // 8

    def row(c):
        return pl.BlockSpec((tm, c), lambda t: (t, 0))

    out_cols = [(ATT_Q, BF16), (2 * ATT_KV, BF16), (2 * ML_QK, BF16), (ML_V, F32),
                (N_GATE, F32), (N_GATE, F32, None), (HG_K, F32), (2 * HG_K, F32), (2 * HG_K, F32),
                (HG_V, BF16), (HG_V, F32), (ML_V, BF16, None), (HG_V, BF16, None)]
    out_specs, out_shape = [], []
    for oc in out_cols:
        if len(oc) == 3:
            out_specs.append(pl.BlockSpec((oc[0], tm), lambda t: (0, t)))
            out_shape.append(jax.ShapeDtypeStruct((oc[0], n), oc[1]))
        else:
            out_specs.append(row(oc[0]))
            out_shape.append(jax.ShapeDtypeStruct((n, oc[0]), oc[1]))
    return pl.pallas_call(
        functools.partial(_proj_kernel, nct=lay.nct, lc=lay.lc, l=lay.l, tm=tm),
        grid=(lay.ntiles,),
        in_specs=[
            row(d),
            pl.BlockSpec((8, d), lambda t: (jnp.maximum(t * (tm // 8) - 1, 0), 0)),
            pl.BlockSpec((8, d), lambda t: (jnp.minimum((t + 1) * (tm // 8), nb8 - 1), 0)),
            pl.BlockSpec((1, N_MOD, d), lambda t: (lay.mod_index(t), 0, 0)),
            _resident((1, d)),
            _resident(w.shape), _resident(wgt.shape),
            row(256),
            _resident(conv.shape), _resident(fbrow.shape), _resident(fbcol.shape), _resident(lbv.shape),
        ],
        out_specs=out_specs,
        out_shape=out_shape,
        compiler_params=_params(("parallel",)),
        name="project",
    )(x, x, x, mod, gpre, w, wgt, rope, conv, fbrow, fbcol, lbv)


def _merge_kernel(x_ref, att_ref, ml_ref, hg_ref, mod_ref, gpre_ref, gpost_ref,
                  wg_ref, wa_ref, wm_ref, wh_ref, wo_ref, o_ref):
    x = x_ref[...]
    d = x.shape[1]
    mod = mod_ref[0]
    shift, scale, gate = mod[3:4], mod[4:5], mod[5:6]
    h = (_rms(x, gpre_ref[...] * (1.0 + scale)) + shift).astype(BF16)
    y = None
    for i, (b_ref, w_ref) in enumerate(((att_ref, wa_ref), (ml_ref, wm_ref), (hg_ref, wh_ref))):
        g = jax.nn.sigmoid(_dot(h, wg_ref[:, i * d:(i + 1) * d]))
        part = g * _dot(b_ref[...], w_ref[...])
        y = part if y is None else y + part
    yy = _dot(y.astype(BF16), wo_ref[...])
    o_ref[...] = x + gate * _rms(yy, gpost_ref[...])


def _merge(lay, x, att, ml, hg, mod, gpre, gpost, wg, wa, wm, wh, wo, latent_only=False):
    d = x.shape[1]
    tm = lay.tm
    tile0 = lay.nct if latent_only else 0
    ntiles = lay.ntiles - tile0

    def row(c, off=0):
        return pl.BlockSpec((tm, c), lambda t: (t + off, 0))

    return pl.pallas_call(
        _merge_kernel,
        grid=(ntiles,),
        in_specs=[
            row(d, tile0), row(ATT_Q), row(ML_V), row(HG_V),
            pl.BlockSpec((1, N_MOD, d), lambda t: (lay.mod_index(t + tile0), 0, 0)),
            _resident((1, d)), _resident((1, d)),
            _resident(wg.shape), _resident(wa.shape), _resident(wm.shape), _resident(wh.shape),
            _resident(wo.shape),
        ],
        out_specs=row(d),
        out_shape=jax.ShapeDtypeStruct((ntiles * tm, d), F32),
        compiler_params=_params(("parallel",)),
        name="merge",
    )(x, att, ml, hg, mod, gpre, gpost, wg, wa, wm, wh, wo)


def _swap_lane_halves(x):
    return pltpu.bitcast(pltpu.roll(pltpu.bitcast(x, jnp.uint32), 64, 1), BF16)


def _attend(q, kv, bias, sink_ref):
    tq = q.shape[0]
    assert ATT_GROUP == 4 and 2 * ATT_HEAD_DIM == 128 and ATT_KV == 128
    k, v = kv[:, :ATT_KV], kv[:, ATT_KV:]
    ks, vs = _swap_lane_halves(k), _swap_lane_halves(v)
    lo_k = lax.broadcasted_iota(jnp.int32, k.shape, 1) < ATT_HEAD_DIM
    lo_q = lax.broadcasted_iota(jnp.int32, (tq, 128), 1) < ATT_HEAD_DIM
    row = lax.broadcasted_iota(jnp.int32, (4 * tq, 1), 0)
    zero = jnp.zeros_like(k)
    zq = jnp.zeros((tq, 128), BF16)
    bias4 = None if bias is None else jnp.concatenate([bias] * 4, axis=0)
    outs = []
    for g in range(ATT_KV_HEADS):
        own, other = (k, ks) if g == 0 else (ks, k)
        k2 = jnp.where(lo_k, own, other)
        vown, voth = (v, vs) if g == 0 else (vs, v)
        va = jnp.where(lo_k, vown, zero)
        vb = jnp.where(lo_k, zero, voth)
        p0 = q[:, 256 * g:256 * g + 128]
        p1 = q[:, 256 * g + 128:256 * g + 256]
        qg = jnp.concatenate([jnp.where(lo_q, p0, zq), jnp.where(lo_q, p1, zq),
                              jnp.where(lo_q, zq, p0), jnp.where(lo_q, zq, p1)], axis=0)
        s = _dot_nt(qg, k2)
        if bias4 is not None:
            nbk = bias4.shape[1]
            s = jnp.concatenate([s[:, :nbk] + bias4, s[:, nbk:]], axis=1)
        h0 = 4 * g
        sk = jnp.where(row < tq, sink_ref[h0],
                       jnp.where(row < 2 * tq, sink_ref[h0 + 2],
                                 jnp.where(row < 3 * tq, sink_ref[h0 + 1], sink_ref[h0 + 3]))) * LOG2E
        m = jnp.maximum(jnp.max(s, axis=1, keepdims=True), sk)
        p = jnp.exp2(s - m)
        inv = 1.0 / (jnp.sum(p, axis=1, keepdims=True) + jnp.exp2(sk - m))
        pb = p.astype(BF16)
        o = _dot(pb[:2 * tq], va) * inv[:2 * tq] + _dot(pb[2 * tq:], vb) * inv[2 * tq:]
        outs += [o[:tq], o[tq:]]
    return jnp.concatenate(outs, axis=1).astype(BF16)


def _attn_kernel(sink_ref, q_ref, *refs, nb, nq, nlat):
    band, kc_ref, o_ref = refs[:nq + 2], refs[nq + 2], refs[nq + 3]
    i = pl.program_id(1)

    @pl.when(i < nlat)
    def _():
        r = lax.broadcasted_iota(jnp.int32, (ATT_BLOCK, ATT_BLOCK), 0)
        j = lax.broadcasted_iota(jnp.int32, (ATT_BLOCK, ATT_BLOCK), 1)
        ninf = jnp.float32(-jnp.inf)
        kc = kc_ref[...]
        for a in range(nq):
            blk = i * nq + a
            kv = jnp.concatenate([band[a][...], band[a + 2][...], band[a + 1][...], kc], axis=0)
            left = jnp.where((j >= r) & (blk > 0), 0.0, ninf)
            right = jnp.where((j <= r) & (blk < nb - 1), 0.0, ninf)
            rows = slice(a * ATT_BLOCK, (a + 1) * ATT_BLOCK)
            o_ref[rows, :] = _attend(q_ref[rows, :], kv, jnp.concatenate([left, right], axis=1), sink_ref)

    @pl.when(i >= nlat)
    def _():
        o_ref[...] = _attend(q_ref[...], kc_ref[...], None, sink_ref)


def _attention(lay, aq, akv, sink, with_ctx):
    nb = lay.l // ATT_BLOCK
    lc = lay.lc
    nq = 2 if (lc % (2 * ATT_BLOCK) == 0 and nb % 2 == 0) else 1
    step = nq * ATT_BLOCK
    nlat = nb // nq
    base = lay.nc // step
    nctx = lc // step if with_ctx else 0
    row0 = base if with_ctx else 0
    smem = pl.BlockSpec(memory_space=pltpu.SMEM)
    kvw = 2 * ATT_KV
    kbase = lay.nc // ATT_BLOCK

    def band(off):
        return pl.BlockSpec(
            (ATT_BLOCK, kvw),
            lambda b, i: (kbase + b * nb + jnp.clip(jnp.minimum(i, nlat - 1) * nq + off, 0, nb - 1), 0))

    def qrow(b, i):
        return jnp.where(i < nlat, base + b * nlat + i, b * nctx + (i - nlat))

    def orow(b, i):
        return jnp.where(i < nlat, row0 + b * nlat + i, b * nctx + (i - nlat))

    return pl.pallas_call(
        functools.partial(_attn_kernel, nb=nb, nq=nq, nlat=nlat),
        grid=(lay.batch, nlat + nctx),
        in_specs=[smem,
                  pl.BlockSpec((step, ATT_Q), lambda b, i: (qrow(b, i), 0))]
        + [band(off) for off in range(-1, nq + 1)]
        + [pl.BlockSpec((lc, kvw), lambda b, i: (b, 0))],
        out_specs=pl.BlockSpec((step, ATT_Q), lambda b, i: (orow(b, i), 0)),
        out_shape=jax.ShapeDtypeStruct(((row0 + lay.batch * nlat) * step, ATT_Q), BF16),
        compiler_params=_params(("parallel", "parallel")),
        name="attention",
    )(sink, aq, *([akv] * (nq + 3)))


class _Chunks:
    def __init__(self, lay, ch):
        self.ch = ch
        self.nctx = lay.lc // ch
        self.nlat = lay.l // ch
        self.base = lay.nc // ch
        self.steps = self.nctx + self.nlat
        self.total = lay.n // ch

    def fwd(self, b, j):
        return jnp.where(j < self.nctx, b * self.nctx + j, self.base + b * self.nlat + (j - self.nctx))

    def bwd(self, b, j):
        return jnp.where(j < self.nctx, b * self.nctx + (self.nctx - 1 - j),
                         self.base + b * self.nlat + (self.nlat - 1 - (j - self.nctx)))


def _ml_scan_body(kf_ref, vf_ref, gf_ref, kb_ref, vb_ref, gb_ref, tri_ref,
                  cf_ref, nf_ref, mf_ref, cb_ref, nb_ref, mb_ref, c_s, n_s, m_s, *, rows, cg):
    nsub = rows // cg
    streams = ((kf_ref, vf_ref, gf_ref, cf_ref, nf_ref, mf_ref),
               (kb_ref, vb_ref, gb_ref, cb_ref, nb_ref, mb_ref))
    for d, (k_ref, v_ref, g_ref, c_out, n_out, m_out) in enumerate(streams):
        gc = g_ref[...]
        ig = gc[:, ML_HEADS * d:ML_HEADS * (d + 1)]
        lf = gc[:, 2 * ML_HEADS + ML_HEADS * d:2 * ML_HEADS + ML_HEADS * (d + 1)]
        b = _cumsum3(tri_ref[d], lf)
        a = ig - b
        last = cg - 1 if d == 0 else 0
        gmaxs = [jnp.max(a[c * cg:(c + 1) * cg], axis=0, keepdims=True) for c in range(nsub)]
        rel = a - jnp.concatenate([jnp.broadcast_to(g, (cg, ML_HEADS)) for g in gmaxs], axis=0)
        ku = k_ref[...].astype(F32) * jnp.exp2(_head_lanes_mxu(rel, ML_DK))
        kub = ku.astype(BF16)
        vt = v_ref[...]
        m = m_s[d]
        ct = c_s[d]
        n = n_s[d]
        for c in (range(nsub) if d == 0 else reversed(range(nsub))):
            rs = slice(c * cg, (c + 1) * cg)
            mx = jnp.maximum(m, gmaxs[c])
            alpha = _head_lanes(jnp.exp2(m - mx), ML_DK)
            beta = _head_lanes(jnp.exp2(gmaxs[c] - mx), ML_DK)
            m_out[c] = m
            c_out[c] = ct.astype(BF16)
            n_out[c] = n
            ct = alpha * ct + beta * _head_outer(vt[:, rs], kub[rs], ML_HEADS)
            n = alpha * n + beta * jnp.sum(ku[rs], axis=0, keepdims=True)
            m = b[c * cg + last:c * cg + last + 1] + mx
        m_s[d] = m
        c_s[d] = ct
        n_s[d] = n


def _ml_out_kernel(mf_ref, mb_ref, qk_ref, vt_ref, o_ref, gc_ref, gr_ref, cf_ref, cb_ref, nf_ref, nb_ref,
                   tri_ref, gnt_ref, out_ref, *, c0, rows, cg):
    step = pl.program_id(0) + c0
    nsub = rows // cg
    nch = 2 * ML_HEADS
    qk = qk_ref[...]
    vt = vt_ref[...]
    gc = gc_ref[...]
    gr = gr_ref[...]
    ninf = jnp.float32(-jnp.inf)
    chain =lax.broadcasted_iota(jnp.int32, (nch, rows), 0)
    lane = lax.broadcasted_iota(jnp.int32, (nch, rows), 1)
    pos = lane & (cg - 1)
    is_fwd = chain < ML_HEADS

    l3 = jnp.concatenate(_split3(gr[nch:]), axis=0)

    def sum3(bb):
        return bb[:nch] + bb[nch:2 * nch] + bb[2 * nch:]

    b_rows = jnp.where(is_fwd, sum3(_dot_nt(l3, tri_ref[0])), sum3(_dot_nt(l3, tri_ref[1])))
    a_rows = gr[:nch] - b_rows
    g_rows = a_rows
    sh = 1
    while sh < cg:
        xf = jnp.where(pos >= sh, pltpu.roll(g_rows, sh, 1), ninf)
        xb = jnp.where(pos < cg - sh, pltpu.roll(g_rows, rows - sh, 1), ninf)
        g_rows = jnp.maximum(g_rows, jnp.where(is_fwd, xf, xb))
        sh *= 2
    m_rows = jnp.zeros((nch, rows), F32)
    for c in range(nsub):
        in_chunk = (lane >= c * cg) & (lane < (c + 1) * cg)
        for ch in range(nch):
            m_ref = mf_ref if ch < ML_HEADS else mb_ref
            m_rows = jnp.where(in_chunk & (chain == ch), m_ref[step * nsub + c, ch % ML_HEADS], m_rows)
    mt = jnp.maximum(g_rows, m_rows)
    rf = jnp.exp2(g_rows - mt)
    wp = jnp.exp2(m_rows - mt)
    emt = jnp.exp2(-b_rows - mt)

    a_cols = []
    for d in range(2):
        lo = nch + ML_HEADS * d
        a_cols.append(gc[:, ML_HEADS * d:ML_HEADS * (d + 1)] - _cumsum3(tri_ref[d], gc[:, lo:lo + ML_HEADS]))

    ss = lax.broadcasted_iota(jnp.int32, (cg, cg), 0)
    tt = lax.broadcasted_iota(jnp.int32, (cg, cg), 1)
    gnt = gnt_ref[...]
    og = o_ref[...]
    def head_stack(x):
        lane = lax.broadcasted_iota(jnp.int32, x.shape, 1)
        zero = jnp.zeros_like(x)
        return jnp.concatenate([jnp.where((lane >= ML_DK * h) & (lane < ML_DK * (h + 1)), x, zero)
                                for h in range(ML_HEADS)], axis=0)

    for c in range(nsub):
        rs = slice(c * cg, (c + 1) * cg)
        q = qk[rs, :ML_QK]
        st_all = _dot_nt(head_stack(qk[rs, ML_QK:]), q)
        inter_all = [_dot_nt(head_stack(c_ref[c]), q) for c_ref in (cf_ref, cb_ref)]
        nn = [t for x in (nf_ref[c], nb_ref[c]) for t in _split3(x)[:2]]
        dn_all = _dot_nt(head_stack(jnp.concatenate(nn + [jnp.zeros((12, ML_QK), BF16)], axis=0)), q)
        outs = []
        for hh in range(ML_HEADS):
            dv = slice(ML_DV * hh, ML_DV * (hh + 1))
            st = st_all[hh * cg:(hh + 1) * cg]
            dn2 = dn_all[16 * hh:16 * (hh + 1)]
            ht = None
            for d in range(2):
                ch = d * ML_HEADS + hh
                mask = (ss <= tt) if d == 0 else (ss >= tt)
                e = jnp.exp2(jnp.where(mask, a_cols[d][rs, hh:hh + 1] - g_rows[ch:ch + 1, rs], ninf))
                sd = st * e
                den_i = jnp.sum(sd, axis=0, keepdims=True)
                num_t = _dot(vt[dv, rs], sd.astype(BF16))
                inter_t = inter_all[d][dv]
                wpr, rfr = wp[ch:ch + 1, rs], rf[ch:ch + 1, rs]
                den = wpr * (dn2[2 * d:2 * d + 1] + dn2[2 * d + 1:2 * d + 2]) + rfr * den_i
                inv = 1.0 / jnp.maximum(jnp.abs(den), emt[ch:ch + 1, rs])
                part = (wpr * inv) * inter_t + (rfr * inv) * num_t
                ht = part if ht is None else ht + part
            ms = jnp.mean(ht * ht, axis=0, keepdims=True)
            y = (ht * lax.rsqrt(ms + EPS) * gnt[dv]).T
            outs.append(y * jax.nn.sigmoid(og[rs, dv]))
        out_ref[rs, :] = jnp.concatenate(outs, axis=1).astype(BF16)


def _out_rows(lay):
    rows = 512
    while lay.nc % rows or lay.l % rows:
        rows //= 2
    return rows


def _mix_rows(lay):
    return min(256, lay.lc), min(128, lay.lc), min(64, lay.lc)


def _scan_kernel(*refs, rows, ml_cg, hg_cg):
    ml_in, hg_in = refs[0:7], refs[7:14]
    ml_out, hg_out = refs[14:20], refs[20:22]
    ml_scratch, hg_scratch = refs[22:25], refs[25:26]

    @pl.when(pl.program_id(1) == 0)
    def _():
        for s in ml_scratch + hg_scratch:
            s[...] = jnp.zeros_like(s)

    _ml_scan_body(*ml_in, *ml_out, *ml_scratch, rows=rows, cg=ml_cg)
    _hg_scan_body(*hg_in, *hg_out, *hg_scratch, rows=rows, cg=hg_cg)


def _scans(lay, mqk, mvt, gc, hk, hlf, hvt):
    orows, ml_cg, hg_cg = _mix_rows(lay)
    ck = _Chunks(lay, orows)
    ml_sub, hg_sub = orows // ml_cg, orows // hg_cg
    ml_nt, hg_nt = lay.n // ml_cg, lay.n // hg_cg

    def spec(cols, fn, colblk=0):
        return pl.BlockSpec((orows, cols), lambda b, j: (fn(b, j), colblk))

    def tspec(rows_, fn):
        return pl.BlockSpec((rows_, orows), lambda b, j: (0, fn(b, j)))

    def st(shape, fn):
        nd = len(shape)
        return pl.BlockSpec(shape, lambda b, j: (fn(b, j),) + (0,) * (nd - 1))

    def ml_states(fn):
        return [st((ml_sub, ML_DV, ML_QK), fn), st((ml_sub, 1, ML_QK), fn), st((ml_sub, 1, ML_HEADS), fn)]

    ml_shapes = [jax.ShapeDtypeStruct((ml_nt, ML_DV, ML_QK), BF16),
                 jax.ShapeDtypeStruct((ml_nt, 1, ML_QK), F32),
                 jax.ShapeDtypeStruct((ml_nt, 1, ML_HEADS), F32)]
    hg_shape = jax.ShapeDtypeStruct((hg_nt, HG_DV, HG_K), BF16)
    tri_spec = pl.BlockSpec((2, orows, orows), lambda b, j: (0, 0, 0))
    res = pl.pallas_call(
        functools.partial(_scan_kernel, rows=orows, ml_cg=ml_cg, hg_cg=hg_cg),
        grid=(lay.batch, ck.steps),
        in_specs=[spec(ML_QK, ck.fwd, 1), tspec(ML_V, ck.fwd), spec(N_GATE, ck.fwd),
                  spec(ML_QK, ck.bwd, 1), tspec(ML_V, ck.bwd), spec(N_GATE, ck.bwd), tri_spec,
                  spec(HG_K, ck.fwd, 0), spec(HG_K, ck.fwd, 0), tspec(HG_V, ck.fwd),
                  spec(HG_K, ck.bwd, 1), spec(HG_K, ck.bwd, 1), tspec(HG_V, ck.bwd), tri_spec],
        out_specs=ml_states(ck.fwd) + ml_states(ck.bwd)
        + [st((hg_sub, HG_DV, HG_K), ck.fwd), st((hg_sub, HG_DV, HG_K), ck.bwd)],
        out_shape=ml_shapes + ml_shapes + [hg_shape, hg_shape],
        scratch_shapes=[pltpu.VMEM((2, ML_DV, ML_QK), F32),
                        pltpu.VMEM((2, 1, ML_QK), F32),
                        pltpu.VMEM((2, 1, ML_HEADS), F32),
                        pltpu.VMEM((2, HG_DV, HG_K), F32)],
        compiler_params=_params(("parallel", "arbitrary")),
        name="scan",
    )(mqk, mvt, gc, mqk, mvt, gc, _chunk_tri(orows, ml_cg),
      hk, hlf, hvt, hk, hlf, hvt, _chunk_tri(orows, hg_cg))
    return res[:6], res[6:]


def _mlstm_out(lay, states, mqk, mvt, mo, gc, gr, gnorm, with_ctx):
    cf, nf, mf, cb, nb, mb = states
    _, cg, _ = _mix_rows(lay)
    orows = _out_rows(lay)
    nsub = orows // cg
    tri = _chunk_tri(orows, cg)
    nt = lay.n // cg
    c0 = 0 if with_ctx else lay.nc // orows
    nout = lay.n // orows - c0
    smem = pl.BlockSpec(memory_space=pltpu.SMEM)

    def rows(cols):
        return pl.BlockSpec((orows, cols), lambda c: (c + c0, 0))

    cst = pl.BlockSpec((nsub, ML_DV, ML_QK), lambda c: (c + c0, 0, 0))
    nst = pl.BlockSpec((nsub, 1, ML_QK), lambda c: (c + c0, 0, 0))
    out = pl.pallas_call(
        functools.partial(_ml_out_kernel, c0=c0, rows=orows, cg=cg),
        grid=(nout,),
        in_specs=[smem, smem, rows(2 * ML_QK), pl.BlockSpec((ML_V, orows), lambda c: (0, c + c0)),
                  rows(ML_V), rows(N_GATE),
                  pl.BlockSpec((N_GATE, orows), lambda c: (0, c + c0)),
                  cst, cst, nst, nst,
                  pl.BlockSpec((2, orows, orows), lambda c: (0, 0, 0)),
                  pl.BlockSpec((ML_V, 1), lambda c: (0, 0))],
        out_specs=pl.BlockSpec((orows, ML_V), lambda c: (c, 0)),
        out_shape=jax.ShapeDtypeStruct((nout * orows, ML_V), BF16),
        compiler_params=_params(("parallel",)),
        name="mlstm_out",
    )(mf.reshape(nt, ML_HEADS), mb.reshape(nt, ML_HEADS), mqk, mvt, mo, gc, gr, cf, cb, nf, nb, tri,
      gnorm.reshape(ML_V, 1))
    return out


def _chunk_tri(rows, cg):
    t = np.arange(rows)[:, None]
    u = np.arange(rows)[None, :]
    same = (t // cg) == (u // cg)
    return jnp.asarray(np.stack([same & (u <= t), same & (u >= t)]).astype(np.float32)).astype(BF16)


def _cumsum3(tri, x):
    w = x.shape[1]
    hi = x.astype(BF16)
    r1 = x - hi.astype(F32)
    mid = r1.astype(BF16)
    lo = (r1 - mid.astype(F32)).astype(BF16)
    bb = _dot(tri, jnp.concatenate([hi, mid, lo], axis=1))
    return bb[:, :w] + bb[:, w:2 * w] + bb[:, 2 * w:]


def _level_ref(b, bs, d):
    rows, w = b.shape
    off = bs // 2 - 1 + d
    if bs >= 8:
        pieces = [jnp.broadcast_to(b[s + off:s + off + 1], (bs, w)) for s in range(0, rows, bs)]
        return jnp.concatenate(pieces, axis=0)
    b8 = b.reshape(rows // 8, 8, w)
    u = lax.broadcasted_iota(jnp.int32, (1, 8, 1), 1)
    out = None
    for s in range(0, 8, bs):
        piece = jnp.broadcast_to(b8[:, s + off:s + off + 1, :], b8.shape)
        out = piece if out is None else jnp.where(u >= s, piece, out)
    return out.reshape(rows, w)


def _hg_scan_body(kf_ref, lff_ref, vf_ref, kb_ref, lfb_ref, vb_ref, tri_ref, sf_ref, sb_ref, s_s, *, rows, cg):
    nsub = rows // cg
    streams = ((kf_ref, lff_ref, vf_ref, sf_ref), (kb_ref, lfb_ref, vb_ref, sb_ref))
    for d, (k_ref, lf_ref, v_ref, s_out) in enumerate(streams):
        b = _cumsum3(tri_ref[d], lf_ref[...])
        last = cg - 1 if d == 0 else 0
        bls = [b[c * cg + last:c * cg + last + 1] for c in range(nsub)]
        bl_rows = jnp.concatenate([jnp.broadcast_to(bl, (cg, HG_K)) for bl in bls], axis=0)
        kd = (k_ref[...] * jnp.exp2(bl_rows - b)).astype(BF16)
        vt = v_ref[...]
        s = s_s[d]
        for c in (range(nsub) if d == 0 else reversed(range(nsub))):
            rs = slice(c * cg, (c + 1) * cg)
            s_out[c] = s.astype(BF16)
            s = jnp.exp2(bls[c]) * s + _head_outer(vt[:, rs], kd[rs], HG_HEADS)
        s_s[d] = s


def _hg_out_kernel(q_ref, k_ref, lf_ref, v_ref, g_ref, sf_ref, sb_ref, tri_ref, gn_ref, out_ref,
                   *, rows, cg, nlev):
    q = q_ref[...]
    kk = k_ref[...]
    lff = lf_ref[...]
    v = v_ref[...]
    nsub = rows // cg
    qb = q.astype(BF16)
    t_i = lax.broadcasted_iota(jnp.int32, (cg, cg), 0)
    s_i = lax.broadcasted_iota(jnp.int32, (cg, cg), 1)
    xr = t_i ^ s_i
    level = jnp.zeros((cg, cg), jnp.int32)
    for l in range(nlev):
        level = level + (xr >= 2 ** l).astype(jnp.int32)
    assert 2 * HG_DK == 128 and HG_HEADS % 2 == 0
    npair = HG_HEADS // 2
    pairs = [slice(128 * p, 128 * (p + 1)) for p in range(npair)]
    bs, levs = [], []
    for d in range(2):
        bs.append(_cumsum3(tri_ref[d], lff[:, HG_K * d:HG_K * (d + 1)]))
        lev_d = jnp.where((t_i > s_i) if d == 0 else (t_i < s_i), level, -1)
        lev_d = jnp.where(t_i == s_i, 0, lev_d)
        levs.append(jnp.concatenate([lev_d, lev_d], axis=1))
    gn = gn_ref[...]
    gate = g_ref[...]
    lo = lax.broadcasted_iota(jnp.int32, (cg, 128), 1) < HG_DK
    zk = jnp.zeros((cg, 128), BF16)

    def split_heads(x):
        return jnp.concatenate([jnp.where(lo, x, zk), jnp.where(lo, zk, x)], axis=0)

    for c in range(nsub):
        rs = slice(c * cg, (c + 1) * cg)
        qc = q[rs]
        amat = [None] * npair
        inter = [None] * npair
        for d in range(2):
            kc = kk[rs, HG_K * d:HG_K * (d + 1)]
            bc = bs[d][rs]
            a = [jnp.zeros((cg, 2 * cg), F32)] * npair
            for l in range(nlev + 1):
                if l == 0:
                    qt, kt = qb[rs], kc.astype(BF16)
                else:
                    e = jnp.exp2(-jnp.abs(bc - _level_ref(bc, 2 ** l, d)))
                    qt, kt = (qc * e).astype(BF16), (kc * e).astype(BF16)
                a = [jnp.where(levs[d] == l, _dot_nt(qt[:, pr], split_heads(kt[:, pr])), a[p])
                     for p, pr in enumerate(pairs)]
            qe = (qc * jnp.exp2(bc)).astype(BF16)
            st = (sf_ref if d == 0 else sb_ref)[c]
            it = []
            for pr in pairs:
                qs = split_heads(qe[:, pr])
                r2 = _dot_nt(qs, st[:, pr])
                it.append(jnp.concatenate([r2[:cg], r2[cg:]], axis=1))
            amat = a if d == 0 else [x + y for x, y in zip(amat, a)]
            inter = it if d == 0 else [x + y for x, y in zip(inter, it)]
        outs = []
        for p in range(npair):
            vp = v[rs, 2 * HG_DV * p:2 * HG_DV * (p + 1)]
            zv = jnp.zeros((cg, HG_DV), BF16)
            vbd = jnp.concatenate([jnp.concatenate([vp[:, :HG_DV], zv], axis=1),
                                   jnp.concatenate([zv, vp[:, HG_DV:]], axis=1)], axis=0)
            o = inter[p] + _dot(amat[p].astype(BF16), vbd)
            for hh in (2 * p, 2 * p + 1):
                sl = slice(HG_DV * hh, HG_DV * (hh + 1))
                oh = o[:, HG_DV * (hh - 2 * p):HG_DV * (hh - 2 * p + 1)]
                outs.append(_rms(oh, gn[:, sl]) * _silu(gate[rs, sl]))
        out_ref[rs, :] = jnp.concatenate(outs, axis=1).astype(BF16)


def _hgrn2_out(lay, states, hq, hk, hlf, hv, hgate, gnorm, with_ctx):
    sf, sb = states
    _, _, cg = _mix_rows(lay)
    orows = _out_rows(lay)
    nsub = orows // cg
    nlev = int(np.log2(cg))
    tri = _chunk_tri(orows, cg)
    c0 = 0 if with_ctx else lay.nc // orows
    nout = lay.n // orows - c0

    def rows(cols):
        return pl.BlockSpec((orows, cols), lambda c: (c + c0, 0))

    sst = pl.BlockSpec((nsub, HG_DV, HG_K), lambda c: (c + c0, 0, 0))
    out = pl.pallas_call(
        functools.partial(_hg_out_kernel, rows=orows, cg=cg, nlev=nlev),
        grid=(nout,),
        in_specs=[rows(HG_K), rows(2 * HG_K), rows(2 * HG_K), rows(HG_V), rows(HG_V), sst, sst,
                  pl.BlockSpec((2, orows, orows), lambda c: (0, 0, 0)),
                  pl.BlockSpec((1, HG_V), lambda c: (0, 0))],
        out_specs=pl.BlockSpec((orows, HG_V), lambda c: (c, 0)),
        out_shape=jax.ShapeDtypeStruct((nout * orows, HG_V), BF16),
        compiler_params=_params(("parallel",)),
        name="hgrn2_out",
    )(hq, hk, hlf, hv, hgate, sf, sb, tri, gnorm)
    return out


def _rope_table(lay):
    l = lay.l
    rows = l // GRID_W
    row = jnp.repeat(jnp.arange(rows, dtype=F32), GRID_W)
    col = jnp.tile(jnp.arange(GRID_W, dtype=F32), rows)
    n_freq = ATT_HEAD_DIM // 4
    inv = ROPE_BASE ** (-jnp.arange(n_freq, dtype=F32) / n_freq)
    ar, ac = row[:, None] * inv, col[:, None] * inv
    cos = jnp.concatenate([jnp.cos(ar), jnp.cos(ar), jnp.cos(ac), jnp.cos(ac)], axis=1)
    sin = jnp.concatenate([-jnp.sin(ar), jnp.sin(ar), -jnp.sin(ac), jnp.sin(ac)], axis=1)
    lat = jnp.concatenate([cos, cos, sin, sin], axis=1)
    lat = jnp.tile(lat, (lay.batch, 1))
    ctx = jnp.concatenate([jnp.ones((lay.nc, 128), F32), jnp.zeros((lay.nc, 128), F32)], axis=1)
    return jnp.concatenate([ctx, lat], axis=0)


def kernel(x, c, ctx, c_ctx, w_ada, b_ada, norm_pre, norm_post, ffn_w1, ffn_w3, ffn_w2, w_in, att_sink,
           ml_conv, ml_f_bias, ml_norm, hg_lb_logits, hg_norm, w_branch_att, w_branch_ml, w_branch_hg,
           w_out):
    batch, l, d = x.shape
    lc = ctx.shape[1]
    depth = w_ada.shape[0]
    assert l % GRID_W == 0 and l & (l - 1) == 0 and lc & (lc - 1) == 0
    assert l % 256 == 0 and lc % 128 == 0 and batch + 1 <= MOD_ROWS
    lay = _Layout(batch, lc, l)
    lay_ffn = _Layout(batch, lc, l, FFN_TM)

    xs = (ctx.reshape(batch * lc, d), x.reshape(batch * l, d))
    cs = jnp.concatenate([c, c_ctx[None, :], jnp.zeros((MOD_ROWS - batch - 1, d), F32)], axis=0)
    mod_all = _modulation(cs, w_ada, b_ada).reshape(depth, MOD_ROWS, N_MOD, d)
    rope = _rope_table(lay)

    lb_all = jnp.cumsum(jax.nn.softmax(hg_lb_logits.astype(F32), axis=0), axis=0)
    lb_all = lb_all - lb_all[0:1]

    sizes = (ATT_Q, ATT_KV, ATT_KV, ML_QK, ML_QK, ML_V, ML_V, 2 * ML_HEADS, 2 * ML_HEADS,
             HG_K, HG_K, HG_K, HG_V, HG_V, 3 * d)
    offs = np.concatenate([[0], np.cumsum(sizes)])
    g0, g1 = int(offs[7]), int(offs[9])

    for layer in range(depth):
        last = layer == depth - 1
        mod = mod_all[layer]
        wl = w_in[layer]
        w_gate = wl[:, g0:g1]
        w_proj = jnp.concatenate(
            [wl[:, :g0], wl[:, g1:int(offs[14])], w_gate, jnp.zeros((d, GATE_PAD - N_GATE), F32)],
            axis=1).astype(BF16)
        w_gate_t = jnp.concatenate([wl[:, int(offs[5]):int(offs[6])], wl[:, int(offs[12]):int(offs[13])],
                                    w_gate], axis=1).T.astype(BF16)
        w_brg = wl[:, int(offs[14]):].astype(BF16)
        fb = ml_f_bias[layer].reshape(1, 2 * ML_HEADS)
        fbrow = jnp.concatenate([jnp.zeros((1, 2 * ML_HEADS), F32), fb], axis=1)
        fbcol = fbrow.reshape(N_GATE, 1)
        lb = lb_all[layer]
        lbv = jnp.stack([jnp.tile(jnp.log(lb), 2), jnp.tile(jnp.log1p(-lb), 2), jnp.tile(1.0 - lb, 2)])

        def npre(i):
            return norm_pre[layer, i].reshape(1, d)

        def npost(i):
            return norm_post[layer, i].reshape(1, d)

        def ffn(xin, j, i, latent_only=False):
            return _ffn(lay_ffn, xin, mod, j, npre(j), npost(j), ffn_w1[layer, i].astype(BF16),
                        ffn_w3[layer, i].astype(BF16), ffn_w2[layer, i].astype(BF16), latent_only)

        xs = ffn(xs, 0, 0)
        (aq, akv, mqk, mo, gc, gr, hq, hk, hlf, hv, hgate, mvt, hvt) = _project(
            lay, xs, mod, npre(1), w_proj, w_gate_t, rope, ml_conv[layer], fbrow, fbcol, lbv)
        att = _attention(lay, aq, akv, att_sink[layer], not last)
        ml_states, hg_states = _scans(lay, mqk, mvt, gc, hk, hlf, hvt)
        ml = _mlstm_out(lay, ml_states, mqk, mvt, mo, gc, gr, ml_norm[layer], not last)
        hg = _hgrn2_out(lay, hg_states, hq, hk, hlf, hv, hgate, hg_norm[layer].reshape(1, HG_V), not last)
        xs = _merge(lay_ffn, xs, att, ml, hg, mod, npre(1), npost(1), w_brg,
                    w_branch_att[layer].astype(BF16), w_branch_ml[layer].astype(BF16),
                    w_branch_hg[layer].astype(BF16), w_out[layer].astype(BF16), last)
        xs = ffn(xs, 2, 1, last)
    return xs.reshape(batch, l, d)
```

```python
import functools

import numpy as np
import jax
import jax.numpy as jnp
from jax import lax
from jax.experimental import pallas as pl
from jax.experimental.pallas import tpu as pltpu

F32 = jnp.float32
BF16 = jnp.bfloat16

ATT_HEADS = 8
ATT_KV_HEADS = 2
ATT_GROUP = ATT_HEADS // ATT_KV_HEADS
ATT_HEAD_DIM = 64
ATT_BLOCK = 128
ML_HEADS = 4
ML_DK = 64
ML_DV = 128
HG_HEADS = 4
HG_DK = 64
HG_DV = 128
GRID_W = 64
ROPE_BASE = 10000.0
EPS = 1e-6
N_MOD = 9
MOD_ROWS = 16

ATT_Q = ATT_HEADS * ATT_HEAD_DIM
ATT_KV = ATT_KV_HEADS * ATT_HEAD_DIM
ML_QK = ML_HEADS * ML_DK
ML_V = ML_HEADS * ML_DV
HG_K = HG_HEADS * HG_DK
HG_V = HG_HEADS * HG_DV
N_GATE = 4 * ML_HEADS
GATE_PAD = 128

OFF_ATT = 0
OFF_MQK = OFF_ATT + ATT_Q + 2 * ATT_KV
OFF_MVO = OFF_MQK + 2 * ML_QK
OFF_HG = OFF_MVO + 2 * ML_V
OFF_GATE = OFF_HG + 3 * HG_K + 2 * HG_V
W_PROJ_COLS = OFF_GATE + GATE_PAD

LOG2E = 1.4426950408889634
VMEM_LIMIT = 56 * 1024 * 1024
FFN_TM = 1024
FFN_SPLIT = 11


def _dot(a, b):
    return jnp.dot(a, b, preferred_element_type=F32)


def _dot_nt(a, b):
    return lax.dot_general(a, b, (((1,), (1,)), ((), ())), preferred_element_type=F32)


def _dot_tn(a, b):
    return lax.dot_general(a, b, (((0,), (0,)), ((), ())), preferred_element_type=F32)


def _rms(x, g):
    ms = jnp.mean(x * x, axis=-1, keepdims=True)
    return x * lax.rsqrt(ms + EPS) * g


def _silu(x):
    return x * jax.nn.sigmoid(x)


def _log_sigmoid(x):
    return jnp.minimum(x, 0.0) - jnp.log(1.0 + jnp.exp(-jnp.abs(x)))


def _head_lanes(x, w):
    r, h = x.shape
    return jnp.concatenate([jnp.broadcast_to(x[:, i:i + 1], (r, w)) for i in range(h)], axis=1)


def _split3(x):
    hi = x.astype(BF16)
    r1 = x - hi.astype(F32)
    mid = r1.astype(BF16)
    return hi, mid, (r1 - mid.astype(F32)).astype(BF16)


def _head_outer(vt, k, heads):
    dv, dk = vt.shape[0] // heads, k.shape[1] // heads
    assert heads % 2 == 0 and 2 * dk == 128
    lo = lax.broadcasted_iota(jnp.int32, (dv, 2 * dk), 1) < dk
    outs = []
    for p in range(heads // 2):
        big = _dot(vt[2 * p * dv:(2 * p + 2) * dv], k[:, 2 * p * dk:(2 * p + 2) * dk])
        outs.append(jnp.where(lo, big[:dv], big[dv:]))
    return jnp.concatenate(outs, axis=1)


def _head_lanes_mxu(x, w):
    h = x.shape[1]
    row = lax.broadcasted_iota(jnp.int32, (3 * h, h * w), 0)
    lane = lax.broadcasted_iota(jnp.int32, (3 * h, h * w), 1)
    sel = jnp.zeros((3 * h, h * w), jnp.bool_)
    for t in range(3):
        for i in range(h):
            sel = sel | ((row == t * h + i) & (lane >= i * w) & (lane < (i + 1) * w))
    return _dot(jnp.concatenate(_split3(x), axis=1), jnp.where(sel, 1.0, 0.0).astype(BF16))


def _params(sem):
    return pltpu.CompilerParams(dimension_semantics=sem, vmem_limit_bytes=VMEM_LIMIT)


def _resident(shape):
    nd = len(shape)
    return pl.BlockSpec(shape, lambda *_: (0,) * nd, pipeline_mode=pl.Buffered(1))


def _mod_kernel(c_ref, w_ref, b_ref, o_ref):
    s = _silu(c_ref[...]).astype(BF16)
    o_ref[0] = _dot(s, w_ref[0].astype(BF16)) + b_ref[0]


def _modulation(cs, w_ada, b_ada):
    depth, d, nm = w_ada.shape
    tn = 1024
    return pl.pallas_call(
        _mod_kernel,
        grid=(depth, nm // tn),
        in_specs=[
            pl.BlockSpec((MOD_ROWS, d), lambda l, n: (0, 0)),
            pl.BlockSpec((1, d, tn), lambda l, n: (l, 0, n)),
            pl.BlockSpec((1, 1, tn), lambda l, n: (l, 0, n)),
        ],
        out_specs=pl.BlockSpec((1, MOD_ROWS, tn), lambda l, n: (l, 0, n)),
        out_shape=jax.ShapeDtypeStruct((depth, MOD_ROWS, nm), F32),
        compiler_params=_params(("parallel", "parallel")),
        name="modulation",
    )(cs, w_ada, b_ada.reshape(depth, 1, nm))


class _Layout:
    def __init__(self, batch, lc, l, tm=512):
        self.batch, self.lc, self.l = batch, lc, l
        self.nc = batch * lc
        self.n = self.nc + batch * l
        while self.nc % tm or l % tm:
            tm //= 2
        self.tm = tm
        self.nct = self.nc // tm
        self.tpb = l // tm
        self.ntiles = self.n // tm

    def mod_index(self, t):
        return jnp.where(t < self.nct, self.batch, (t - self.nct) // self.tpb)


def _ffn_split_kernel(c_ref, x_ref, *rest, nct, **kw):
    _ffn_body(jnp.where(pl.program_id(0) < nct, c_ref[...], x_ref[...]), *rest, **kw)


def _ffn_kernel(x_ref, *rest, **kw):
    _ffn_body(x_ref[...], *rest, **kw)


def _ffn_body(x, mod_ref, gpre_ref, gpost_ref, w1_ref, w3_ref, w2_ref, o_ref, *, j, halves):
    mod = mod_ref[0]
    shift, scale, gate = mod[3 * j:3 * j + 1], mod[3 * j + 1:3 * j + 2], mod[3 * j + 2:3 * j + 3]
    h = (_rms(x, gpre_ref[...] * (1.0 + scale)) + shift).astype(BF16)
    dff = w1_ref.shape[1]
    step = dff // halves
    y = None
    for c in range(halves):
        sl = slice(c * step, (c + 1) * step)
        a = _dot(h, w1_ref[:, sl])
        b = _dot(h, w3_ref[:, sl])
        part = _dot((_silu(a) * b).astype(BF16), w2_ref[sl, :])
        y = part if y is None else y + part
    o_ref[...] = x + 0.5 * gate * _rms(y, gpost_ref[...])


def _ffn(lay, xin, mod, j, gpre, gpost, w1, w3, w2, latent_only=False):
    d, dff = w1.shape
    tm = lay.tm
    halves = FFN_SPLIT if (dff // FFN_SPLIT) % 128 == 0 else 1
    tile0 = lay.nct if latent_only else 0
    ntiles = lay.ntiles - tile0
    row = pl.BlockSpec((tm, d), lambda t: (t, 0))
    if isinstance(xin, tuple):
        body = functools.partial(_ffn_split_kernel, nct=lay.nct, j=j, halves=halves)
        xs_specs = [pl.BlockSpec((tm, d), lambda t: (jnp.minimum(t, lay.nct - 1), 0)),
                    pl.BlockSpec((tm, d), lambda t: (jnp.maximum(t - lay.nct, 0), 0))]
    else:
        body = functools.partial(_ffn_kernel, j=j, halves=halves)
        xs_specs, xin = [row], (xin,)
    return pl.pallas_call(
        body,
        grid=(ntiles,),
        in_specs=xs_specs + [
            pl.BlockSpec((1, N_MOD, d), lambda t: (lay.mod_index(t + tile0), 0, 0)),
            _resident((1, d)), _resident((1, d)),
            _resident((d, dff)), _resident((d, dff)), _resident((dff, d)),
        ],
        out_specs=row,
        out_shape=jax.ShapeDtypeStruct((ntiles * tm, d), F32),
        compiler_params=_params(("parallel",)),
        name="ffn",
    )(*xin, mod, gpre, gpost, w1, w3, w2)


def _proj_kernel(x_ref, xp_ref, xn_ref, mod_ref, gpre_ref, w_ref, wgt_ref, rope_ref, conv_ref,
                 fbrow_ref, fbcol_ref, lbv_ref,
                 aq_ref, akv_ref, mqk_ref, mo_ref, gc_ref, gr_ref,
                 hq_ref, hk_ref, hlf_ref, hv_ref, hg_ref, mvt_ref, hvt_ref, *, nct, lc, l, tm):
    t = pl.program_id(0)
    mod = mod_ref[0]
    shift, scale = mod[3:4], mod[4:5]
    gpre = gpre_ref[...]

    gain = gpre * (1.0 + scale)

    def pre(xx):
        return (_rms(xx, gain) + shift).astype(BF16)

    h = pre(x_ref[...])

    pa = _dot(h, w_ref[:, OFF_ATT:OFF_MQK])
    rope = rope_ref[...]
    cos, sin = rope[:, :128], rope[:, 128:]

    def swap(v):
        w = v.shape[1]
        lane = lax.broadcasted_iota(jnp.int32, v.shape, 1)
        return jnp.where((lane & 16) == 0, pltpu.roll(v, w - 16, 1), pltpu.roll(v, 16, 1))

    q = pa[:, :ATT_Q]
    k = pa[:, ATT_Q:ATT_Q + ATT_KV]
    cos4 = jnp.concatenate([cos] * (ATT_Q // 128), axis=1)
    sin4 = jnp.concatenate([sin] * (ATT_Q // 128), axis=1)
    q = (q * cos4 + swap(q) * sin4) * (ATT_HEAD_DIM ** -0.5 * LOG2E)
    k = k * cos + swap(k) * sin
    aq_ref[...] = q.astype(BF16)
    akv_ref[...] = jnp.concatenate([k, pa[:, ATT_Q + ATT_KV:]], axis=1).astype(BF16)

    wm = w_ref[:, OFF_MQK:OFF_MVO]
    pm = _dot(h, wm)
    pprev = _dot(pre(xp_ref[...]), wm)[7:8]
    pnext = _dot(pre(xn_ref[...]), wm)[0:1]
    r = lax.broadcasted_iota(jnp.int32, (tm, 1), 0)
    grow = t * tm + r
    is_ctx = t < nct
    pos = jnp.where(is_ctx, grow & (lc - 1), (grow - nct * tm) & (l - 1))
    last = jnp.where(is_ctx, lc - 1, l - 1)
    dn = jnp.where(r == 0, pprev, pltpu.roll(pm, 1, 0))
    dn = jnp.where(pos == 0, 0.0, dn)
    up = jnp.where(r == tm - 1, pnext, pltpu.roll(pm, tm - 1, 0))
    up = jnp.where(pos == last, 0.0, up)
    cw = conv_ref[...]
    cv = dn * cw[0:1] + pm * cw[1:2] + up * cw[2:3]
    lane = lax.broadcasted_iota(jnp.int32, (1, 2 * ML_QK), 1)
    mqk_ref[...] = (_silu(cv) * jnp.where(lane < ML_QK, 1.0, ML_DK ** -0.5)).astype(BF16)

    mo_ref[...] = _dot(h, w_ref[:, OFF_MVO + ML_V:OFF_HG])

    pg = _dot(h, w_ref[:, OFF_GATE:OFF_GATE + GATE_PAD])[:, :N_GATE] + fbrow_ref[...]
    lane16 = lax.broadcasted_iota(jnp.int32, (1, N_GATE), 1)
    gc_ref[...] = jnp.where(lane16 < 2 * ML_HEADS, pg, _log_sigmoid(pg)) * LOG2E
    pt = _dot_nt(wgt_ref[...], h)
    mvt_ref[...] = pt[:ML_V].astype(BF16)
    hvt_ref[...] = pt[ML_V:ML_V + HG_V].astype(BF16)
    pgr = pt[ML_V + HG_V:] + fbcol_ref[...]
    row16 = lax.broadcasted_iota(jnp.int32, (N_GATE, 1), 0)
    gr_ref[...] = jnp.where(row16 < 2 * ML_HEADS, pgr, _log_sigmoid(pgr)) * LOG2E

    ph = _dot(h, w_ref[:, OFF_HG:OFF_GATE])
    hq_ref[...] = _silu(ph[:, :HG_K]) * (HG_DK ** -0.5)
    z = ph[:, HG_K:3 * HG_K]
    lbv = lbv_ref[...]
    log_lb, log_1m_lb, one_m_lb = lbv[0:1], lbv[1:2], lbv[2:3]
    ls = _log_sigmoid(z)
    bv = log_1m_lb + ls
    hlf_ref[...] = (jnp.maximum(log_lb, bv) + jnp.log(1.0 + jnp.exp(-jnp.abs(log_lb - bv)))) * LOG2E
    hk_ref[...] = one_m_lb * jnp.exp(ls - z)
    hv_ref[...] = ph[:, 3 * HG_K:3 * HG_K + HG_V].astype(BF16)
    hg_ref[...] = ph[:, 3 * HG_K + HG_V:]


def _project(lay, x, mod, gpre, w, wgt, rope, conv, fbrow, fbcol, lbv):
    n, d = x.shape
    tm = lay.tm
    nb8 = n // 8

    def row(c):
        return pl.BlockSpec((tm, c), lambda t: (t, 0))

    out_cols = [(ATT_Q, BF16), (2 * ATT_KV, BF16), (2 * ML_QK, BF16), (ML_V, F32),
                (N_GATE, F32), (N_GATE, F32, None), (HG_K, F32), (2 * HG_K, F32), (2 * HG_K, F32),
                (HG_V, BF16), (HG_V, F32), (ML_V, BF16, None), (HG_V, BF16, None)]
    out_specs, out_shape = [], []
    for oc in out_cols:
        if len(oc) == 3:
            out_specs.append(pl.BlockSpec((oc[0], tm), lambda t: (0, t)))
            out_shape.append(jax.ShapeDtypeStruct((oc[0], n), oc[1]))
        else:
            out_specs.append(row(oc[0]))
            out_shape.append(jax.ShapeDtypeStruct((n, oc[0]), oc[1]))
    return pl.pallas_call(
        functools.partial(_proj_kernel, nct=lay.nct, lc=lay.lc, l=lay.l, tm=tm),
        grid=(lay.ntiles,),
        in_specs=[
            row(d),
            pl.BlockSpec((8, d), lambda t: (jnp.maximum(t * (tm // 8) - 1, 0), 0)),
            pl.BlockSpec((8, d), lambda t: (jnp.minimum((t + 1) * (tm // 8), nb8 - 1), 0)),
            pl.BlockSpec((1, N_MOD, d), lambda t: (lay.mod_index(t), 0, 0)),
            _resident((1, d)),
            _resident(w.shape), _resident(wgt.shape),
            row(256),
            _resident(conv.shape), _resident(fbrow.shape), _resident(fbcol.shape), _resident(lbv.shape),
        ],
        out_specs=out_specs,
        out_shape=out_shape,
        compiler_params=_params(("parallel",)),
        name="project",
    )(x, x, x, mod, gpre, w, wgt, rope, conv, fbrow, fbcol, lbv)


def _merge_kernel(x_ref, att_ref, ml_ref, hg_ref, mod_ref, gpre_ref, gpost_ref,
                  wg_ref, wa_ref, wm_ref, wh_ref, wo_ref, o_ref):
    x = x_ref[...]
    d = x.shape[1]
    mod = mod_ref[0]
    shift, scale, gate = mod[3:4], mod[4:5], mod[5:6]
    h = (_rms(x, gpre_ref[...] * (1.0 + scale)) + shift).astype(BF16)
    y = None
    for i, (b_ref, w_ref) in enumerate(((att_ref, wa_ref), (ml_ref, wm_ref), (hg_ref, wh_ref))):
        g = jax.nn.sigmoid(_dot(h, wg_ref[:, i * d:(i + 1) * d]))
        part = g * _dot(b_ref[...], w_ref[...])
        y = part if y is None else y + part
    yy = _dot(y.astype(BF16), wo_ref[...])
    o_ref[...] = x + gate * _rms(yy, gpost_ref[...])


def _merge(lay, x, att, ml, hg, mod, gpre, gpost, wg, wa, wm, wh, wo, latent_only=False):
    d = x.shape[1]
    tm = lay.tm
    tile0 = lay.nct if latent_only else 0
    ntiles = lay.ntiles - tile0

    def row(c, off=0):
        return pl.BlockSpec((tm, c), lambda t: (t + off, 0))

    return pl.pallas_call(
        _merge_kernel,
        grid=(ntiles,),
        in_specs=[
            row(d, tile0), row(ATT_Q), row(ML_V), row(HG_V),
            pl.BlockSpec((1, N_MOD, d), lambda t: (lay.mod_index(t + tile0), 0, 0)),
            _resident((1, d)), _resident((1, d)),
            _resident(wg.shape), _resident(wa.shape), _resident(wm.shape), _resident(wh.shape),
            _resident(wo.shape),
        ],
        out_specs=row(d),
        out_shape=jax.ShapeDtypeStruct((ntiles * tm, d), F32),
        compiler_params=_params(("parallel",)),
        name="merge",
    )(x, att, ml, hg, mod, gpre, gpost, wg, wa, wm, wh, wo)


def _swap_lane_halves(x):
    return pltpu.bitcast(pltpu.roll(pltpu.bitcast(x, jnp.uint32), 64, 1), BF16)


def _attend(q, kv, bias, sink_ref):
    tq = q.shape[0]
    assert ATT_GROUP == 4 and 2 * ATT_HEAD_DIM == 128 and ATT_KV == 128
    k, v = kv[:, :ATT_KV], kv[:, ATT_KV:]
    ks, vs = _swap_lane_halves(k), _swap_lane_halves(v)
    lo_k = lax.broadcasted_iota(jnp.int32, k.shape, 1) < ATT_HEAD_DIM
    lo_q = lax.broadcasted_iota(jnp.int32, (tq, 128), 1) < ATT_HEAD_DIM
    row = lax.broadcasted_iota(jnp.int32, (4 * tq, 1), 0)
    zero = jnp.zeros_like(k)
    zq = jnp.zeros((tq, 128), BF16)
    bias4 = None if bias is None else jnp.concatenate([bias] * 4, axis=0)
    outs = []
    for g in range(ATT_KV_HEADS):
        own, other = (k, ks) if g == 0 else (ks, k)
        k2 = jnp.where(lo_k, own, other)
        vown, voth = (v, vs) if g == 0 else (vs, v)
        va = jnp.where(lo_k, vown, zero)
        vb = jnp.where(lo_k, zero, voth)
        p0 = q[:, 256 * g:256 * g + 128]
        p1 = q[:, 256 * g + 128:256 * g + 256]
        qg = jnp.concatenate([jnp.where(lo_q, p0, zq), jnp.where(lo_q, p1, zq),
                              jnp.where(lo_q, zq, p0), jnp.where(lo_q, zq, p1)], axis=0)
        s = _dot_nt(qg, k2)
        if bias4 is not None:
            nbk = bias4.shape[1]
            s = jnp.concatenate([s[:, :nbk] + bias4, s[:, nbk:]], axis=1)
        h0 = 4 * g
        sk = jnp.where(row < tq, sink_ref[h0],
                       jnp.where(row < 2 * tq, sink_ref[h0 + 2],
                                 jnp.where(row < 3 * tq, sink_ref[h0 + 1], sink_ref[h0 + 3]))) * LOG2E
        m = jnp.maximum(jnp.max(s, axis=1, keepdims=True), sk)
        p = jnp.exp2(s - m)
        inv = 1.0 / (jnp.sum(p, axis=1, keepdims=True) + jnp.exp2(sk - m))
        pb = p.astype(BF16)
        o = _dot(pb[:2 * tq], va) * inv[:2 * tq] + _dot(pb[2 * tq:], vb) * inv[2 * tq:]
        outs += [o[:tq], o[tq:]]
    return jnp.concatenate(outs, axis=1).astype(BF16)


def _attn_kernel(sink_ref, q_ref, *refs, nb, nq, nlat):
    band, kc_ref, o_ref = refs[:nq + 2], refs[nq + 2], refs[nq + 3]
    i = pl.program_id(1)

    @pl.when(i < nlat)
    def _():
        r = lax.broadcasted_iota(jnp.int32, (ATT_BLOCK, ATT_BLOCK), 0)
        j = lax.broadcasted_iota(jnp.int32, (ATT_BLOCK, ATT_BLOCK), 1)
        ninf = jnp.float32(-jnp.inf)
        kc = kc_ref[...]
        for a in range(nq):
            blk = i * nq + a
            kv = jnp.concatenate([band[a][...], band[a + 2][...], band[a + 1][...], kc], axis=0)
            left = jnp.where((j >= r) & (blk > 0), 0.0, ninf)
            right = jnp.where((j <= r) & (blk < nb - 1), 0.0, ninf)
            rows = slice(a * ATT_BLOCK, (a + 1) * ATT_BLOCK)
            o_ref[rows, :] = _attend(q_ref[rows, :], kv, jnp.concatenate([left, right], axis=1), sink_ref)

    @pl.when(i >= nlat)
    def _():
        o_ref[...] = _attend(q_ref[...], kc_ref[...], None, sink_ref)


def _attention(lay, aq, akv, sink, with_ctx):
    nb = lay.l // ATT_BLOCK
    lc = lay.lc
    nq = 2 if (lc % (2 * ATT_BLOCK) == 0 and nb % 2 == 0) else 1
    step = nq * ATT_BLOCK
    nlat = nb // nq
    base = lay.nc // step
    nctx = lc // step if with_ctx else 0
    row0 = base if with_ctx else 0
    smem = pl.BlockSpec(memory_space=pltpu.SMEM)
    kvw = 2 * ATT_KV
    kbase = lay.nc // ATT_BLOCK

    def band(off):
        return pl.BlockSpec(
            (ATT_BLOCK, kvw),
            lambda b, i: (kbase + b * nb + jnp.clip(jnp.minimum(i, nlat - 1) * nq + off, 0, nb - 1), 0))

    def qrow(b, i):
        return jnp.where(i < nlat, base + b * nlat + i, b * nctx + (i - nlat))

    def orow(b, i):
        return jnp.where(i < nlat, row0 + b * nlat + i, b * nctx + (i - nlat))

    return pl.pallas_call(
        functools.partial(_attn_kernel, nb=nb, nq=nq, nlat=nlat),
        grid=(lay.batch, nlat + nctx),
        in_specs=[smem,
                  pl.BlockSpec((step, ATT_Q), lambda b, i: (qrow(b, i), 0))]
        + [band(off) for off in range(-1, nq + 1)]
        + [pl.BlockSpec((lc, kvw), lambda b, i: (b, 0))],
        out_specs=pl.BlockSpec((step, ATT_Q), lambda b, i: (orow(b, i), 0)),
        out_shape=jax.ShapeDtypeStruct(((row0 + lay.batch * nlat) * step, ATT_Q), BF16),
        compiler_params=_params(("parallel", "parallel")),
        name="attention",
    )(sink, aq, *([akv] * (nq + 3)))


class _Chunks:
    def __init__(self, lay, ch):
        self.ch = ch
        self.nctx = lay.lc // ch
        self.nlat = lay.l // ch
        self.base = lay.nc // ch
        self.steps = self.nctx + self.nlat
        self.total = lay.n // ch

    def fwd(self, b, j):
        return jnp.where(j < self.nctx, b * self.nctx + j, self.base + b * self.nlat + (j - self.nctx))

    def bwd(self, b, j):
        return jnp.where(j < self.nctx, b * self.nctx + (self.nctx - 1 - j),
                         self.base + b * self.nlat + (self.nlat - 1 - (j - self.nctx)))


def _ml_scan_body(kf_ref, vf_ref, gf_ref, kb_ref, vb_ref, gb_ref, tri_ref,
                  cf_ref, nf_ref, mf_ref, cb_ref, nb_ref, mb_ref, c_s, n_s, m_s, *, rows, cg):
    nsub = rows // cg
    streams = ((kf_ref, vf_ref, gf_ref, cf_ref, nf_ref, mf_ref),
               (kb_ref, vb_ref, gb_ref, cb_ref, nb_ref, mb_ref))
    for d, (k_ref, v_ref, g_ref, c_out, n_out, m_out) in enumerate(streams):
        gc = g_ref[...]
        ig = gc[:, ML_HEADS * d:ML_HEADS * (d + 1)]
        lf = gc[:, 2 * ML_HEADS + ML_HEADS * d:2 * ML_HEADS + ML_HEADS * (d + 1)]
        b = _cumsum3(tri_ref[d], lf)
        a = ig - b
        last = cg - 1 if d == 0 else 0
        gmaxs = [jnp.max(a[c * cg:(c + 1) * cg], axis=0, keepdims=True) for c in range(nsub)]
        rel = a - jnp.concatenate([jnp.broadcast_to(g, (cg, ML_HEADS)) for g in gmaxs], axis=0)
        ku = k_ref[...].astype(F32) * jnp.exp2(_head_lanes_mxu(rel, ML_DK))
        kub = ku.astype(BF16)
        vt = v_ref[...]
        m = m_s[d]
        ct = c_s[d]
        n = n_s[d]
        for c in (range(nsub) if d == 0 else reversed(range(nsub))):
            rs = slice(c * cg, (c + 1) * cg)
            mx = jnp.maximum(m, gmaxs[c])
            alpha = _head_lanes(jnp.exp2(m - mx), ML_DK)
            beta = _head_lanes(jnp.exp2(gmaxs[c] - mx), ML_DK)
            m_out[c] = m
            c_out[c] = ct.astype(BF16)
            n_out[c] = n
            ct = alpha * ct + beta * _head_outer(vt[:, rs], kub[rs], ML_HEADS)
            n = alpha * n + beta * jnp.sum(ku[rs], axis=0, keepdims=True)
            m = b[c * cg + last:c * cg + last + 1] + mx
        m_s[d] = m
        c_s[d] = ct
        n_s[d] = n


def _ml_out_kernel(mf_ref, mb_ref, qk_ref, vt_ref, o_ref, gc_ref, gr_ref, cf_ref, cb_ref, nf_ref, nb_ref,
                   tri_ref, gnt_ref, out_ref, *, c0, rows, cg):
    step = pl.program_id(0) + c0
    nsub = rows // cg
    nch = 2 * ML_HEADS
    qk = qk_ref[...]
    vt = vt_ref[...]
    gc = gc_ref[...]
    gr = gr_ref[...]
    ninf = jnp.float32(-jnp.inf)
    chain =lax.broadcasted_iota(jnp.int32, (nch, rows), 0)
    lane = lax.broadcasted_iota(jnp.int32, (nch, rows), 1)
    pos = lane & (cg - 1)
    is_fwd = chain < ML_HEADS

    l3 = jnp.concatenate(_split3(gr[nch:]), axis=0)

    def sum3(bb):
        return bb[:nch] + bb[nch:2 * nch] + bb[2 * nch:]

    b_rows = jnp.where(is_fwd, sum3(_dot_nt(l3, tri_ref[0])), sum3(_dot_nt(l3, tri_ref[1])))
    a_rows = gr[:nch] - b_rows
    g_rows = a_rows
    sh = 1
    while sh < cg:
        xf = jnp.where(pos >= sh, pltpu.roll(g_rows, sh, 1), ninf)
        xb = jnp.where(pos < cg - sh, pltpu.roll(g_rows, rows - sh, 1), ninf)
        g_rows = jnp.maximum(g_rows, jnp.where(is_fwd, xf, xb))
        sh *= 2
    m_rows = jnp.zeros((nch, rows), F32)
    for c in range(nsub):
        in_chunk = (lane >= c * cg) & (lane < (c + 1) * cg)
        for ch in range(nch):
            m_ref = mf_ref if ch < ML_HEADS else mb_ref
            m_rows = jnp.where(in_chunk & (chain == ch), m_ref[step * nsub + c, ch % ML_HEADS], m_rows)
    mt = jnp.maximum(g_rows, m_rows)
    rf = jnp.exp2(g_rows - mt)
    wp = jnp.exp2(m_rows - mt)
    emt = jnp.exp2(-b_rows - mt)

    a_cols = []
    for d in range(2):
        lo = nch + ML_HEADS * d
        a_cols.append(gc[:, ML_HEADS * d:ML_HEADS * (d + 1)] - _cumsum3(tri_ref[d], gc[:, lo:lo + ML_HEADS]))

    ss = lax.broadcasted_iota(jnp.int32, (cg, cg), 0)
    tt = lax.broadcasted_iota(jnp.int32, (cg, cg), 1)
    gnt = gnt_ref[...]
    og = o_ref[...]
    def head_stack(x):
        lane = lax.broadcasted_iota(jnp.int32, x.shape, 1)
        zero = jnp.zeros_like(x)
        return jnp.concatenate([jnp.where((lane >= ML_DK * h) & (lane < ML_DK * (h + 1)), x, zero)
                                for h in range(ML_HEADS)], axis=0)

    for c in range(nsub):
        rs = slice(c * cg, (c + 1) * cg)
        q = qk[rs, :ML_QK]
        st_all = _dot_nt(head_stack(qk[rs, ML_QK:]), q)
        inter_all = [_dot_nt(head_stack(c_ref[c]), q) for c_ref in (cf_ref, cb_ref)]
        nn = [t for x in (nf_ref[c], nb_ref[c]) for t in _split3(x)[:2]]
        dn_all = _dot_nt(head_stack(jnp.concatenate(nn + [jnp.zeros((12, ML_QK), BF16)], axis=0)), q)
        outs = []
        for hh in range(ML_HEADS):
            dv = slice(ML_DV * hh, ML_DV * (hh + 1))
            st = st_all[hh * cg:(hh + 1) * cg]
            dn2 = dn_all[16 * hh:16 * (hh + 1)]
            ht = None
            for d in range(2):
                ch = d * ML_HEADS + hh
                mask = (ss <= tt) if d == 0 else (ss >= tt)
                e = jnp.exp2(jnp.where(mask, a_cols[d][rs, hh:hh + 1] - g_rows[ch:ch + 1, rs], ninf))
                sd = st * e
                den_i = jnp.sum(sd, axis=0, keepdims=True)
                num_t = _dot(vt[dv, rs], sd.astype(BF16))
                inter_t = inter_all[d][dv]
                wpr, rfr = wp[ch:ch + 1, rs], rf[ch:ch + 1, rs]
                den = wpr * (dn2[2 * d:2 * d + 1] + dn2[2 * d + 1:2 * d + 2]) + rfr * den_i
                inv = 1.0 / jnp.maximum(jnp.abs(den), emt[ch:ch + 1, rs])
                part = (wpr * inv) * inter_t + (rfr * inv) * num_t
                ht = part if ht is None else ht + part
            ms = jnp.mean(ht * ht, axis=0, keepdims=True)
            y = (ht * lax.rsqrt(ms + EPS) * gnt[dv]).T
            outs.append(y * jax.nn.sigmoid(og[rs, dv]))
        out_ref[rs, :] = jnp.concatenate(outs, axis=1).astype(BF16)


def _out_rows(lay):
    rows = 512
    while lay.nc % rows or lay.l % rows:
        rows //= 2
    return rows


def _mix_rows(lay):
    return min(256, lay.lc), min(128, lay.lc), min(64, lay.lc)


def _scan_kernel(*refs, rows, ml_cg, hg_cg):
    ml_in, hg_in = refs[0:7], refs[7:14]
    ml_out, hg_out = refs[14:20], refs[20:22]
    ml_scratch, hg_scratch = refs[22:25], refs[25:26]

    @pl.when(pl.program_id(1) == 0)
    def _():
        for s in ml_scratch + hg_scratch:
            s[...] = jnp.zeros_like(s)

    _ml_scan_body(*ml_in, *ml_out, *ml_scratch, rows=rows, cg=ml_cg)
    _hg_scan_body(*hg_in, *hg_out, *hg_scratch, rows=rows, cg=hg_cg)


def _scans(lay, mqk, mvt, gc, hk, hlf, hvt):
    orows, ml_cg, hg_cg = _mix_rows(lay)
    ck = _Chunks(lay, orows)
    ml_sub, hg_sub = orows // ml_cg, orows // hg_cg
    ml_nt, hg_nt = lay.n // ml_cg, lay.n // hg_cg

    def spec(cols, fn, colblk=0):
        return pl.BlockSpec((orows, cols), lambda b, j: (fn(b, j), colblk))

    def tspec(rows_, fn):
        return pl.BlockSpec((rows_, orows), lambda b, j: (0, fn(b, j)))

    def st(shape, fn):
        nd = len(shape)
        return pl.BlockSpec(shape, lambda b, j: (fn(b, j),) + (0,) * (nd - 1))

    def ml_states(fn):
        return [st((ml_sub, ML_DV, ML_QK), fn), st((ml_sub, 1, ML_QK), fn), st((ml_sub, 1, ML_HEADS), fn)]

    ml_shapes = [jax.ShapeDtypeStruct((ml_nt, ML_DV, ML_QK), BF16),
                 jax.ShapeDtypeStruct((ml_nt, 1, ML_QK), F32),
                 jax.ShapeDtypeStruct((ml_nt, 1, ML_HEADS), F32)]
    hg_shape = jax.ShapeDtypeStruct((hg_nt, HG_DV, HG_K), BF16)
    tri_spec = pl.BlockSpec((2, orows, orows), lambda b, j: (0, 0, 0))
    res = pl.pallas_call(
        functools.partial(_scan_kernel, rows=orows, ml_cg=ml_cg, hg_cg=hg_cg),
        grid=(lay.batch, ck.steps),
        in_specs=[spec(ML_QK, ck.fwd, 1), tspec(ML_V, ck.fwd), spec(N_GATE, ck.fwd),
                  spec(ML_QK, ck.bwd, 1), tspec(ML_V, ck.bwd), spec(N_GATE, ck.bwd), tri_spec,
                  spec(HG_K, ck.fwd, 0), spec(HG_K, ck.fwd, 0), tspec(HG_V, ck.fwd),
                  spec(HG_K, ck.bwd, 1), spec(HG_K, ck.bwd, 1), tspec(HG_V, ck.bwd), tri_spec],
        out_specs=ml_states(ck.fwd) + ml_states(ck.bwd)
        + [st((hg_sub, HG_DV, HG_K), ck.fwd), st((hg_sub, HG_DV, HG_K), ck.bwd)],
        out_shape=ml_shapes + ml_shapes + [hg_shape, hg_shape],
        scratch_shapes=[pltpu.VMEM((2, ML_DV, ML_QK), F32),
                        pltpu.VMEM((2, 1, ML_QK), F32),
                        pltpu.VMEM((2, 1, ML_HEADS), F32),
                        pltpu.VMEM((2, HG_DV, HG_K), F32)],
        compiler_params=_params(("parallel", "arbitrary")),
        name="scan",
    )(mqk, mvt, gc, mqk, mvt, gc, _chunk_tri(orows, ml_cg),
      hk, hlf, hvt, hk, hlf, hvt, _chunk_tri(orows, hg_cg))
    return res[:6], res[6:]


def _mlstm_out(lay, states, mqk, mvt, mo, gc, gr, gnorm, with_ctx):
    cf, nf, mf, cb, nb, mb = states
    _, cg, _ = _mix_rows(lay)
    orows = _out_rows(lay)
    nsub = orows // cg
    tri = _chunk_tri(orows, cg)
    nt = lay.n // cg
    c0 = 0 if with_ctx else lay.nc // orows
    nout = lay.n // orows - c0
    smem = pl.BlockSpec(memory_space=pltpu.SMEM)

    def rows(cols):
        return pl.BlockSpec((orows, cols), lambda c: (c + c0, 0))

    cst = pl.BlockSpec((nsub, ML_DV, ML_QK), lambda c: (c + c0, 0, 0))
    nst = pl.BlockSpec((nsub, 1, ML_QK), lambda c: (c + c0, 0, 0))
    out = pl.pallas_call(
        functools.partial(_ml_out_kernel, c0=c0, rows=orows, cg=cg),
        grid=(nout,),
        in_specs=[smem, smem, rows(2 * ML_QK), pl.BlockSpec((ML_V, orows), lambda c: (0, c + c0)),
                  rows(ML_V), rows(N_GATE),
                  pl.BlockSpec((N_GATE, orows), lambda c: (0, c + c0)),
                  cst, cst, nst, nst,
                  pl.BlockSpec((2, orows, orows), lambda c: (0, 0, 0)),
                  pl.BlockSpec((ML_V, 1), lambda c: (0, 0))],
        out_specs=pl.BlockSpec((orows, ML_V), lambda c: (c, 0)),
        out_shape=jax.ShapeDtypeStruct((nout * orows, ML_V), BF16),
        compiler_params=_params(("parallel",)),
        name="mlstm_out",
    )(mf.reshape(nt, ML_HEADS), mb.reshape(nt, ML_HEADS), mqk, mvt, mo, gc, gr, cf, cb, nf, nb, tri,
      gnorm.reshape(ML_V, 1))
    return out


def _chunk_tri(rows, cg):
    t = np.arange(rows)[:, None]
    u = np.arange(rows)[None, :]
    same = (t // cg) == (u // cg)
    return jnp.asarray(np.stack([same & (u <= t), same & (u >= t)]).astype(np.float32)).astype(BF16)


def _cumsum3(tri, x):
    w = x.shape[1]
    hi = x.astype(BF16)
    r1 = x - hi.astype(F32)
    mid = r1.astype(BF16)
    lo = (r1 - mid.astype(F32)).astype(BF16)
    bb = _dot(tri, jnp.concatenate([hi, mid, lo], axis=1))
    return bb[:, :w] + bb[:, w:2 * w] + bb[:, 2 * w:]


def _level_ref(b, bs, d):
    rows, w = b.shape
    off = bs // 2 - 1 + d
    if bs >= 8:
        pieces = [jnp.broadcast_to(b[s + off:s + off + 1], (bs, w)) for s in range(0, rows, bs)]
        return jnp.concatenate(pieces, axis=0)
    b8 = b.reshape(rows // 8, 8, w)
    u = lax.broadcasted_iota(jnp.int32, (1, 8, 1), 1)
    out = None
    for s in range(0, 8, bs):
        piece = jnp.broadcast_to(b8[:, s + off:s + off + 1, :], b8.shape)
        out = piece if out is None else jnp.where(u >= s, piece, out)
    return out.reshape(rows, w)


def _hg_scan_body(kf_ref, lff_ref, vf_ref, kb_ref, lfb_ref, vb_ref, tri_ref, sf_ref, sb_ref, s_s, *, rows, cg):
    nsub = rows // cg
    streams = ((kf_ref, lff_ref, vf_ref, sf_ref), (kb_ref, lfb_ref, vb_ref, sb_ref))
    for d, (k_ref, lf_ref, v_ref, s_out) in enumerate(streams):
        b = _cumsum3(tri_ref[d], lf_ref[...])
        last = cg - 1 if d == 0 else 0
        bls = [b[c * cg + last:c * cg + last + 1] for c in range(nsub)]
        bl_rows = jnp.concatenate([jnp.broadcast_to(bl, (cg, HG_K)) for bl in bls], axis=0)
        kd = (k_ref[...] * jnp.exp2(bl_rows - b)).astype(BF16)
        vt = v_ref[...]
        s = s_s[d]
        for c in (range(nsub) if d == 0 else reversed(range(nsub))):
            rs = slice(c * cg, (c + 1) * cg)
            s_out[c] = s.astype(BF16)
            s = jnp.exp2(bls[c]) * s + _head_outer(vt[:, rs], kd[rs], HG_HEADS)
        s_s[d] = s


def _hg_out_kernel(q_ref, k_ref, lf_ref, v_ref, g_ref, sf_ref, sb_ref, tri_ref, gn_ref, out_ref, b_scr,
                   *, rows, cg, nlev, unroll):
    lff = lf_ref[...]
    nsub = rows // cg
    t_i =lax.broadcasted_iota(jnp.int32, (cg, cg), 0)
    s_i = lax.broadcasted_iota(jnp.int32, (cg, cg), 1)
    xr = t_i ^ s_i
    level = jnp.zeros((cg, cg), jnp.int32)
    for l in range(nlev):
        level = level + (xr >= 2 ** l).astype(jnp.int32)
    assert 2 * HG_DK == 128 and HG_HEADS % 2 == 0
    npair = HG_HEADS // 2
    pairs = [slice(128 * p, 128 * (p + 1)) for p in range(npair)]
    levs = []
    for d in range(2):
        tr = tri_ref.shape[1]
        for r0 in range(0, rows, tr):
            b_scr[d, r0:r0 + tr, :] = _cumsum3(tri_ref[d], lff[r0:r0 + tr, HG_K * d:HG_K * (d + 1)])
        lev_d = jnp.where((t_i > s_i) if d == 0 else (t_i < s_i), level, -1)
        lev_d = jnp.where(t_i == s_i, 0, lev_d)
        levs.append(jnp.concatenate([lev_d, lev_d], axis=1))
    gn = gn_ref[...]
    lo = lax.broadcasted_iota(jnp.int32, (cg, 128), 1) < HG_DK
    zk = jnp.zeros((cg, 128), BF16)

    def split_heads(x):
        return jnp.concatenate([jnp.where(lo, x, zk), jnp.where(lo, zk, x)], axis=0)

    def one_chunk(c):
        rs = pl.ds(pl.multiple_of(c * cg, cg), cg)
        qc = q_ref[rs, :]
        amat = [None] * npair
        inter = [None] * npair
        for d in range(2):
            kc = k_ref[rs, HG_K * d:HG_K * (d + 1)]
            bc = b_scr[d, rs, :]
            a = [jnp.zeros((cg, 2 * cg), F32)] * npair
            for l in range(nlev + 1):
                if l == 0:
                    qt, kt = qc.astype(BF16), kc.astype(BF16)
                else:
                    e = jnp.exp2(-jnp.abs(bc - _level_ref(bc, 2 ** l, d)))
                    qt, kt = (qc * e).astype(BF16), (kc * e).astype(BF16)
                a = [jnp.where(levs[d] == l, _dot_nt(qt[:, pr], split_heads(kt[:, pr])), a[p])
                     for p, pr in enumerate(pairs)]
            qe = (qc * jnp.exp2(bc)).astype(BF16)
            st = (sf_ref if d == 0 else sb_ref)[c]
            it = []
            for pr in pairs:
                qs = split_heads(qe[:, pr])
                r2 = _dot_nt(qs, st[:, pr])
                it.append(jnp.concatenate([r2[:cg], r2[cg:]], axis=1))
            amat = a if d == 0 else [x + y for x, y in zip(amat, a)]
            inter = it if d == 0 else [x + y for x, y in zip(inter, it)]
        outs = []
        for p in range(npair):
            vp = v_ref[rs, 2 * HG_DV * p:2 * HG_DV * (p + 1)]
            zv = jnp.zeros((cg, HG_DV), BF16)
            vbd = jnp.concatenate([jnp.concatenate([vp[:, :HG_DV], zv], axis=1),
                                   jnp.concatenate([zv, vp[:, HG_DV:]], axis=1)], axis=0)
            o = inter[p] + _dot(amat[p].astype(BF16), vbd)
            for hh in (2 * p, 2 * p + 1):
                sl = slice(HG_DV * hh, HG_DV * (hh + 1))
                oh = o[:, HG_DV * (hh - 2 * p):HG_DV * (hh - 2 * p + 1)]
                outs.append(_rms(oh, gn[:, sl]) * _silu(g_ref[rs, sl]))
        out_ref[rs, :] = jnp.concatenate(outs, axis=1).astype(BF16)

    def group(i, carry):
        for u in range(unroll):
            one_chunk(i * unroll + u)
        return carry

    lax.fori_loop(0, nsub // unroll, group, 0)


def _hgrn2_out(lay, states, hq, hk, hlf, hv, hgate, gnorm, with_ctx):
    sf, sb = states
    _, _, cg = _mix_rows(lay)
    orows = _out_rows(lay)
    nsub = orows // cg
    nlev = int(np.log2(cg))
    trows = min(256, orows)
    tri = _chunk_tri(trows, cg)
    c0 = 0 if with_ctx else lay.nc // orows
    nout = lay.n // orows - c0

    def rows(cols):
        return pl.BlockSpec((orows, cols), lambda c: (c + c0, 0))

    sst = pl.BlockSpec((nsub, HG_DV, HG_K), lambda c: (c + c0, 0, 0))
    out = pl.pallas_call(
        functools.partial(_hg_out_kernel, rows=orows, cg=cg, nlev=nlev, unroll=2 if nsub % 2 == 0 else 1),
        grid=(nout,),
        scratch_shapes=[pltpu.VMEM((2, orows, HG_K), F32)],
        in_specs=[rows(HG_K), rows(2 * HG_K), rows(2 * HG_K), rows(HG_V), rows(HG_V), sst, sst,
                  pl.BlockSpec((2, trows, trows), lambda c: (0, 0, 0)),
                  pl.BlockSpec((1, HG_V), lambda c: (0, 0))],
        out_specs=pl.BlockSpec((orows, HG_V), lambda c: (c, 0)),
        out_shape=jax.ShapeDtypeStruct((nout * orows, HG_V), BF16),
        compiler_params=_params(("parallel",)),
        name="hgrn2_out",
    )(hq, hk, hlf, hv, hgate, sf, sb, tri, gnorm)
    return out


def _rope_table(lay):
    l = lay.l
    rows = l // GRID_W
    row = jnp.repeat(jnp.arange(rows, dtype=F32), GRID_W)
    col = jnp.tile(jnp.arange(GRID_W, dtype=F32), rows)
    n_freq = ATT_HEAD_DIM // 4
    inv = ROPE_BASE ** (-jnp.arange(n_freq, dtype=F32) / n_freq)
    ar, ac = row[:, None] * inv, col[:, None] * inv
    cos = jnp.concatenate([jnp.cos(ar), jnp.cos(ar), jnp.cos(ac), jnp.cos(ac)], axis=1)
    sin = jnp.concatenate([-jnp.sin(ar), jnp.sin(ar), -jnp.sin(ac), jnp.sin(ac)], axis=1)
    lat = jnp.concatenate([cos, cos, sin, sin], axis=1)
    lat = jnp.tile(lat, (lay.batch, 1))
    ctx = jnp.concatenate([jnp.ones((lay.nc, 128), F32), jnp.zeros((lay.nc, 128), F32)], axis=1)
    return jnp.concatenate([ctx, lat], axis=0)


def kernel(x, c, ctx, c_ctx, w_ada, b_ada, norm_pre, norm_post, ffn_w1, ffn_w3, ffn_w2, w_in, att_sink,
           ml_conv, ml_f_bias, ml_norm, hg_lb_logits, hg_norm, w_branch_att, w_branch_ml, w_branch_hg,
           w_out):
    batch, l, d = x.shape
    lc = ctx.shape[1]
    depth = w_ada.shape[0]
    assert l % GRID_W == 0 and l & (l - 1) == 0 and lc & (lc - 1) == 0
    assert l % 256 == 0 and lc % 128 == 0 and batch + 1 <= MOD_ROWS
    lay = _Layout(batch, lc, l)
    lay_ffn = _Layout(batch, lc, l, FFN_TM)

    xs = (ctx.reshape(batch * lc, d), x.reshape(batch * l, d))
    cs = jnp.concatenate([c, c_ctx[None, :], jnp.zeros((MOD_ROWS - batch - 1, d), F32)], axis=0)
    mod_all = _modulation(cs, w_ada, b_ada).reshape(depth, MOD_ROWS, N_MOD, d)
    rope = _rope_table(lay)

    lb_all = jnp.cumsum(jax.nn.softmax(hg_lb_logits.astype(F32), axis=0), axis=0)
    lb_all = lb_all - lb_all[0:1]

    sizes = (ATT_Q, ATT_KV, ATT_KV, ML_QK, ML_QK, ML_V, ML_V, 2 * ML_HEADS, 2 * ML_HEADS,
             HG_K, HG_K, HG_K, HG_V, HG_V, 3 * d)
    offs = np.concatenate([[0], np.cumsum(sizes)])
    g0, g1 = int(offs[7]), int(offs[9])

    for layer in range(depth):
        last = layer == depth - 1
        mod = mod_all[layer]
        wl = w_in[layer]
        w_gate = wl[:, g0:g1]
        w_proj = jnp.concatenate(
            [wl[:, :g0], wl[:, g1:int(offs[14])], w_gate, jnp.zeros((d, GATE_PAD - N_GATE), F32)],
            axis=1).astype(BF16)
        w_gate_t = jnp.concatenate([wl[:, int(offs[5]):int(offs[6])], wl[:, int(offs[12]):int(offs[13])],
                                    w_gate], axis=1).T.astype(BF16)
        w_brg = wl[:, int(offs[14]):].astype(BF16)
        fb = ml_f_bias[layer].reshape(1, 2 * ML_HEADS)
        fbrow = jnp.concatenate([jnp.zeros((1, 2 * ML_HEADS), F32), fb], axis=1)
        fbcol = fbrow.reshape(N_GATE, 1)
        lb = lb_all[layer]
        lbv = jnp.stack([jnp.tile(jnp.log(lb), 2), jnp.tile(jnp.log1p(-lb), 2), jnp.tile(1.0 - lb, 2)])

        def npre(i):
            return norm_pre[layer, i].reshape(1, d)

        def npost(i):
            return norm_post[layer, i].reshape(1, d)

        def ffn(xin, j, i, latent_only=False):
            return _ffn(lay_ffn, xin, mod, j, npre(j), npost(j), ffn_w1[layer, i].astype(BF16),
                        ffn_w3[layer, i].astype(BF16), ffn_w2[layer, i].astype(BF16), latent_only)

        xs = ffn(xs, 0, 0)
        (aq, akv, mqk, mo, gc, gr, hq, hk, hlf, hv, hgate, mvt, hvt) = _project(
            lay, xs, mod, npre(1), w_proj, w_gate_t, rope, ml_conv[layer], fbrow, fbcol, lbv)
        att = _attention(lay, aq, akv, att_sink[layer], not last)
        ml_states, hg_states = _scans(lay, mqk, mvt, gc, hk, hlf, hvt)
        ml = _mlstm_out(lay, ml_states, mqk, mvt, mo, gc, gr, ml_norm[layer], not last)
        hg = _hgrn2_out(lay, hg_states, hq, hk, hlf, hv, hgate, hg_norm[layer].reshape(1, HG_V), not last)
        xs = _merge(lay_ffn, xs, att, ml, hg, mod, npre(1), npost(1), w_brg,
                    w_branch_att[layer].astype(BF16), w_branch_ml[layer].astype(BF16),
                    w_branch_hg[layer].astype(BF16), w_out[layer].astype(BF16), last)
        xs = ffn(xs, 2, 1, last)
    return xs.reshape(batch, l, d)
```

```python
import functools

import numpy as np
import jax
import jax.numpy as jnp
from jax import lax
from jax.experimental import pallas as pl
from jax.experimental.pallas import tpu as pltpu

F32 = jnp.float32
BF16 = jnp.bfloat16

ATT_HEADS = 8
ATT_KV_HEADS = 2
ATT_GROUP = ATT_HEADS // ATT_KV_HEADS
ATT_HEAD_DIM = 64
ATT_BLOCK = 128
ML_HEADS = 4
ML_DK = 64
ML_DV = 128
HG_HEADS = 4
HG_DK = 64
HG_DV = 128
GRID_W = 64
ROPE_BASE = 10000.0
EPS = 1e-6
N_MOD = 9
MOD_ROWS = 16

ATT_Q = ATT_HEADS * ATT_HEAD_DIM
ATT_KV = ATT_KV_HEADS * ATT_HEAD_DIM
ML_QK = ML_HEADS * ML_DK
ML_V = ML_HEADS * ML_DV
HG_K = HG_HEADS * HG_DK
HG_V = HG_HEADS * HG_DV
N_GATE = 4 * ML_HEADS
GATE_PAD = 128

OFF_ATT = 0
OFF_MQK = OFF_ATT + ATT_Q + 2 * ATT_KV
OFF_MVO = OFF_MQK + 2 * ML_QK
OFF_HG = OFF_MVO + 2 * ML_V
OFF_GATE = OFF_HG + 3 * HG_K + 2 * HG_V
W_PROJ_COLS = OFF_GATE + GATE_PAD

LOG2E = 1.4426950408889634
VMEM_LIMIT = 56 * 1024 * 1024
FFN_TM = 1024
FFN_SPLIT = 11


def _dot(a, b):
    return jnp.dot(a, b, preferred_element_type=F32)


def _dot_nt(a, b):
    return lax.dot_general(a, b, (((1,), (1,)), ((), ())), preferred_element_type=F32)


def _dot_tn(a, b):
    return lax.dot_general(a, b, (((0,), (0,)), ((), ())), preferred_element_type=F32)


def _rms(x, g):
    ms = jnp.mean(x * x, axis=-1, keepdims=True)
    return x * lax.rsqrt(ms + EPS) * g


def _silu(x):
    return x * jax.nn.sigmoid(x)


def _log_sigmoid(x):
    return jnp.minimum(x, 0.0) - jnp.log(1.0 + jnp.exp(-jnp.abs(x)))


def _head_lanes(x, w):
    r, h = x.shape
    return jnp.concatenate([jnp.broadcast_to(x[:, i:i + 1], (r, w)) for i in range(h)], axis=1)


def _split3(x):
    hi = x.astype(BF16)
    r1 = x - hi.astype(F32)
    mid = r1.astype(BF16)
    return hi, mid, (r1 - mid.astype(F32)).astype(BF16)


def _head_outer(vt, k, heads):
    dv, dk = vt.shape[0] // heads, k.shape[1] // heads
    assert heads % 2 == 0 and 2 * dk == 128
    lo = lax.broadcasted_iota(jnp.int32, (dv, 2 * dk), 1) < dk
    outs = []
    for p in range(heads // 2):
        big = _dot(vt[2 * p * dv:(2 * p + 2) * dv], k[:, 2 * p * dk:(2 * p + 2) * dk])
        outs.append(jnp.where(lo, big[:dv], big[dv:]))
    return jnp.concatenate(outs, axis=1)


def _head_lanes_mxu(x, w):
    h = x.shape[1]
    row = lax.broadcasted_iota(jnp.int32, (3 * h, h * w), 0)
    lane = lax.broadcasted_iota(jnp.int32, (3 * h, h * w), 1)
    sel = jnp.zeros((3 * h, h * w), jnp.bool_)
    for t in range(3):
        for i in range(h):
            sel = sel | ((row == t * h + i) & (lane >= i * w) & (lane < (i + 1) * w))
    return _dot(jnp.concatenate(_split3(x), axis=1), jnp.where(sel, 1.0, 0.0).astype(BF16))


def _params(sem):
    return pltpu.CompilerParams(dimension_semantics=sem, vmem_limit_bytes=VMEM_LIMIT)


def _resident(shape):
    nd = len(shape)
    return pl.BlockSpec(shape, lambda *_: (0,) * nd, pipeline_mode=pl.Buffered(1))


def _mod_kernel(c_ref, w_ref, b_ref, o_ref):
    s = _silu(c_ref[...]).astype(BF16)
    o_ref[0] = _dot(s, w_ref[0].astype(BF16)) + b_ref[0]


def _modulation(cs, w_ada, b_ada):
    depth, d, nm = w_ada.shape
    tn = 1024
    return pl.pallas_call(
        _mod_kernel,
        grid=(depth, nm // tn),
        in_specs=[
            pl.BlockSpec((MOD_ROWS, d), lambda l, n: (0, 0)),
            pl.BlockSpec((1, d, tn), lambda l, n: (l, 0, n)),
            pl.BlockSpec((1, 1, tn), lambda l, n: (l, 0, n)),
        ],
        out_specs=pl.BlockSpec((1, MOD_ROWS, tn), lambda l, n: (l, 0, n)),
        out_shape=jax.ShapeDtypeStruct((depth, MOD_ROWS, nm), F32),
        compiler_params=_params(("parallel", "parallel")),
        name="modulation",
    )(cs, w_ada, b_ada.reshape(depth, 1, nm))


class _Layout:
    def __init__(self, batch, lc, l, tm=512):
        self.batch, self.lc, self.l = batch, lc, l
        self.nc = batch * lc
        self.n = self.nc + batch * l
        while self.nc % tm or l % tm:
            tm //= 2
        self.tm = tm
        self.nct = self.nc // tm
        self.tpb = l // tm
        self.ntiles = self.n // tm

    def mod_index(self, t):
        return jnp.where(t < self.nct, self.batch, (t - self.nct) // self.tpb)


def _ffn_split_kernel(c_ref, x_ref, *rest, nct, **kw):
    _ffn_body(jnp.where(pl.program_id(0) < nct, c_ref[...], x_ref[...]), *rest, **kw)


def _ffn_kernel(x_ref, *rest, **kw):
    _ffn_body(x_ref[...], *rest, **kw)


def _ffn_body(x, mod_ref, gpre_ref, gpost_ref, w1_ref, w3_ref, w2_ref, o_ref, *, j, halves):
    mod = mod_ref[0]
    shift, scale, gate = mod[3 * j:3 * j + 1], mod[3 * j + 1:3 * j + 2], mod[3 * j + 2:3 * j + 3]
    h = (_rms(x, gpre_ref[...] * (1.0 + scale)) + shift).astype(BF16)
    dff = w1_ref.shape[1]
    step = dff // halves
    y = None
    for c in range(halves):
        sl = slice(c * step, (c + 1) * step)
        a = _dot(h, w1_ref[:, sl])
        b = _dot(h, w3_ref[:, sl])
        part = _dot((_silu(a) * b).astype(BF16), w2_ref[sl, :])
        y = part if y is None else y + part
    o_ref[...] = x + 0.5 * gate * _rms(y, gpost_ref[...])


def _ffn(lay, xin, mod, j, gpre, gpost, w1, w3, w2, latent_only=False):
    d, dff = w1.shape
    tm = lay.tm
    halves = FFN_SPLIT if (dff // FFN_SPLIT) % 128 == 0 else 1
    tile0 = lay.nct if latent_only else 0
    ntiles = lay.ntiles - tile0
    row = pl.BlockSpec((tm, d), lambda t: (t, 0))
    if isinstance(xin, tuple):
        body = functools.partial(_ffn_split_kernel, nct=lay.nct, j=j, halves=halves)
        xs_specs = [pl.BlockSpec((tm, d), lambda t: (jnp.minimum(t, lay.nct - 1), 0)),
                    pl.BlockSpec((tm, d), lambda t: (jnp.maximum(t - lay.nct, 0), 0))]
    else:
        body = functools.partial(_ffn_kernel, j=j, halves=halves)
        xs_specs, xin = [row], (xin,)
    return pl.pallas_call(
        body,
        grid=(ntiles,),
        in_specs=xs_specs + [
            pl.BlockSpec((1, N_MOD, d), lambda t: (lay.mod_index(t + tile0), 0, 0)),
            _resident((1, d)), _resident((1, d)),
            _resident((d, dff)), _resident((d, dff)), _resident((dff, d)),
        ],
        out_specs=row,
        out_shape=jax.ShapeDtypeStruct((ntiles * tm, d), F32),
        compiler_params=_params(("parallel",)),
        name="ffn",
    )(*xin, mod, gpre, gpost, w1, w3, w2)


def _proj_kernel(x_ref, xp_ref, xn_ref, mod_ref, gpre_ref, w_ref, wgt_ref, rope_ref, conv_ref,
                 fbrow_ref, fbcol_ref, lbv_ref,
                 aq_ref, akv_ref, mqk_ref, mo_ref, gc_ref, gr_ref,
                 hq_ref, hk_ref, hlf_ref, hv_ref, hg_ref, mvt_ref, hvt_ref, *, nct, lc, l, tm):
    t = pl.program_id(0)
    mod = mod_ref[0]
    shift, scale = mod[3:4], mod[4:5]
    gpre = gpre_ref[...]

    gain = gpre * (1.0 + scale)

    def pre(xx):
        return (_rms(xx, gain) + shift).astype(BF16)

    h = pre(x_ref[...])

    pa = _dot(h, w_ref[:, OFF_ATT:OFF_MQK])
    rope = rope_ref[...]
    cos, sin = rope[:, :128], rope[:, 128:]

    def swap(v):
        w = v.shape[1]
        lane = lax.broadcasted_iota(jnp.int32, v.shape, 1)
        return jnp.where((lane & 16) == 0, pltpu.roll(v, w - 16, 1), pltpu.roll(v, 16, 1))

    q = pa[:, :ATT_Q]
    k = pa[:, ATT_Q:ATT_Q + ATT_KV]
    cos4 = jnp.concatenate([cos] * (ATT_Q // 128), axis=1)
    sin4 = jnp.concatenate([sin] * (ATT_Q // 128), axis=1)
    q = (q * cos4 + swap(q) * sin4) * (ATT_HEAD_DIM ** -0.5 * LOG2E)
    k = k * cos + swap(k) * sin
    aq_ref[...] = q.astype(BF16)
    akv_ref[...] = jnp.concatenate([k, pa[:, ATT_Q + ATT_KV:]], axis=1).astype(BF16)

    wm = w_ref[:, OFF_MQK:OFF_MVO]
    pm = _dot(h, wm)
    pprev = _dot(pre(xp_ref[...]), wm)[7:8]
    pnext = _dot(pre(xn_ref[...]), wm)[0:1]
    r = lax.broadcasted_iota(jnp.int32, (tm, 1), 0)
    grow = t * tm + r
    is_ctx = t < nct
    pos = jnp.where(is_ctx, grow & (lc - 1), (grow - nct * tm) & (l - 1))
    last = jnp.where(is_ctx, lc - 1, l - 1)
    dn = jnp.where(r == 0, pprev, pltpu.roll(pm, 1, 0))
    dn = jnp.where(pos == 0, 0.0, dn)
    up = jnp.where(r == tm - 1, pnext, pltpu.roll(pm, tm - 1, 0))
    up = jnp.where(pos == last, 0.0, up)
    cw = conv_ref[...]
    cv = dn * cw[0:1] + pm * cw[1:2] + up * cw[2:3]
    lane = lax.broadcasted_iota(jnp.int32, (1, 2 * ML_QK), 1)
    mqk_ref[...] = (_silu(cv) * jnp.where(lane < ML_QK, 1.0, ML_DK ** -0.5)).astype(BF16)

    mo_ref[...] = _dot(h, w_ref[:, OFF_MVO + ML_V:OFF_HG])

    pg = _dot(h, w_ref[:, OFF_GATE:OFF_GATE + GATE_PAD])[:, :N_GATE] + fbrow_ref[...]
    lane16 = lax.broadcasted_iota(jnp.int32, (1, N_GATE), 1)
    gc_ref[...] = jnp.where(lane16 < 2 * ML_HEADS, pg, _log_sigmoid(pg)) * LOG2E
    pt = _dot_nt(wgt_ref[...], h)
    mvt_ref[...] = pt[:ML_V].astype(BF16)
    hvt_ref[...] = pt[ML_V:ML_V + HG_V].astype(BF16)
    pgr = pt[ML_V + HG_V:] + fbcol_ref[...]
    row16 = lax.broadcasted_iota(jnp.int32, (N_GATE, 1), 0)
    gr_ref[...] = jnp.where(row16 < 2 * ML_HEADS, pgr, _log_sigmoid(pgr)) * LOG2E

    ph = _dot(h, w_ref[:, OFF_HG:OFF_GATE])
    hq_ref[...] = _silu(ph[:, :HG_K]) * (HG_DK ** -0.5)
    z = ph[:, HG_K:3 * HG_K]
    lbv = lbv_ref[...]
    log_lb, log_1m_lb, one_m_lb = lbv[0:1], lbv[1:2], lbv[2:3]
    ls = _log_sigmoid(z)
    bv = log_1m_lb + ls
    hlf_ref[...] = (jnp.maximum(log_lb, bv) + jnp.log(1.0 + jnp.exp(-jnp.abs(log_lb - bv)))) * LOG2E
    hk_ref[...] = one_m_lb * jnp.exp(ls - z)
    hv_ref[...] = ph[:, 3 * HG_K:3 * HG_K + HG_V].astype(BF16)
    hg_ref[...] = ph[:, 3 * HG_K + HG_V:]


def _project(lay, x, mod, gpre, w, wgt, rope, conv, fbrow, fbcol, lbv):
    n, d = x.shape
    tm = lay.tm
    nb8 = n // 8

    def row(c):
        return pl.BlockSpec((tm, c), lambda t: (t, 0))

    out_cols = [(ATT_Q, BF16), (2 * ATT_KV, BF16), (2 * ML_QK, BF16), (ML_V, F32),
                (N_GATE, F32), (N_GATE, F32, None), (HG_K, F32), (2 * HG_K, F32), (2 * HG_K, F32),
                (HG_V, BF16), (HG_V, F32), (ML_V, BF16, None), (HG_V, BF16, None)]
    out_specs, out_shape = [], []
    for oc in out_cols:
        if len(oc) == 3:
            out_specs.append(pl.BlockSpec((oc[0], tm), lambda t: (0, t)))
            out_shape.append(jax.ShapeDtypeStruct((oc[0], n), oc[1]))
        else:
            out_specs.append(row(oc[0]))
            out_shape.append(jax.ShapeDtypeStruct((n, oc[0]), oc[1]))
    return pl.pallas_call(
        functools.partial(_proj_kernel, nct=lay.nct, lc=lay.lc, l=lay.l, tm=tm),
        grid=(lay.ntiles,),
        in_specs=[
            row(d),
            pl.BlockSpec((8, d), lambda t: (jnp.maximum(t * (tm // 8) - 1, 0), 0)),
            pl.BlockSpec((8, d), lambda t: (jnp.minimum((t + 1) * (tm // 8), nb8 - 1), 0)),
            pl.BlockSpec((1, N_MOD, d), lambda t: (lay.mod_index(t), 0, 0)),
            _resident((1, d)),
            _resident(w.shape), _resident(wgt.shape),
            row(256),
            _resident(conv.shape), _resident(fbrow.shape), _resident(fbcol.shape), _resident(lbv.shape),
        ],
        out_specs=out_specs,
        out_shape=out_shape,
        compiler_params=_params(("parallel",)),
        name="project",
    )(x, x, x, mod, gpre, w, wgt, rope, conv, fbrow, fbcol, lbv)


def _merge_kernel(x_ref, att_ref, ml_ref, hg_ref, mod_ref, gpre_ref, gpost_ref,
                  wg_ref, wa_ref, wm_ref, wh_ref, wo_ref, o_ref):
    x = x_ref[...]
    d = x.shape[1]
    mod = mod_ref[0]
    shift, scale, gate = mod[3:4], mod[4:5], mod[5:6]
    h = (_rms(x, gpre_ref[...] * (1.0 + scale)) + shift).astype(BF16)
    y = None
    for i, (b_ref, w_ref) in enumerate(((att_ref, wa_ref), (ml_ref, wm_ref), (hg_ref, wh_ref))):
        g = jax.nn.sigmoid(_dot(h, wg_ref[:, i * d:(i + 1) * d]))
        part = g * _dot(b_ref[...], w_ref[...])
        y = part if y is None else y + part
    yy = _dot(y.astype(BF16), wo_ref[...])
    o_ref[...] = x + gate * _rms(yy, gpost_ref[...])


def _merge(lay, x, att, ml, hg, mod, gpre, gpost, wg, wa, wm, wh, wo, latent_only=False):
    d = x.shape[1]
    tm = lay.tm
    tile0 = lay.nct if latent_only else 0
    ntiles = lay.ntiles - tile0

    def row(c, off=0):
        return pl.BlockSpec((tm, c), lambda t: (t + off, 0))

    return pl.pallas_call(
        _merge_kernel,
        grid=(ntiles,),
        in_specs=[
            row(d, tile0), row(ATT_Q), row(ML_V), row(HG_V),
            pl.BlockSpec((1, N_MOD, d), lambda t: (lay.mod_index(t + tile0), 0, 0)),
            _resident((1, d)), _resident((1, d)),
            _resident(wg.shape), _resident(wa.shape), _resident(wm.shape), _resident(wh.shape),
            _resident(wo.shape),
        ],
        out_specs=row(d),
        out_shape=jax.ShapeDtypeStruct((ntiles * tm, d), F32),
        compiler_params=_params(("parallel",)),
        name="merge",
    )(x, att, ml, hg, mod, gpre, gpost, wg, wa, wm, wh, wo)


def _swap_lane_halves(x):
    return pltpu.bitcast(pltpu.roll(pltpu.bitcast(x, jnp.uint32), 64, 1), BF16)


def _attend(q, kv, bias, sink_ref):
    tq = q.shape[0]
    assert ATT_GROUP == 4 and 2 * ATT_HEAD_DIM == 128 and ATT_KV == 128
    k, v = kv[:, :ATT_KV], kv[:, ATT_KV:]
    ks, vs = _swap_lane_halves(k), _swap_lane_halves(v)
    lo_k = lax.broadcasted_iota(jnp.int32, k.shape, 1) < ATT_HEAD_DIM
    lo_q = lax.broadcasted_iota(jnp.int32, (tq, 128), 1) < ATT_HEAD_DIM
    row = lax.broadcasted_iota(jnp.int32, (4 * tq, 1), 0)
    zero = jnp.zeros_like(k)
    zq = jnp.zeros((tq, 128), BF16)
    bias4 = None if bias is None else jnp.concatenate([bias] * 4, axis=0)
    outs = []
    for g in range(ATT_KV_HEADS):
        own, other = (k, ks) if g == 0 else (ks, k)
        k2 = jnp.where(lo_k, own, other)
        vown, voth = (v, vs) if g == 0 else (vs, v)
        va = jnp.where(lo_k, vown, zero)
        vb = jnp.where(lo_k, zero, voth)
        p0 = q[:, 256 * g:256 * g + 128]
        p1 = q[:, 256 * g + 128:256 * g + 256]
        qg = jnp.concatenate([jnp.where(lo_q, p0, zq), jnp.where(lo_q, p1, zq),
                              jnp.where(lo_q, zq, p0), jnp.where(lo_q, zq, p1)], axis=0)
        s = _dot_nt(qg, k2)
        if bias4 is not None:
            nbk = bias4.shape[1]
            s = jnp.concatenate([s[:, :nbk] + bias4, s[:, nbk:]], axis=1)
        h0 = 4 * g
        sk = jnp.where(row < tq, sink_ref[h0],
                       jnp.where(row < 2 * tq, sink_ref[h0 + 2],
                                 jnp.where(row < 3 * tq, sink_ref[h0 + 1], sink_ref[h0 + 3]))) * LOG2E
        m = jnp.maximum(jnp.max(s, axis=1, keepdims=True), sk)
        p = jnp.exp2(s - m)
        inv = 1.0 / (jnp.sum(p, axis=1, keepdims=True) + jnp.exp2(sk - m))
        pb = p.astype(BF16)
        o = _dot(pb[:2 * tq], va) * inv[:2 * tq] + _dot(pb[2 * tq:], vb) * inv[2 * tq:]
        outs += [o[:tq], o[tq:]]
    return jnp.concatenate(outs, axis=1).astype(BF16)


def _attn_kernel(sink_ref, q_ref, *refs, nb, nq, nlat):
    band, kc_ref, o_ref = refs[:nq + 2], refs[nq + 2], refs[nq + 3]
    i = pl.program_id(1)

    @pl.when(i < nlat)
    def _():
        r = lax.broadcasted_iota(jnp.int32, (ATT_BLOCK, ATT_BLOCK), 0)
        j = lax.broadcasted_iota(jnp.int32, (ATT_BLOCK, ATT_BLOCK), 1)
        ninf = jnp.float32(-jnp.inf)
        kc = kc_ref[...]
        for a in range(nq):
            blk = i * nq + a
            kv = jnp.concatenate([band[a][...], band[a + 2][...], band[a + 1][...], kc], axis=0)
            left = jnp.where((j >= r) & (blk > 0), 0.0, ninf)
            right = jnp.where((j <= r) & (blk < nb - 1), 0.0, ninf)
            rows = slice(a * ATT_BLOCK, (a + 1) * ATT_BLOCK)
            o_ref[rows, :] = _attend(q_ref[rows, :], kv, jnp.concatenate([left, right], axis=1), sink_ref)

    @pl.when(i >= nlat)
    def _():
        o_ref[...] = _attend(q_ref[...], kc_ref[...], None, sink_ref)


def _attention(lay, aq, akv, sink, with_ctx):
    nb = lay.l // ATT_BLOCK
    lc = lay.lc
    nq = 2 if (lc % (2 * ATT_BLOCK) == 0 and nb % 2 == 0) else 1
    step = nq * ATT_BLOCK
    nlat = nb // nq
    base = lay.nc // step
    nctx = lc // step if with_ctx else 0
    row0 = base if with_ctx else 0
    smem = pl.BlockSpec(memory_space=pltpu.SMEM)
    kvw = 2 * ATT_KV
    kbase = lay.nc // ATT_BLOCK

    def band(off):
        return pl.BlockSpec(
            (ATT_BLOCK, kvw),
            lambda b, i: (kbase + b * nb + jnp.clip(jnp.minimum(i, nlat - 1) * nq + off, 0, nb - 1), 0))

    def qrow(b, i):
        return jnp.where(i < nlat, base + b * nlat + i, b * nctx + (i - nlat))

    def orow(b, i):
        return jnp.where(i < nlat, row0 + b * nlat + i, b * nctx + (i - nlat))

    return pl.pallas_call(
        functools.partial(_attn_kernel, nb=nb, nq=nq, nlat=nlat),
        grid=(lay.batch, nlat + nctx),
        in_specs=[smem,
                  pl.BlockSpec((step, ATT_Q), lambda b, i: (qrow(b, i), 0))]
        + [band(off) for off in range(-1, nq + 1)]
        + [pl.BlockSpec((lc, kvw), lambda b, i: (b, 0))],
        out_specs=pl.BlockSpec((step, ATT_Q), lambda b, i: (orow(b, i), 0)),
        out_shape=jax.ShapeDtypeStruct(((row0 + lay.batch * nlat) * step, ATT_Q), BF16),
        compiler_params=_params(("parallel", "parallel")),
        name="attention",
    )(sink, aq, *([akv] * (nq + 3)))


class _Chunks:
    def __init__(self, lay, ch):
        self.ch = ch
        self.nctx = lay.lc // ch
        self.nlat = lay.l // ch
        self.base = lay.nc // ch
        self.steps = self.nctx + self.nlat
        self.total = lay.n // ch

    def fwd(self, b, j):
        return jnp.where(j < self.nctx, b * self.nctx + j, self.base + b * self.nlat + (j - self.nctx))

    def bwd(self, b, j):
        return jnp.where(j < self.nctx, b * self.nctx + (self.nctx - 1 - j),
                         self.base + b * self.nlat + (self.nlat - 1 - (j - self.nctx)))


def _ml_scan_body(kf_ref, vf_ref, gf_ref, kb_ref, vb_ref, gb_ref, tri_ref,
                  cf_ref, nf_ref, mf_ref, cb_ref, nb_ref, mb_ref, c_s, n_s, m_s, *, rows, cg):
    nsub = rows // cg
    streams = ((kf_ref, vf_ref, gf_ref, cf_ref, nf_ref, mf_ref),
               (kb_ref, vb_ref, gb_ref, cb_ref, nb_ref, mb_ref))
    for d, (k_ref, v_ref, g_ref, c_out, n_out, m_out) in enumerate(streams):
        gc = g_ref[...]
        ig = gc[:, ML_HEADS * d:ML_HEADS * (d + 1)]
        lf = gc[:, 2 * ML_HEADS + ML_HEADS * d:2 * ML_HEADS + ML_HEADS * (d + 1)]
        b = _cumsum3(tri_ref[d], lf)
        a = ig - b
        last = cg - 1 if d == 0 else 0
        gmaxs = [jnp.max(a[c * cg:(c + 1) * cg], axis=0, keepdims=True) for c in range(nsub)]
        rel = a - jnp.concatenate([jnp.broadcast_to(g, (cg, ML_HEADS)) for g in gmaxs], axis=0)
        ku = k_ref[...].astype(F32) * jnp.exp2(_head_lanes_mxu(rel, ML_DK))
        kub = ku.astype(BF16)
        vt = v_ref[...]
        m = m_s[d]
        ct = c_s[d]
        n = n_s[d]
        for c in (range(nsub) if d == 0 else reversed(range(nsub))):
            rs = slice(c * cg, (c + 1) * cg)
            mx = jnp.maximum(m, gmaxs[c])
            alpha = _head_lanes(jnp.exp2(m - mx), ML_DK)
            beta = _head_lanes(jnp.exp2(gmaxs[c] - mx), ML_DK)
            m_out[c] = m
            c_out[c] = ct.astype(BF16)
            n_out[c] = n
            ct = alpha * ct + beta * _head_outer(vt[:, rs], kub[rs], ML_HEADS)
            n = alpha * n + beta * jnp.sum(ku[rs], axis=0, keepdims=True)
            m = b[c * cg + last:c * cg + last + 1] + mx
        m_s[d] = m
        c_s[d] = ct
        n_s[d] = n


def _ml_out_kernel(mf_ref, mb_ref, qk_ref, vt_ref, o_ref, gc_ref, gr_ref, cf_ref, cb_ref, nf_ref, nb_ref,
                   tri_ref, gnt_ref, out_ref, *, c0, rows, cg):
    step = pl.program_id(0) + c0
    nsub = rows // cg
    nch = 2 * ML_HEADS
    qk = qk_ref[...]
    vt = vt_ref[...]
    gc = gc_ref[...]
    gr = gr_ref[...]
    ninf = jnp.float32(-jnp.inf)
    chain =lax.broadcasted_iota(jnp.int32, (nch, rows), 0)
    lane = lax.broadcasted_iota(jnp.int32, (nch, rows), 1)
    pos = lane & (cg - 1)
    is_fwd = chain < ML_HEADS

    l3 = jnp.concatenate(_split3(gr[nch:]), axis=0)

    def sum3(bb):
        return bb[:nch] + bb[nch:2 * nch] + bb[2 * nch:]

    b_rows = jnp.where(is_fwd, sum3(_dot_nt(l3, tri_ref[0])), sum3(_dot_nt(l3, tri_ref[1])))
    a_rows = gr[:nch] - b_rows
    g_rows = a_rows
    sh = 1
    while sh < cg:
        xf = jnp.where(pos >= sh, pltpu.roll(g_rows, sh, 1), ninf)
        xb = jnp.where(pos < cg - sh, pltpu.roll(g_rows, rows - sh, 1), ninf)
        g_rows = jnp.maximum(g_rows, jnp.where(is_fwd, xf, xb))
        sh *= 2
    m_rows = jnp.zeros((nch, rows), F32)
    for c in range(nsub):
        in_chunk = (lane >= c * cg) & (lane < (c + 1) * cg)
        for ch in range(nch):
            m_ref = mf_ref if ch < ML_HEADS else mb_ref
            m_rows = jnp.where(in_chunk & (chain == ch), m_ref[step * nsub + c, ch % ML_HEADS], m_rows)
    mt = jnp.maximum(g_rows, m_rows)
    rf = jnp.exp2(g_rows - mt)
    wp = jnp.exp2(m_rows - mt)
    emt = jnp.exp2(-b_rows - mt)

    a_cols = []
    for d in range(2):
        lo = nch + ML_HEADS * d
        a_cols.append(gc[:, ML_HEADS * d:ML_HEADS * (d + 1)] - _cumsum3(tri_ref[d], gc[:, lo:lo + ML_HEADS]))

    ss = lax.broadcasted_iota(jnp.int32, (cg, cg), 0)
    tt = lax.broadcasted_iota(jnp.int32, (cg, cg), 1)
    gnt = gnt_ref[...]
    og = o_ref[...]
    def head_stack(x):
        lane = lax.broadcasted_iota(jnp.int32, x.shape, 1)
        zero = jnp.zeros_like(x)
        return jnp.concatenate([jnp.where((lane >= ML_DK * h) & (lane < ML_DK * (h + 1)), x, zero)
                                for h in range(ML_HEADS)], axis=0)

    for c in range(nsub):
        rs = slice(c * cg, (c + 1) * cg)
        q = qk[rs, :ML_QK]
        st_all = _dot_nt(head_stack(qk[rs, ML_QK:]), q)
        inter_all = [_dot_nt(head_stack(c_ref[c]), q) for c_ref in (cf_ref, cb_ref)]
        nn = [t for x in (nf_ref[c], nb_ref[c]) for t in _split3(x)[:2]]
        dn_all = _dot_nt(head_stack(jnp.concatenate(nn + [jnp.zeros((12, ML_QK), BF16)], axis=0)), q)
        outs = []
        for hh in range(ML_HEADS):
            dv = slice(ML_DV * hh, ML_DV * (hh + 1))
            st = st_all[hh * cg:(hh + 1) * cg]
            dn2 = dn_all[16 * hh:16 * (hh + 1)]
            ht = None
            for d in range(2):
                ch = d * ML_HEADS + hh
                mask = (ss <= tt) if d == 0 else (ss >= tt)
                e = jnp.exp2(jnp.where(mask, a_cols[d][rs, hh:hh + 1] - g_rows[ch:ch + 1, rs], ninf))
                sd = st * e
                den_i = jnp.sum(sd, axis=0, keepdims=True)
                num_t = _dot(vt[dv, rs], sd.astype(BF16))
                inter_t = inter_all[d][dv]
                wpr, rfr = wp[ch:ch + 1, rs], rf[ch:ch + 1, rs]
                den = wpr * (dn2[2 * d:2 * d + 1] + dn2[2 * d + 1:2 * d + 2]) + rfr * den_i
                inv = 1.0 / jnp.maximum(jnp.abs(den), emt[ch:ch + 1, rs])
                part = (wpr * inv) * inter_t + (rfr * inv) * num_t
                ht = part if ht is None else ht + part
            ms = jnp.mean(ht * ht, axis=0, keepdims=True)
            y = (ht * lax.rsqrt(ms + EPS) * gnt[dv]).T
            outs.append(y * jax.nn.sigmoid(og[rs, dv]))
        out_ref[rs, :] = jnp.concatenate(outs, axis=1).astype(BF16)


def _out_rows(lay):
    rows = 512
    while lay.nc % rows or lay.l % rows:
        rows //= 2
    return rows


def _mix_rows(lay):
    return min(256, lay.lc), min(128, lay.lc), min(64, lay.lc)


def _scan_kernel(*refs, rows, ml_cg, hg_cg):
    ml_in, hg_in = refs[0:7], refs[7:14]
    ml_out, hg_out = refs[14:20], refs[20:22]
    ml_scratch, hg_scratch = refs[22:25], refs[25:26]

    @pl.when(pl.program_id(1) == 0)
    def _():
        for s in ml_scratch + hg_scratch:
            s[...] = jnp.zeros_like(s)

    _ml_scan_body(*ml_in, *ml_out, *ml_scratch, rows=rows, cg=ml_cg)
    _hg_scan_body(*hg_in, *hg_out, *hg_scratch, rows=rows, cg=hg_cg)


def _scans(lay, mqk, mvt, gc, hk, hlf, hvt):
    orows, ml_cg, hg_cg = _mix_rows(lay)
    ck = _Chunks(lay, orows)
    ml_sub, hg_sub = orows // ml_cg, orows // hg_cg
    ml_nt, hg_nt = lay.n // ml_cg, lay.n // hg_cg

    def spec(cols, fn, colblk=0):
        return pl.BlockSpec((orows, cols), lambda b, j: (fn(b, j), colblk))

    def tspec(rows_, fn):
        return pl.BlockSpec((rows_, orows), lambda b, j: (0, fn(b, j)))

    def st(shape, fn):
        nd = len(shape)
        return pl.BlockSpec(shape, lambda b, j: (fn(b, j),) + (0,) * (nd - 1))

    def ml_states(fn):
        return [st((ml_sub, ML_DV, ML_QK), fn), st((ml_sub, 1, ML_QK), fn), st((ml_sub, 1, ML_HEADS), fn)]

    ml_shapes = [jax.ShapeDtypeStruct((ml_nt, ML_DV, ML_QK), BF16),
                 jax.ShapeDtypeStruct((ml_nt, 1, ML_QK), F32),
                 jax.ShapeDtypeStruct((ml_nt, 1, ML_HEADS), F32)]
    hg_shape = jax.ShapeDtypeStruct((hg_nt, HG_DV, HG_K), BF16)
    tri_spec = pl.BlockSpec((2, orows, orows), lambda b, j: (0, 0, 0))
    res = pl.pallas_call(
        functools.partial(_scan_kernel, rows=orows, ml_cg=ml_cg, hg_cg=hg_cg),
        grid=(lay.batch, ck.steps),
        in_specs=[spec(ML_QK, ck.fwd, 1), tspec(ML_V, ck.fwd), spec(N_GATE, ck.fwd),
                  spec(ML_QK, ck.bwd, 1), tspec(ML_V, ck.bwd), spec(N_GATE, ck.bwd), tri_spec,
                  spec(HG_K, ck.fwd, 0), spec(HG_K, ck.fwd, 0), tspec(HG_V, ck.fwd),
                  spec(HG_K, ck.bwd, 1), spec(HG_K, ck.bwd, 1), tspec(HG_V, ck.bwd), tri_spec],
        out_specs=ml_states(ck.fwd) + ml_states(ck.bwd)
        + [st((hg_sub, HG_DV, HG_K), ck.fwd), st((hg_sub, HG_DV, HG_K), ck.bwd)],
        out_shape=ml_shapes + ml_shapes + [hg_shape, hg_shape],
        scratch_shapes=[pltpu.VMEM((2, ML_DV, ML_QK), F32),
                        pltpu.VMEM((2, 1, ML_QK), F32),
                        pltpu.VMEM((2, 1, ML_HEADS), F32),
                        pltpu.VMEM((2, HG_DV, HG_K), F32)],
        compiler_params=_params(("parallel", "arbitrary")),
        name="scan",
    )(mqk, mvt, gc, mqk, mvt, gc, _chunk_tri(orows, ml_cg),
      hk, hlf, hvt, hk, hlf, hvt, _chunk_tri(orows, hg_cg))
    return res[:6], res[6:]


def _mlstm_out(lay, states, mqk, mvt, mo, gc, gr, gnorm, with_ctx):
    cf, nf, mf, cb, nb, mb = states
    _, cg, _ = _mix_rows(lay)
    orows = _out_rows(lay)
    nsub = orows // cg
    tri = _chunk_tri(orows, cg)
    nt = lay.n // cg
    c0 = 0 if with_ctx else lay.nc // orows
    nout = lay.n // orows - c0
    smem = pl.BlockSpec(memory_space=pltpu.SMEM)

    def rows(cols):
        return pl.BlockSpec((orows, cols), lambda c: (c + c0, 0))

    cst = pl.BlockSpec((nsub, ML_DV, ML_QK), lambda c: (c + c0, 0, 0))
    nst = pl.BlockSpec((nsub, 1, ML_QK), lambda c: (c + c0, 0, 0))
    out = pl.pallas_call(
        functools.partial(_ml_out_kernel, c0=c0, rows=orows, cg=cg),
        grid=(nout,),
        in_specs=[smem, smem, rows(2 * ML_QK), pl.BlockSpec((ML_V, orows), lambda c: (0, c + c0)),
                  rows(ML_V), rows(N_GATE),
                  pl.BlockSpec((N_GATE, orows), lambda c: (0, c + c0)),
                  cst, cst, nst, nst,
                  pl.BlockSpec((2, orows, orows), lambda c: (0, 0, 0)),
                  pl.BlockSpec((ML_V, 1), lambda c: (0, 0))],
        out_specs=pl.BlockSpec((orows, ML_V), lambda c: (c, 0)),
        out_shape=jax.ShapeDtypeStruct((nout * orows, ML_V), BF16),
        compiler_params=_params(("parallel",)),
        name="mlstm_out",
    )(mf.reshape(nt, ML_HEADS), mb.reshape(nt, ML_HEADS), mqk, mvt, mo, gc, gr, cf, cb, nf, nb, tri,
      gnorm.reshape(ML_V, 1))
    return out


def _chunk_tri(rows, cg):
    t = np.arange(rows)[:, None]
    u = np.arange(rows)[None, :]
    same = (t // cg) == (u // cg)
    return jnp.asarray(np.stack([same & (u <= t), same & (u >= t)]).astype(np.float32)).astype(BF16)


def _cumsum3(tri, x):
    w = x.shape[1]
    hi = x.astype(BF16)
    r1 = x - hi.astype(F32)
    mid = r1.astype(BF16)
    lo = (r1 - mid.astype(F32)).astype(BF16)
    bb = _dot(tri, jnp.concatenate([hi, mid, lo], axis=1))
    return bb[:, :w] + bb[:, w:2 * w] + bb[:, 2 * w:]


def _level_ref(b, bs, d):
    rows, w = b.shape
    off = bs // 2 - 1 + d
    if bs >= 8:
        pieces = [jnp.broadcast_to(b[s + off:s + off + 1], (bs, w)) for s in range(0, rows, bs)]
        return jnp.concatenate(pieces, axis=0)
    b8 = b.reshape(rows // 8, 8, w)
    u = lax.broadcasted_iota(jnp.int32, (1, 8, 1), 1)
    out = None
    for s in range(0, 8, bs):
        piece = jnp.broadcast_to(b8[:, s + off:s + off + 1, :], b8.shape)
        out = piece if out is None else jnp.where(u >= s, piece, out)
    return out.reshape(rows, w)


def _hg_scan_body(kf_ref, lff_ref, vf_ref, kb_ref, lfb_ref, vb_ref, tri_ref, sf_ref, sb_ref, s_s, *, rows, cg):
    nsub = rows // cg
    streams = ((kf_ref, lff_ref, vf_ref, sf_ref), (kb_ref, lfb_ref, vb_ref, sb_ref))
    for d, (k_ref, lf_ref, v_ref, s_out) in enumerate(streams):
        b = _cumsum3(tri_ref[d], lf_ref[...])
        last = cg - 1 if d == 0 else 0
        bls = [b[c * cg + last:c * cg + last + 1] for c in range(nsub)]
        bl_rows = jnp.concatenate([jnp.broadcast_to(bl, (cg, HG_K)) for bl in bls], axis=0)
        kd = (k_ref[...] * jnp.exp2(bl_rows - b)).astype(BF16)
        vt = v_ref[...]
        s = s_s[d]
        for c in (range(nsub) if d == 0 else reversed(range(nsub))):
            rs = slice(c * cg, (c + 1) * cg)
            s_out[c] = s.astype(BF16)
            s = jnp.exp2(bls[c]) * s + _head_outer(vt[:, rs], kd[rs], HG_HEADS)
        s_s[d] = s


def _hg_out_kernel(q_ref, k_ref, lf_ref, v_ref, g_ref, sf_ref, sb_ref, tri_ref, gn_ref, out_ref,
                   *, rows, cg, nlev):
    q = q_ref[...]
    kk = k_ref[...]
    lff = lf_ref[...]
    v = v_ref[...]
    nsub = rows // cg
    qb = q.astype(BF16)
    t_i = lax.broadcasted_iota(jnp.int32, (cg, cg), 0)
    s_i = lax.broadcasted_iota(jnp.int32, (cg, cg), 1)
    xr = t_i ^ s_i
    level = jnp.zeros((cg, cg), jnp.int32)
    for l in range(nlev):
        level = level + (xr >= 2 ** l).astype(jnp.int32)
    assert 2 * HG_DK == 128 and HG_HEADS % 2 == 0
    npair = HG_HEADS // 2
    pairs = [slice(128 * p, 128 * (p + 1)) for p in range(npair)]
    bs, levs = [], []
    tr = tri_ref.shape[1]
    for d in range(2):
        bs.append(jnp.concatenate(
            [_cumsum3(tri_ref[d], lff[r0:r0 + tr, HG_K * d:HG_K * (d + 1)]) for r0 in range(0, rows, tr)],
            axis=0))
        lev_d = jnp.where((t_i > s_i) if d == 0 else (t_i < s_i), level, -1)
        lev_d = jnp.where(t_i == s_i, 0, lev_d)
        levs.append(jnp.concatenate([lev_d, lev_d], axis=1))
    gn = gn_ref[...]
    gate = g_ref[...]
    lo = lax.broadcasted_iota(jnp.int32, (cg, 128), 1) < HG_DK
    zk = jnp.zeros((cg, 128), BF16)

    def split_heads(x):
        return jnp.concatenate([jnp.where(lo, x, zk), jnp.where(lo, zk, x)], axis=0)

    for c in range(nsub):
        rs = slice(c * cg, (c + 1) * cg)
        qc = q[rs]
        amat = [None] * npair
        inter = [None] * npair
        for d in range(2):
            kc = kk[rs, HG_K * d:HG_K * (d + 1)]
            bc = bs[d][rs]
            a = [jnp.zeros((cg, 2 * cg), F32)] * npair
            for l in range(nlev + 1):
                if l == 0:
                    qt, kt = qb[rs], kc.astype(BF16)
                else:
                    e = jnp.exp2(-jnp.abs(bc - _level_ref(bc, 2 ** l, d)))
                    qt, kt = (qc * e).astype(BF16), (kc * e).astype(BF16)
                a = [jnp.where(levs[d] == l, _dot_nt(qt[:, pr], split_heads(kt[:, pr])), a[p])
                     for p, pr in enumerate(pairs)]
            qe = (qc * jnp.exp2(bc)).astype(BF16)
            st = (sf_ref if d == 0 else sb_ref)[c]
            it = []
            for pr in pairs:
                qs = split_heads(qe[:, pr])
                r2 = _dot_nt(qs, st[:, pr])
                it.append(jnp.concatenate([r2[:cg], r2[cg:]], axis=1))
            amat = a if d == 0 else [x + y for x, y in zip(amat, a)]
            inter = it if d == 0 else [x + y for x, y in zip(inter, it)]
        outs = []
        for p in range(npair):
            vp = v[rs, 2 * HG_DV * p:2 * HG_DV * (p + 1)]
            zv = jnp.zeros((cg, HG_DV), BF16)
            vbd = jnp.concatenate([jnp.concatenate([vp[:, :HG_DV], zv], axis=1),
                                   jnp.concatenate([zv, vp[:, HG_DV:]], axis=1)], axis=0)
            o = inter[p] + _dot(amat[p].astype(BF16), vbd)
            for hh in (2 * p, 2 * p + 1):
                sl = slice(HG_DV * hh, HG_DV * (hh + 1))
                oh = o[:, HG_DV * (hh - 2 * p):HG_DV * (hh - 2 * p + 1)]
                outs.append(_rms(oh, gn[:, sl]) * _silu(gate[rs, sl]))
        out_ref[rs, :] = jnp.concatenate(outs, axis=1).astype(BF16)


def _hgrn2_out(lay, states, hq, hk, hlf, hv, hgate, gnorm, with_ctx):
    sf, sb = states
    _, _, cg = _mix_rows(lay)
    orows = _out_rows(lay)
    nsub = orows // cg
    nlev = int(np.log2(cg))
    trows = min(256, orows)
    tri = _chunk_tri(trows, cg)
    c0 = 0 if with_ctx else lay.nc // orows
    nout = lay.n // orows - c0

    def rows(cols):
        return pl.BlockSpec((orows, cols), lambda c: (c + c0, 0))

    sst = pl.BlockSpec((nsub, HG_DV, HG_K), lambda c: (c + c0, 0, 0))
    out = pl.pallas_call(
        functools.partial(_hg_out_kernel, rows=orows, cg=cg, nlev=nlev),
        grid=(nout,),
        in_specs=[rows(HG_K), rows(2 * HG_K), rows(2 * HG_K), rows(HG_V), rows(HG_V), sst, sst,
                  pl.BlockSpec((2, trows, trows), lambda c: (0, 0, 0)),
                  pl.BlockSpec((1, HG_V), lambda c: (0, 0))],
        out_specs=pl.BlockSpec((orows, HG_V), lambda c: (c, 0)),
        out_shape=jax.ShapeDtypeStruct((nout * orows, HG_V), BF16),
        compiler_params=_params(("parallel",)),
        name="hgrn2_out",
    )(hq, hk, hlf, hv, hgate, sf, sb, tri, gnorm)
    return out


def _rope_table(lay):
    l = lay.l
    rows = l // GRID_W
    row = jnp.repeat(jnp.arange(rows, dtype=F32), GRID_W)
    col = jnp.tile(jnp.arange(GRID_W, dtype=F32), rows)
    n_freq = ATT_HEAD_DIM // 4
    inv = ROPE_BASE ** (-jnp.arange(n_freq, dtype=F32) / n_freq)
    ar, ac = row[:, None] * inv, col[:, None] * inv
    cos = jnp.concatenate([jnp.cos(ar), jnp.cos(ar), jnp.cos(ac), jnp.cos(ac)], axis=1)
    sin = jnp.concatenate([-jnp.sin(ar), jnp.sin(ar), -jnp.sin(ac), jnp.sin(ac)], axis=1)
    lat = jnp.concatenate([cos, cos, sin, sin], axis=1)
    lat = jnp.tile(lat, (lay.batch, 1))
    ctx = jnp.concatenate([jnp.ones((lay.nc, 128), F32), jnp.zeros((lay.nc, 128), F32)], axis=1)
    return jnp.concatenate([ctx, lat], axis=0)


def kernel(x, c, ctx, c_ctx, w_ada, b_ada, norm_pre, norm_post, ffn_w1, ffn_w3, ffn_w2, w_in, att_sink,
           ml_conv, ml_f_bias, ml_norm, hg_lb_logits, hg_norm, w_branch_att, w_branch_ml, w_branch_hg,
           w_out):
    batch, l, d = x.shape
    lc = ctx.shape[1]
    depth = w_ada.shape[0]
    assert l % GRID_W == 0 and l & (l - 1) == 0 and lc & (lc - 1) == 0
    assert l % 256 == 0 and lc % 128 == 0 and batch + 1 <= MOD_ROWS
    lay = _Layout(batch, lc, l)
    lay_ffn = _Layout(batch, lc, l, FFN_TM)

    xs = (ctx.reshape(batch * lc, d), x.reshape(batch * l, d))
    cs = jnp.concatenate([c, c_ctx[None, :], jnp.zeros((MOD_ROWS - batch - 1, d), F32)], axis=0)
    mod_all = _modulation(cs, w_ada, b_ada).reshape(depth, MOD_ROWS, N_MOD, d)
    rope = _rope_table(lay)

    lb_all = jnp.cumsum(jax.nn.softmax(hg_lb_logits.astype(F32), axis=0), axis=0)
    lb_all = lb_all - lb_all[0:1]

    sizes = (ATT_Q, ATT_KV, ATT_KV, ML_QK, ML_QK, ML_V, ML_V, 2 * ML_HEADS, 2 * ML_HEADS,
             HG_K, HG_K, HG_K, HG_V, HG_V, 3 * d)
    offs = np.concatenate([[0], np.cumsum(sizes)])
    g0, g1 = int(offs[7]), int(offs[9])

    for layer in range(depth):
        last = layer == depth - 1
        mod = mod_all[layer]
        wl = w_in[layer]
        w_gate = wl[:, g0:g1]
        w_proj = jnp.concatenate(
            [wl[:, :g0], wl[:, g1:int(offs[14])], w_gate, jnp.zeros((d, GATE_PAD - N_GATE), F32)],
            axis=1).astype(BF16)
        w_gate_t = jnp.concatenate([wl[:, int(offs[5]):int(offs[6])], wl[:, int(offs[12]):int(offs[13])],
                                    w_gate], axis=1).T.astype(BF16)
        w_brg = wl[:, int(offs[14]):].astype(BF16)
        fb = ml_f_bias[layer].reshape(1, 2 * ML_HEADS)
        fbrow = jnp.concatenate([jnp.zeros((1, 2 * ML_HEADS), F32), fb], axis=1)
        fbcol = fbrow.reshape(N_GATE, 1)
        lb = lb_all[layer]
        lbv = jnp.stack([jnp.tile(jnp.log(lb), 2), jnp.tile(jnp.log1p(-lb), 2), jnp.tile(1.0 - lb, 2)])

        def npre(i):
            return norm_pre[layer, i].reshape(1, d)

        def npost(i):
            return norm_post[layer, i].reshape(1, d)

        def ffn(xin, j, i, latent_only=False):
            return _ffn(lay_ffn, xin, mod, j, npre(j), npost(j), ffn_w1[layer, i].astype(BF16),
                        ffn_w3[layer, i].astype(BF16), ffn_w2[layer, i].astype(BF16), latent_only)

        xs = ffn(xs, 0, 0)
        (aq, akv, mqk, mo, gc, gr, hq, hk, hlf, hv, hgate, mvt, hvt) = _project(
            lay, xs, mod, npre(1), w_proj, w_gate_t, rope, ml_conv[layer], fbrow, fbcol, lbv)
        att = _attention(lay, aq, akv, att_sink[layer], not last)
        ml_states, hg_states = _scans(lay, mqk, mvt, gc, hk, hlf, hvt)
        ml = _mlstm_out(lay, ml_states, mqk, mvt, mo, gc, gr, ml_norm[layer], not last)
        hg = _hgrn2_out(lay, hg_states, hq, hk, hlf, hv, hgate, hg_norm[layer].reshape(1, HG_V), not last)
        xs = _merge(lay_ffn, xs, att, ml, hg, mod, npre(1), npost(1), w_brg,
                    w_branch_att[layer].astype(BF16), w_branch_ml[layer].astype(BF16),
                    w_branch_hg[layer].astype(BF16), w_out[layer].astype(BF16), last)
        xs = ffn(xs, 2, 1, last)
    return xs.reshape(batch, l, d)
```
